```python
import math
import jax, jax.numpy as jnp
from jax import lax
import numpy as np

D_MODEL = 1024
BATCH = 8
SEQ = 8192
DEPTH = 2
DEC_BATCH = 16
DEC_SEQ = 64
PAST_LEN = 2048

CHUNK = 64
HEAD_DIM = 64
ROPE_THETA = 10000.0
LN_EPS = 1e-5
D_FF = 4 * D_MODEL

A_WIDTH = D_MODEL // 2
A_GROUPS = 8
A_GROUP_DIM = A_WIDTH // A_GROUPS
A_CHUNK = 128

B_HEADS = (D_MODEL // 2) // HEAD_DIM
B_KV_HEADS = 2
B_GROUP = B_HEADS // B_KV_HEADS
B_WINDOW = 128
B_WIN_CHUNKS = B_WINDOW // CHUNK

C_HEADS = D_MODEL // HEAD_DIM
C_KV_HEADS = 4
C_GROUP = C_HEADS // C_KV_HEADS
IDX_HEADS = 4
IDX_DIM = 64
TOPK_MAX = 256
ATTN_QBLOCK = 128

N_EVEN = (DEPTH + 1) // 2
N_ODD = DEPTH // 2
ALPHA = (2.0 * DEPTH) ** 0.25
BETA = (8.0 * DEPTH) ** -0.25

B_Q = B_HEADS * HEAD_DIM
B_KV = B_KV_HEADS * HEAD_DIM
AB_IN = 2 * A_WIDTH + B_Q + 2 * B_KV
AB_OUT = A_WIDTH + B_Q
AB_SPLITS = (A_WIDTH, 2 * A_WIDTH, 2 * A_WIDTH + B_Q, 2 * A_WIDTH + B_Q + B_KV)

C_Q = C_HEADS * HEAD_DIM
C_KV = C_KV_HEADS * HEAD_DIM
C_QI = IDX_HEADS * IDX_DIM
C_IN = C_Q + 2 * C_KV + C_QI + IDX_DIM + IDX_HEADS
C_OUT = C_Q
C_SPLITS = (C_Q, C_Q + C_KV, C_Q + 2 * C_KV, C_Q + 2 * C_KV + C_QI, C_Q + 2 * C_KV + C_QI + IDX_DIM)

kernel_name = "chunk_causal_gmlp_swa_dsa_encoder_step"


def layer_norm(x, g, b):
    xf = x.astype(jnp.float32)
    mu = xf.mean(-1, keepdims=True)
    var = jnp.square(xf - mu).mean(-1, keepdims=True)
    return ((xf - mu) * lax.rsqrt(var + LN_EPS) * g + b).astype(x.dtype)


def rope(x, pos):
    half = x.shape[-1] // 2
    inv_freq = jnp.exp(-math.log(ROPE_THETA) * jnp.arange(half, dtype=jnp.float32) / half)
    ang = pos.astype(jnp.float32)[:, None] * inv_freq[None, :]
    cos = jnp.cos(ang)[None, :, None, :]
    sin = jnp.sin(ang)[None, :, None, :]
    xf = x.astype(jnp.float32)
    x1, x2 = xf[..., :half], xf[..., half:]
    return jnp.concatenate([x1 * cos - x2 * sin, x2 * cos + x1 * sin], axis=-1).astype(x.dtype)


def gmlp_spatial(v, w_s, b_s):
    t = v.shape[2]
    w = jnp.tril(w_s[:, :t, :t])
    return jnp.einsum('gij,bnjgc->bnigc', w, v) + b_s[:, :t].T[None, None, :, :, None]


def sink_attention(q, k, v, mask, sinks):
    s = jnp.einsum('bcqhgd,bcnhd->bchgqn', q, k).astype(jnp.float32) * HEAD_DIM ** -0.5
    s = jnp.where(mask[None, :, None, None, None, :], s, -jnp.inf)
    sk = sinks.astype(jnp.float32)[None, None, :, :, None, None]
    m = jnp.maximum(s.max(-1, keepdims=True), sk)
    e = jnp.exp(s - m)
    p = e / (e.sum(-1, keepdims=True) + jnp.exp(sk - m))
    return jnp.einsum('bchgqn,bcnhd->bcqhgd', p.astype(v.dtype), v)


def indexer_scores(qi, wi, ki):
    s = jnp.einsum('bqhd,bsd->bqhs', qi, ki).astype(jnp.float32)
    return jnp.einsum('bqhs,bqh->bqs', jax.nn.relu(s), wi.astype(jnp.float32)) * (IDX_DIM ** -0.5 * IDX_HEADS ** -0.5)


def gather_rows(x, idx):
    return jax.vmap(lambda xb, ib: xb[ib])(x, idx)


def selected_attention(q, ksel, vsel, valid):
    s = jnp.einsum('bqhgd,bqnhd->bqhgn', q, ksel).astype(jnp.float32) * HEAD_DIM ** -0.5
    if valid is not None:
        s = jnp.where(valid[:, :, None, None, :], s, -jnp.inf)
    p = jax.nn.softmax(s, axis=-1)
    return jnp.einsum('bqhgn,bqnhd->bqhgd', p.astype(vsel.dtype), vsel)


def ab_mixer(h, pos, w_in, a_ln_g, a_ln_b, a_ws, a_bs, sinks, w_out, cache_k, cache_v):
    bsz, t, _ = h.shape
    u, va, qb, kb, vb = jnp.split(h @ w_in, AB_SPLITS, axis=-1)
    u = jax.nn.gelu(u)
    va = layer_norm(jax.nn.gelu(va), a_ln_g, a_ln_b)
    if cache_k is None:
        vch = va.reshape(bsz, t // A_CHUNK, A_CHUNK, A_GROUPS, A_GROUP_DIM)
    else:
        vch = va.reshape(bsz, 1, t, A_GROUPS, A_GROUP_DIM)
    a_out = u * gmlp_spatial(vch, a_ws, a_bs).reshape(bsz, t, A_WIDTH)
    q = rope(qb.reshape(bsz, t, B_HEADS, HEAD_DIM), pos)
    k = rope(kb.reshape(bsz, t, B_KV_HEADS, HEAD_DIM), pos)
    v = vb.reshape(bsz, t, B_KV_HEADS, HEAD_DIM)
    sk = sinks.reshape(B_KV_HEADS, B_GROUP)
    if cache_k is None:
        nc = t // CHUNK
        qc = q.reshape(bsz, nc, CHUNK, B_KV_HEADS, B_GROUP, HEAD_DIM)
        pad = ((0, 0), (B_WIN_CHUNKS, 0), (0, 0), (0, 0), (0, 0))
        kp = jnp.pad(k.reshape(bsz, nc, CHUNK, B_KV_HEADS, HEAD_DIM), pad)
        vp = jnp.pad(v.reshape(bsz, nc, CHUNK, B_KV_HEADS, HEAD_DIM), pad)
        kband = jnp.concatenate([kp[:, i:i + nc] for i in range(B_WIN_CHUNKS + 1)], axis=2)
        vband = jnp.concatenate([vp[:, i:i + nc] for i in range(B_WIN_CHUNKS + 1)], axis=2)
        key_chunk = (jnp.arange(nc)[:, None] - B_WIN_CHUNKS
                     + jnp.arange((B_WIN_CHUNKS + 1) * CHUNK)[None, :] // CHUNK)
        o = sink_attention(qc, kband, vband, key_chunk >= 0, sk)
        new_k, new_v = k[:, -B_WINDOW:], v[:, -B_WINDOW:]
    else:
        kall = jnp.concatenate([cache_k, k], axis=1)[:, None]
        vall = jnp.concatenate([cache_v, v], axis=1)[:, None]
        qc = q.reshape(bsz, 1, t, B_KV_HEADS, B_GROUP, HEAD_DIM)
        o = sink_attention(qc, kall, vall, jnp.ones((1, kall.shape[2]), dtype=bool), sk)
        new_k, new_v = k, v
    b_out = o.reshape(bsz, t, B_Q)
    y = jnp.concatenate([a_out, b_out], axis=-1) @ w_out
    return y, va, new_k, new_v


def c_mixer(h, pos, w_in, w_out, cache_k, cache_v, cache_i):
    bsz, t, _ = h.shape
    qb, kb, vb, qib, kib, wi = jnp.split(h @ w_in, C_SPLITS, axis=-1)
    q = rope(qb.reshape(bsz, t, C_HEADS, HEAD_DIM), pos).reshape(bsz, t, C_KV_HEADS, C_GROUP, HEAD_DIM)
    k = rope(kb.reshape(bsz, t, C_KV_HEADS, HEAD_DIM), pos)
    v = vb.reshape(bsz, t, C_KV_HEADS, HEAD_DIM)
    qi = rope(qib.reshape(bsz, t, IDX_HEADS, IDX_DIM), pos)
    ki = rope(kib.reshape(bsz, t, 1, IDX_DIM), pos)[:, :, 0]
    if cache_k is None:
        nb = t // ATTN_QBLOCK
        k_sel = min(TOPK_MAX, t // 4)
        key_chunk = jnp.arange(t) // CHUNK

        def block(args):
            qblk, qiblk, wblk, qpos = args
            q_chunk = qpos // CHUNK
            sc = indexer_scores(qiblk, wblk, ki)
            sc = jnp.where((key_chunk[None, :] <= q_chunk[:, None])[None], sc, -jnp.inf)
            _, idx = lax.top_k(sc, k_sel)
            valid = (idx // CHUNK) <= q_chunk[None, :, None]
            return selected_attention(qblk, gather_rows(k, idx), gather_rows(v, idx), valid)

        to_blocks = lambda a: jnp.moveaxis(a.reshape((bsz, nb, ATTN_QBLOCK) + a.shape[2:]), 1, 0)
        o = lax.map(block, (to_blocks(q), to_blocks(qi), to_blocks(wi), pos.reshape(nb, ATTN_QBLOCK)))
        o = jnp.moveaxis(o, 0, 1)
    else:
        kall = jnp.concatenate([cache_k, k], axis=1)
        vall = jnp.concatenate([cache_v, v], axis=1)
        iall = jnp.concatenate([cache_i, ki], axis=1)
        k_sel = min(TOPK_MAX, kall.shape[1] // 4)
        _, idx = lax.top_k(indexer_scores(qi, wi, iall), k_sel)
        o = selected_attention(q, gather_rows(kall, idx), gather_rows(vall, idx), None)
    y = o.reshape(bsz, t, C_OUT) @ w_out
    return y, k, v, ki


def sq_relu_mlp(x, w1, w2):
    return jnp.square(jax.nn.relu(x @ w1)) @ w2


def trunk(x, pos, cache_b_k, cache_b_v, cache_c_k, cache_c_v, cache_c_idx,
          ab_w_in, a_ln_g, a_ln_b, a_ws, a_bs, b_sinks, ab_w_out, c_w_in, c_w_out,
          ln1_g, ln1_b, ln2_g, ln2_b, ff_w1, ff_w2):
    sample = cache_b_k is not None
    a_v, b_k, b_v, c_k, c_v, c_i = [], [], [], [], [], []
    for layer in range(DEPTH):
        i = layer // 2
        if layer % 2 == 0:
            y, va, nk, nv = ab_mixer(x, pos, ab_w_in[i], a_ln_g[i], a_ln_b[i], a_ws[i], a_bs[i], b_sinks[i], ab_w_out[i],
                                     cache_b_k[i] if sample else None, cache_b_v[i] if sample else None)
            a_v.append(va); b_k.append(nk); b_v.append(nv)
        else:
            y, nk, nv, ni = c_mixer(x, pos, c_w_in[i], c_w_out[i],
                                    cache_c_k[i] if sample else None, cache_c_v[i] if sample else None,
                                    cache_c_idx[i] if sample else None)
            c_k.append(nk); c_v.append(nv); c_i.append(ni)
        x = layer_norm(ALPHA * x + y, ln1_g[layer], ln1_b[layer])
        x = layer_norm(ALPHA * x + sq_relu_mlp(x, ff_w1[layer], ff_w2[layer]), ln2_g[layer], ln2_b[layer])
    return (x, jnp.stack(a_v, 0), jnp.stack(b_k, 0), jnp.stack(b_v, 0),
            jnp.stack(c_k, 0), jnp.stack(c_v, 0), jnp.stack(c_i, 0))


def setup_inputs(seed: int = 0) -> dict:
    key = jax.random.key(seed)
    ks = jax.random.split(key, 24)
    nrm = lambda k, shape, scale: jax.random.normal(k, shape, jnp.float32) * scale
    b_cache = min(B_WINDOW, PAST_LEN)
    return {
        "x_prompt": nrm(ks[0], (BATCH, SEQ, D_MODEL), 1.0),
        "x_sample": nrm(ks[1], (DEC_BATCH, DEC_SEQ, D_MODEL), 1.0),
        "cache_b_k": nrm(ks[2], (N_EVEN, DEC_BATCH, b_cache, B_KV_HEADS, HEAD_DIM), 1.0),
        "cache_b_v": nrm(ks[3], (N_EVEN, DEC_BATCH, b_cache, B_KV_HEADS, HEAD_DIM), 1.0),
        "cache_c_k": nrm(ks[4], (N_ODD, DEC_BATCH, PAST_LEN, C_KV_HEADS, HEAD_DIM), 1.0),
        "cache_c_v": nrm(ks[5], (N_ODD, DEC_BATCH, PAST_LEN, C_KV_HEADS, HEAD_DIM), 1.0),
        "cache_c_idx": nrm(ks[6], (N_ODD, DEC_BATCH, PAST_LEN, IDX_DIM), 1.0),
        "ab_w_in": nrm(ks[7], (N_EVEN, D_MODEL, AB_IN), D_MODEL ** -0.5),
        "a_ln_g": 1.0 + nrm(ks[8], (N_EVEN, A_WIDTH), 0.01),
        "a_ln_b": nrm(ks[9], (N_EVEN, A_WIDTH), 0.01),
        "a_ws": nrm(ks[10], (N_EVEN, A_GROUPS, A_CHUNK, A_CHUNK), A_CHUNK ** -0.5),
        "a_bs": 1.0 + nrm(ks[11], (N_EVEN, A_GROUPS, A_CHUNK), 0.01),
        "b_sinks": nrm(ks[12], (N_EVEN, B_HEADS), 0.5),
        "ab_w_out": nrm(ks[13], (N_EVEN, AB_OUT, D_MODEL), BETA * AB_OUT ** -0.5),
        "c_w_in": nrm(ks[14], (N_ODD, D_MODEL, C_IN), D_MODEL ** -0.5),
        "c_w_out": nrm(ks[15], (N_ODD, C_OUT, D_MODEL), BETA * C_OUT ** -0.5),
        "ln1_g": 1.0 + nrm(ks[16], (DEPTH, D_MODEL), 0.01),
        "ln1_b": nrm(ks[17], (DEPTH, D_MODEL), 0.01),
        "ln2_g": 1.0 + nrm(ks[18], (DEPTH, D_MODEL), 0.01),
        "ln2_b": nrm(ks[19], (DEPTH, D_MODEL), 0.01),
        "ff_w1": nrm(ks[20], (DEPTH, D_MODEL, D_FF), D_MODEL ** -0.5),
        "ff_w2": nrm(ks[21], (DEPTH, D_FF, D_MODEL), BETA * D_FF ** -0.5),
    }


def reference(x_prompt, x_sample, cache_b_k, cache_b_v, cache_c_k, cache_c_v, cache_c_idx,
              ab_w_in, a_ln_g, a_ln_b, a_ws, a_bs, b_sinks, ab_w_out, c_w_in, c_w_out,
              ln1_g, ln1_b, ln2_g, ln2_b, ff_w1, ff_w2):
    weights = (ab_w_in, a_ln_g, a_ln_b, a_ws, a_bs, b_sinks, ab_w_out, c_w_in, c_w_out,
               ln1_g, ln1_b, ln2_g, ln2_b, ff_w1, ff_w2)
    pos_p = jnp.arange(x_prompt.shape[1], dtype=jnp.int32)
    pos_s = PAST_LEN + jnp.arange(x_sample.shape[1], dtype=jnp.int32)
    y_prompt, _, p_b_k, p_b_v, p_c_k, p_c_v, p_c_idx = trunk(
        x_prompt, pos_p, None, None, None, None, None, *weights)
    y_sample, s_a_v, s_b_k, s_b_v, s_c_k, s_c_v, s_c_idx = trunk(
        x_sample, pos_s, cache_b_k, cache_b_v, cache_c_k, cache_c_v, cache_c_idx, *weights)
    return (y_prompt, y_sample, p_b_k, p_b_v, p_c_k, p_c_v, p_c_idx,
            s_a_v, s_b_k, s_b_v, s_c_k, s_c_v, s_c_idx)
```

```python
import functools
import math

import numpy as np
import jax
import jax.numpy as jnp
from jax import lax
from jax.experimental import pallas as pl
from jax.experimental.pallas import tpu as pltpu

F32 = jnp.float32
I32 = jnp.int32
MXU_DTYPE = jnp.bfloat16

HEAD_DIM = 64
CHUNK = 64
ROPE_THETA = 10000.0
LN_EPS = 1e-5
DEPTH = 2
ALPHA = (2.0 * DEPTH) ** 0.25
A_GROUPS = 8
A_CHUNK = 128
B_KV_HEADS = 2
B_GROUP = 4
B_WINDOW = 128
C_KV_HEADS = 4
C_GROUP = 4
IDX_HEADS = 4
IDX_DIM = 64
TOPK_MAX = 256

LANES = 128
KEY_TILE = 256
QUERY_BLOCK = 128
INT_MIN = -(2 ** 31)
NEG_BIG = -1e30
VMEM_LIMIT = 56 * 1024 * 1024


def _gelu(x):
    c = math.sqrt(2.0 / math.pi)
    return 0.5 * x * (1.0 + jnp.tanh(c * (x + 0.044715 * (x * x * x))))


def _ln(z, g, b):
    mu = jnp.mean(z, axis=-1, keepdims=True)
    d = z - mu
    var = jnp.mean(d * d, axis=-1, keepdims=True)
    return d * lax.rsqrt(var + LN_EPS) * g + b


def _rope2(x, cos, sin):
    lane = lax.broadcasted_iota(I32, (1, LANES), 1)
    first = (lane % HEAD_DIM) < (HEAD_DIM // 2)
    swapped = jnp.where(first, pltpu.roll(x, LANES - HEAD_DIM // 2, 1), pltpu.roll(x, HEAD_DIM // 2, 1))
    return x * cos + swapped * sin


def _dot(a, b):
    return jnp.dot(a, b, preferred_element_type=F32)


def _dot_nt(a, b):
    return lax.dot_general(a, b, (((1,), (1,)), ((), ())), preferred_element_type=F32)


def _ab_in_kernel(x_ref, w_ref, cos_ref, sin_ref, g_ref, b_ref, u_ref, va_ref, q_ref, k_ref, v_ref):
    y = _dot(x_ref[...].astype(MXU_DTYPE), w_ref[...])
    u_ref[...] = _gelu(y[:, 0:512])
    va_ref[...] = _ln(_gelu(y[:, 512:1024]), g_ref[...], b_ref[...])
    cos = cos_ref[...]
    sin = sin_ref[...]
    for c in range(4):
        lo = 1024 + LANES * c
        q_ref[:, LANES * c:LANES * (c + 1)] = (
            _rope2(y[:, lo:lo + LANES], cos, sin) * (HEAD_DIM ** -0.5)).astype(q_ref.dtype)
    k_ref[...] = _rope2(y[:, 1536:1664], cos, sin)
    v_ref[...] = y[:, 1664:1792]


def _ab_in(x2d, w, cos, sin, g, b, *, t, tm):
    rows = x2d.shape[0]
    nt = t // tm
    row_spec = lambda w_: pl.BlockSpec((tm, w_), lambda i: (i, 0))
    full = lambda a: pl.BlockSpec(a.shape, lambda i: (0,) * a.ndim)
    tab = pl.BlockSpec((tm, LANES), lambda i: (i % nt, 0))
    return pl.pallas_call(
        _ab_in_kernel,
        grid=(rows // tm,),
        in_specs=[row_spec(1024), full(w), tab, tab, full(g), full(b)],
        out_specs=[row_spec(512), row_spec(512), row_spec(512), row_spec(128), row_spec(128)],
        out_shape=[
            jax.ShapeDtypeStruct((rows, 512), F32),
            jax.ShapeDtypeStruct((rows, 512), F32),
            jax.ShapeDtypeStruct((rows, 512), MXU_DTYPE),
            jax.ShapeDtypeStruct((rows, 128), F32),
            jax.ShapeDtypeStruct((rows, 128), F32),
        ],
        compiler_params=pltpu.CompilerParams(
            dimension_semantics=("arbitrary",), vmem_limit_bytes=VMEM_LIMIT),
        name="ab_in",
    )(x2d, w, cos, sin, g, b)


def _ab_mix_kernel(sink_ref, x_ref, u_ref, va_ref, q_ref, k_ref, v_ref, hk_ref, hv_ref,
                   ws_ref, bias_ref, wout_ref, g_ref, b_ref, o_ref, cat_ref, *, rows, cs, mask_first):
    t = pl.program_id(1)
    lane = lax.broadcasted_iota(I32, (1, LANES), 1)
    lo_half = lane < HEAD_DIM

    r_i = lax.broadcasted_iota(I32, (cs, cs), 0)
    c_i = lax.broadcasted_iota(I32, (cs, cs), 1)
    tril = r_i >= c_i
    w_tril = [jnp.where(tril, ws_ref[g], 0.0).astype(MXU_DTYPE) for g in range(A_GROUPS)]
    for c in range(rows // cs):
        rs = slice(c * cs, (c + 1) * cs)
        for p in range(A_GROUPS // 2):
            ls = slice(LANES * p, LANES * (p + 1))
            vp = va_ref[0, rs, ls].astype(MXU_DTYPE)
            gate = jnp.where(lo_half, _dot(w_tril[2 * p], vp), _dot(w_tril[2 * p + 1], vp)) + bias_ref[:, ls]
            cat_ref[rs, ls] = (u_ref[0, rs, ls] * gate).astype(cat_ref.dtype)

    kcat = jnp.concatenate([hk_ref[0], k_ref[0]], axis=0).astype(MXU_DTYPE)
    vcat = jnp.concatenate([hv_ref[0], v_ref[0]], axis=0).astype(MXU_DTYPE)
    nwin = B_WINDOW + CHUNK
    head_of_row = lax.broadcasted_iota(I32, (B_GROUP * CHUNK, 1), 0) // CHUNK
    col = lax.broadcasted_iota(I32, (1, nwin), 1)
    for j in range(rows // CHUNK):
        rs = slice(CHUNK * j, CHUNK * (j + 1))
        kwin = kcat[CHUNK * j:CHUNK * j + nwin]
        vwin = vcat[CHUNK * j:CHUNK * j + nwin]
        outs = []
        for h in range(B_KV_HEADS):
            half = lo_half if h == 0 else jnp.logical_not(lo_half)
            q4 = jnp.concatenate(
                [jnp.where(half, q_ref[0, rs, LANES * g:LANES * (g + 1)], 0).astype(MXU_DTYPE)
                 for g in range(B_GROUP)], axis=0)
            s = _dot_nt(q4, kwin)
            if mask_first:
                s = jnp.where(t * rows + CHUNK * j - B_WINDOW + col >= 0, s, -jnp.inf)
            sink = jnp.zeros((B_GROUP * CHUNK, 1), F32)
            for g in range(B_GROUP):
                sink = jnp.where(head_of_row == g, sink_ref[B_GROUP * h + g], sink)
            m = jnp.maximum(jnp.max(s, axis=-1, keepdims=True), sink)
            e = jnp.exp(s - m)
            p = e / (jnp.sum(e, axis=-1, keepdims=True) + jnp.exp(sink - m))
            outs.append(_dot(p.astype(MXU_DTYPE), vwin))
        for g in range(B_GROUP):
            gs = slice(CHUNK * g, CHUNK * (g + 1))
            cat_ref[rs, 512 + LANES * g:512 + LANES * (g + 1)] = jnp.where(
                lo_half, outs[0][gs], outs[1][gs]).astype(cat_ref.dtype)

    y = _dot(cat_ref[...], wout_ref[...])
    o_ref[0] = _ln(ALPHA * x_ref[0] + y, g_ref[...], b_ref[...])


def _ab_mix(sinks, x, u, va, q, k, v, hist_k, hist_v, ws, bias, wout, g, b, *, rows, cs, hist_from_self):
    bsz, t, _ = x.shape
    full = lambda a: pl.BlockSpec(a.shape, lambda bi, ti: (0,) * a.ndim)
    blk = lambda w_: pl.BlockSpec((1, rows, w_), lambda bi, ti: (bi, ti, 0))
    if hist_from_self:
        per = rows // B_WINDOW
        hist = pl.BlockSpec((1, B_WINDOW, 128), lambda bi, ti: (bi, jnp.maximum(ti * per - 1, 0), 0))
    else:
        hist = pl.BlockSpec((1, B_WINDOW, 128), lambda bi, ti: (bi, 0, 0))
    kern = functools.partial(_ab_mix_kernel, rows=rows, cs=cs, mask_first=hist_from_self)
    return pl.pallas_call(
        kern,
        grid=(bsz, t // rows),
        in_specs=[pl.BlockSpec(memory_space=pltpu.SMEM),
                  blk(1024), blk(512), blk(512), blk(512), blk(128), blk(128), hist, hist,
                  full(ws), full(bias), full(wout), full(g), full(b)],
        out_specs=blk(1024),
        out_shape=jax.ShapeDtypeStruct((bsz, t, 1024), F32),
        scratch_shapes=[pltpu.VMEM((rows, 1024), MXU_DTYPE)],
        compiler_params=pltpu.CompilerParams(
            dimension_semantics=("arbitrary", "arbitrary"), vmem_limit_bytes=VMEM_LIMIT),
        name="ab_mix",
    )(sinks, x, u, va, q, k, v, hist_k, hist_v, ws, bias, wout, g, b)


def _mlp_kernel(x_ref, w1_ref, w2_ref, g_ref, b_ref, o_ref, *, ff_tile):
    x = x_ref[...]
    xb = x.astype(MXU_DTYPE)
    acc = jnp.zeros(x.shape, F32)
    for c in range(w1_ref.shape[1] // ff_tile):
        h = _dot(xb, w1_ref[:, c * ff_tile:(c + 1) * ff_tile])
        h = jnp.square(jnp.maximum(h, 0.0)).astype(MXU_DTYPE)
        acc = acc + _dot(h, w2_ref[c * ff_tile:(c + 1) * ff_tile, :])
    o_ref[...] = _ln(ALPHA * x + acc, g_ref[...], b_ref[...])


def _mlp(x2d, w1, w2, g, b, *, tm):
    rows, d = x2d.shape
    full = lambda a: pl.BlockSpec(a.shape, lambda i: (0,) * a.ndim)
    row = pl.BlockSpec((tm, d), lambda i: (i, 0))
    return pl.pallas_call(
        functools.partial(_mlp_kernel, ff_tile=1024),
        grid=(rows // tm,),
        in_specs=[row, full(w1), full(w2), full(g), full(b)],
        out_specs=row,
        out_shape=jax.ShapeDtypeStruct((rows, d), F32),
        compiler_params=pltpu.CompilerParams(
            dimension_semantics=("arbitrary",), vmem_limit_bytes=VMEM_LIMIT),
        name="mlp",
    )(x2d, w1, w2, g, b)


def _c_in_kernel(x_ref, w_ref, cos_ref, sin_ref, q_ref, k_ref, kb_ref, v_ref, qi_ref, kk_ref, kkb_ref, wi_ref):
    y = _dot(x_ref[...].astype(MXU_DTYPE), w_ref[...])
    cos = cos_ref[...]
    sin = sin_ref[...]
    for c in range(8):
        q_ref[:, LANES * c:LANES * (c + 1)] = (
            _rope2(y[:, LANES * c:LANES * (c + 1)], cos, sin) * (HEAD_DIM ** -0.5)).astype(q_ref.dtype)
    for c in range(2):
        kr = _rope2(y[:, 1024 + LANES * c:1024 + LANES * (c + 1)], cos, sin)
        k_ref[:, LANES * c:LANES * (c + 1)] = kr
        kb_ref[:, LANES * c:LANES * (c + 1)] = kr.astype(kb_ref.dtype)
    v_ref[...] = y[:, 1280:1536]
    for c in range(2):
        qi_ref[:, LANES * c:LANES * (c + 1)] = _rope2(
            y[:, 1536 + LANES * c:1536 + LANES * (c + 1)], cos, sin).astype(qi_ref.dtype)
    kk = _rope2(y[:, 1792:1920], cos, sin)
    kk_ref[...] = kk
    kkb_ref[...] = kk.astype(kkb_ref.dtype)
    wi_ref[...] = y[:, 1920:2048]


def _c_in(x2d, w, cos, sin, *, t, tm):
    rows = x2d.shape[0]
    nt = t // tm
    row_spec = lambda w_: pl.BlockSpec((tm, w_), lambda i: (i, 0))
    full = lambda a: pl.BlockSpec(a.shape, lambda i: (0,) * a.ndim)
    tab = pl.BlockSpec((tm, LANES), lambda i: (i % nt, 0))
    widths = [(1024, MXU_DTYPE), (256, F32), (256, MXU_DTYPE), (256, F32), (256, MXU_DTYPE),
              (128, F32), (128, MXU_DTYPE), (128, F32)]
    return pl.pallas_call(
        _c_in_kernel,
        grid=(rows // tm,),
        in_specs=[row_spec(1024), full(w), tab, tab],
        out_specs=[row_spec(w_) for w_, _ in widths],
        out_shape=[jax.ShapeDtypeStruct((rows, w_), dt) for w_, dt in widths],
        compiler_params=pltpu.CompilerParams(
            dimension_semantics=("arbitrary",), vmem_limit_bytes=VMEM_LIMIT),
        name="c_in",
    )(x2d, w, cos, sin)


def _dsa_kernel(x_ref, q_ref, qi_ref, w_ref, k_ref, vt_ref, kk_ref, wout_ref, g_ref, b_ref, o_ref,
                key_ref, bias_ref, ot_ref, *, causal, n_keys, ksel, idx_bits):
    tk = KEY_TILE
    i = pl.program_id(1)
    lane = lax.broadcasted_iota(I32, (1, LANES), 1)
    lo_half = lane < HEAD_DIM
    hi_half = jnp.logical_not(lo_half)
    if causal:
        limit = CHUNK * (2 * i + jnp.where(lo_half, jnp.int32(0), jnp.int32(1)) + 1)
        n_tiles = (QUERY_BLOCK * (i + 1) + tk - 1) // tk
    else:
        limit = jnp.full((1, LANES), n_keys, I32)
        n_tiles = (n_keys + tk - 1) // tk
    sub = lax.broadcasted_iota(I32, (tk, LANES), 0)

    def tile_off(t):
        return pl.multiple_of(t * tk, tk)

    qi_blk = qi_ref[0]
    qi4 = jnp.concatenate(
        [jnp.where(lo_half if h % 2 == 0 else hi_half, qi_blk[:, LANES * (h // 2):LANES * (h // 2 + 1)], 0)
         for h in range(IDX_HEADS)], axis=0)
    w = w_ref[0]

    def score_tile(t, carry):
        off = tile_off(t)
        st = _dot_nt(kk_ref[0, pl.ds(off, tk), :], qi4)
        sc = jnp.maximum(st[:, 0:LANES], 0.0) * w[0:1]
        for h in range(1, IDX_HEADS):
            sc = sc + jnp.maximum(st[:, LANES * h:LANES * (h + 1)], 0.0) * w[h:h + 1]
        bits = pltpu.bitcast(sc, I32)
        key = jnp.where(bits < 0, bits ^ 0x7FFFFFFF, bits)
        key_ref[pl.ds(off, tk), :] = jnp.where(sub + off < limit, key, INT_MIN)
        return carry

    lax.fori_loop(0, n_tiles, score_tile, 0)

    def count(pred):
        def body(t, acc):
            off = tile_off(t)
            ones = pred(key_ref[pl.ds(off, tk), :], off)
            return acc + jnp.sum(ones.reshape(tk // 8, 8, LANES), axis=0)
        acc = lax.fori_loop(0, n_tiles, body, jnp.zeros((8, LANES), I32))
        return jnp.sum(acc, axis=0, keepdims=True)

    one = jnp.ones((tk, LANES), I32)
    zero = jnp.zeros((tk, LANES), I32)
    c0 = count(lambda blk, off: jnp.where(blk >= 0, one, zero))
    thr = jnp.where(c0 >= ksel, jnp.int32(0), jnp.int32(INT_MIN))

    def thr_step(s, thr):
        cand = thr + lax.shift_left(jnp.int32(1), 30 - s)
        c = count(lambda blk, off: jnp.where(blk >= cand, one, zero))
        return jnp.where(c >= ksel, cand, thr)

    thr = lax.fori_loop(0, 31, thr_step, thr)

    need = ksel - count(lambda blk, off: jnp.where(blk > thr, one, zero))

    def idx_step(s, jlo):
        cand = jlo + lax.shift_left(jnp.int32(1), idx_bits - 1 - s)
        f = count(lambda blk, off: jnp.where(blk == thr, jnp.where(sub + off < cand, one, zero), zero))
        return jnp.where(f < need, cand, jlo)

    jlo = lax.fori_loop(0, idx_bits, idx_step, jnp.zeros((1, LANES), I32))
    jstar = jnp.where(thr == INT_MIN, jnp.int32(0), jlo + 1)

    keep = jnp.zeros((tk, LANES), F32)
    drop = jnp.full((tk, LANES), NEG_BIG, F32)

    def bias_tile(t, carry):
        off = tile_off(t)
        blk = key_ref[pl.ds(off, tk), :]
        tie = jnp.where(blk == thr, jnp.where(sub + off < jstar, keep, drop), drop)
        bias_ref[pl.ds(off, tk), :] = jnp.where(blk > thr, keep, tie)
        return carry

    lax.fori_loop(0, n_tiles, bias_tile, 0)

    for h in range(C_KV_HEADS):
        half = lo_half if h % 2 == 0 else hi_half
        q4 = jnp.concatenate(
            [jnp.where(half, q_ref[0, :, LANES * (4 * (h // 2) + g):LANES * (4 * (h // 2) + g + 1)], 0)
             for g in range(C_GROUP)], axis=0)
        kcol = slice(LANES * (h // 2), LANES * (h // 2 + 1))

        def attn_tile(t, carry, q4=q4, kcol=kcol, h=h):
            m, l, acc = carry
            off = tile_off(t)
            st = _dot_nt(k_ref[0, pl.ds(off, tk), kcol], q4)
            bias = bias_ref[pl.ds(off, tk), :]
            st = st + jnp.concatenate([bias] * C_GROUP, axis=1)
            m_new = jnp.maximum(m, jnp.max(st, axis=0, keepdims=True))
            a = jnp.exp(m - m_new)
            p = jnp.exp(st - m_new)
            l = a * l + jnp.sum(p, axis=0, keepdims=True)
            vt = vt_ref[0, HEAD_DIM * h:HEAD_DIM * (h + 1), pl.ds(off, tk)]
            acc = a * acc + _dot(vt, p.astype(MXU_DTYPE))
            return m_new, l, acc

        init = (jnp.full((1, C_GROUP * LANES), NEG_BIG, F32), jnp.zeros((1, C_GROUP * LANES), F32),
                jnp.zeros((HEAD_DIM, C_GROUP * LANES), F32))
        _, l, acc = lax.fori_loop(0, n_tiles, attn_tile, init)
        o = acc / l
        for g in range(C_GROUP):
            slot = 8 * (h // 2) + 2 * g + (h % 2)
            ot_ref[HEAD_DIM * slot:HEAD_DIM * (slot + 1), :] = o[:, LANES * g:LANES * (g + 1)]

    attn = jnp.transpose(ot_ref[...]).astype(MXU_DTYPE)
    y = _dot(attn, wout_ref[...])
    o_ref[0] = _ln(ALPHA * x_ref[0] + y, g_ref[...], b_ref[...])


def _dsa(x, q, qi, w_t, k, v_t, kk, wout, g, b, *, causal, n_keys, ksel):
    bsz, t, _ = x.shape
    nk = k.shape[1]
    qb = QUERY_BLOCK
    idx_bits = max(1, int(math.ceil(math.log2(nk))))
    full = lambda a: pl.BlockSpec(a.shape, lambda bi, qi_: (0,) * a.ndim)
    blk = lambda w_: pl.BlockSpec((1, qb, w_), lambda bi, qi_: (bi, qi_, 0))
    per_b = lambda a: pl.BlockSpec((1,) + a.shape[1:], lambda bi, qi_: (bi, 0, 0))
    kern = functools.partial(_dsa_kernel, causal=causal, n_keys=n_keys, ksel=ksel, idx_bits=idx_bits)
    return pl.pallas_call(
        kern,
        grid=(bsz, t // qb),
        in_specs=[blk(1024), blk(1024), blk(256),
                  pl.BlockSpec((1, 8, qb), lambda bi, qi_: (bi, 0, qi_)),
                  per_b(k), per_b(v_t), per_b(kk), full(wout), full(g), full(b)],
        out_specs=blk(1024),
        out_shape=jax.ShapeDtypeStruct((bsz, t, 1024), F32),
        scratch_shapes=[pltpu.VMEM((nk, LANES), I32), pltpu.VMEM((nk, LANES), F32),
                        pltpu.VMEM((1024, LANES), F32)],
        compiler_params=pltpu.CompilerParams(
            dimension_semantics=("arbitrary", "arbitrary"), vmem_limit_bytes=VMEM_LIMIT),
        name="dsa",
    )(x, q, qi, w_t, k, v_t, kk, wout, g, b)


def _slot_perm(head_of_slot):
    return np.concatenate([np.arange(HEAD_DIM * j, HEAD_DIM * (j + 1)) for j in head_of_slot])


_AB_Q_PERM = _slot_perm([B_GROUP * (s % 2) + s // 2 for s in range(8)])
_C_Q_PERM = _slot_perm([4 * (2 * (s // 8) + (s % 8) % 2) + (s % 8) // 2 for s in range(16)])


def _rope_tables(pos):
    half = HEAD_DIM // 2
    inv_freq = jnp.exp(-math.log(ROPE_THETA) * jnp.arange(half, dtype=F32) / half)
    ang = pos.astype(F32)[:, None] * inv_freq[None, :]
    cos, sin = jnp.cos(ang), jnp.sin(ang)
    return jnp.concatenate([cos] * 4, axis=1), jnp.concatenate([-sin, sin, -sin, sin], axis=1)


def _round_up(n, m):
    return (n + m - 1) // m * m


def _trunk(x, pos, caches, wts):
    (ab_w_in, a_ln_g, a_ln_b, a_ws, a_bs, b_sinks, ab_w_out, c_w_in, c_w_out,
     ln1_g, ln1_b, ln2_g, ln2_b, ff_w1, ff_w2) = wts
    bsz, t, d = x.shape
    sample = caches is not None
    cos, sin = _rope_tables(pos)
    tm = min(512, t)
    row = lambda a: a.reshape(1, -1)

    w_in = jnp.concatenate([ab_w_in[0][:, :1024], ab_w_in[0][:, 1024:1536][:, _AB_Q_PERM],
                            ab_w_in[0][:, 1536:]], axis=1).astype(MXU_DTYPE)
    w_out = jnp.concatenate([ab_w_out[0][:512], ab_w_out[0][512:][_AB_Q_PERM]], axis=0).astype(MXU_DTYPE)
    u, va, q, k, v = _ab_in(x.reshape(bsz * t, d), w_in, cos, sin, row(a_ln_g[0]), row(a_ln_b[0]), t=t, tm=tm)
    r3 = lambda a: a.reshape(bsz, t, a.shape[-1])
    u, va, q, k, v = r3(u), r3(va), r3(q), r3(k), r3(v)
    cs = min(A_CHUNK, t)
    ws = a_ws[0][:, :cs, :cs]
    bias = jnp.repeat(a_bs[0][:, :cs].T, HEAD_DIM, axis=1)
    if sample:
        hist_k = caches[0][0].reshape(bsz, B_WINDOW, 128)
        hist_v = caches[1][0].reshape(bsz, B_WINDOW, 128)
        rows = t
    else:
        hist_k, hist_v = k, v
        rows = min(256, t)
    x = _ab_mix(b_sinks[0], x, u, va, q, k, v, hist_k, hist_v, ws, bias, w_out,
                row(ln1_g[0]), row(ln1_b[0]), rows=rows, cs=cs, hist_from_self=not sample)
    x = _mlp(x.reshape(bsz * t, d), ff_w1[0].astype(MXU_DTYPE), ff_w2[0].astype(MXU_DTYPE),
             row(ln2_g[0]), row(ln2_b[0]), tm=tm).reshape(bsz, t, d)
    b_k = k.reshape(bsz, t, B_KV_HEADS, HEAD_DIM)
    b_v = v.reshape(bsz, t, B_KV_HEADS, HEAD_DIM)

    cw = c_w_in[0]
    w_in = jnp.concatenate(
        [cw[:, :1024][:, _C_Q_PERM], cw[:, 1024:1792], cw[:, 1792:1856], cw[:, 1792:1856], cw[:, 1856:1860],
         jnp.zeros((d, LANES - IDX_HEADS), F32)], axis=1).astype(MXU_DTYPE)
    w_out = c_w_out[0][_C_Q_PERM].astype(MXU_DTYPE)
    q, k, kb, v, qi, kk, kkb, wi = _c_in(x.reshape(bsz * t, d), w_in, cos, sin, t=t, tm=tm)
    q, k, kb, v, qi, kk, kkb, wi = (r3(a) for a in (q, k, kb, v, qi, kk, kkb, wi))
    ki = kk[:, :, :IDX_DIM]
    w_t = jnp.swapaxes(wi[:, :, :8], 1, 2) * (IDX_DIM ** -0.5 * IDX_HEADS ** -0.5)
    if sample:
        keys_k = jnp.concatenate([caches[2][0].reshape(bsz, -1, 256).astype(MXU_DTYPE), kb], axis=1)
        keys_v = jnp.concatenate([caches[3][0].reshape(bsz, -1, 256), v], axis=1).astype(MXU_DTYPE)
        ci = caches[4][0].astype(MXU_DTYPE)
        keys_i = jnp.concatenate([jnp.concatenate([ci, ci], axis=-1), kkb], axis=1)
        n_keys = keys_k.shape[1]
        pad_k = _round_up(n_keys, KEY_TILE) - n_keys
        keys_k, keys_v, keys_i = (jnp.pad(a, ((0, 0), (0, pad_k), (0, 0))) for a in (keys_k, keys_v, keys_i))
        pad_q = QUERY_BLOCK - t
        padq = lambda a: jnp.pad(a, ((0, 0), (0, pad_q), (0, 0)))
        xo = _dsa(padq(x), padq(q), padq(qi), jnp.pad(w_t, ((0, 0), (0, 0), (0, pad_q))), keys_k,
                  jnp.swapaxes(keys_v, 1, 2), keys_i, w_out, row(ln1_g[1]), row(ln1_b[1]),
                  causal=False, n_keys=n_keys, ksel=min(TOPK_MAX, n_keys // 4))[:, :t]
    else:
        xo = _dsa(x, q, qi, w_t, kb, jnp.swapaxes(v.astype(MXU_DTYPE), 1, 2), kkb, w_out,
                  row(ln1_g[1]), row(ln1_b[1]), causal=True, n_keys=t, ksel=min(TOPK_MAX, t // 4))
    x = _mlp(xo.reshape(bsz * t, d), ff_w1[1].astype(MXU_DTYPE), ff_w2[1].astype(MXU_DTYPE),
             row(ln2_g[1]), row(ln2_b[1]), tm=tm).reshape(bsz, t, d)
    c_k = k.reshape(bsz, t, C_KV_HEADS, HEAD_DIM)
    c_v = v.reshape(bsz, t, C_KV_HEADS, HEAD_DIM)
    return x, va[None], b_k[None], b_v[None], c_k[None], c_v[None], ki[None]


def kernel(x_prompt, x_sample, cache_b_k, cache_b_v, cache_c_k, cache_c_v, cache_c_idx, ab_w_in, a_ln_g, a_ln_b, a_ws, a_bs, b_sinks, ab_w_out, c_w_in, c_w_out, ln1_g, ln1_b, ln2_g, ln2_b, ff_w1, ff_w2):
    wts = (ab_w_in, a_ln_g, a_ln_b, a_ws, a_bs, b_sinks, ab_w_out, c_w_in, c_w_out,
           ln1_g, ln1_b, ln2_g, ln2_b, ff_w1, ff_w2)
    past_len = cache_c_k.shape[2]
    pos_p = jnp.arange(x_prompt.shape[1], dtype=jnp.int32)
    pos_s = past_len + jnp.arange(x_sample.shape[1], dtype=jnp.int32)
    y_p, _, p_b_k, p_b_v, p_c_k, p_c_v, p_c_idx = _trunk(x_prompt, pos_p, None, wts)
    y_s, s_a_v, s_b_k, s_b_v, s_c_k, s_c_v, s_c_idx = _trunk(
        x_sample, pos_s, (cache_b_k, cache_b_v, cache_c_k, cache_c_v, cache_c_idx), wts)
    return (y_p, y_s, p_b_k[:, :, -B_WINDOW:], p_b_v[:, :, -B_WINDOW:], p_c_k, p_c_v, p_c_idx,
            s_a_v, s_b_k, s_b_v, s_c_k, s_c_v, s_c_idx)
```

```python
import functools
import math

import numpy as np
import jax
import jax.numpy as jnp
from jax import lax
from jax.experimental import pallas as pl
from jax.experimental.pallas import tpu as pltpu

F32 = jnp.float32
I32 = jnp.int32
MXU_DTYPE = jnp.bfloat16

HEAD_DIM = 64
CHUNK = 64
ROPE_THETA = 10000.0
LN_EPS = 1e-5
DEPTH = 2
ALPHA = (2.0 * DEPTH) ** 0.25
A_GROUPS = 8
A_CHUNK = 128
B_KV_HEADS = 2
B_GROUP = 4
B_WINDOW = 128
C_KV_HEADS = 4
C_GROUP = 4
IDX_HEADS = 4
IDX_DIM = 64
TOPK_MAX = 256

LANES = 128
KEY_TILE = 512
SUB_TILE = 256
QUERY_BLOCK = 128
INT_MIN = -(2 ** 31)
NEG_BIG = -(2.0 ** 100)
LOG2E = math.log2(math.e)
V_ROWS = 80
VMEM_LIMIT = 56 * 1024 * 1024


def _gelu(x):
    c = math.sqrt(2.0 / math.pi)
    return 0.5 * x * (1.0 + jnp.tanh(c * (x + 0.044715 * (x * x * x))))


def _ln(z, g, b):
    mu = jnp.mean(z, axis=-1, keepdims=True)
    d = z - mu
    var = jnp.mean(d * d, axis=-1, keepdims=True)
    return d * lax.rsqrt(var + LN_EPS) * g + b


def _rope2(x, cos, sin):
    lane = lax.broadcasted_iota(I32, (1, LANES), 1)
    first = (lane % HEAD_DIM) < (HEAD_DIM // 2)
    swapped = jnp.where(first, pltpu.roll(x, LANES - HEAD_DIM // 2, 1), pltpu.roll(x, HEAD_DIM // 2, 1))
    return x * cos + swapped * sin


def _dot(a, b):
    return jnp.dot(a, b, preferred_element_type=F32)


def _dot_nt(a, b):
    return lax.dot_general(a, b, (((1,), (1,)), ((), ())), preferred_element_type=F32)


def _ab_in_kernel(x_ref, w_ref, cos_ref, sin_ref, g_ref, b_ref, u_ref, va_ref, q_ref, k_ref, v_ref):
    y = _dot(x_ref[...].astype(MXU_DTYPE), w_ref[...])
    u_ref[...] = _gelu(y[:, 0:512])
    va_ref[...] = _ln(_gelu(y[:, 512:1024]), g_ref[...], b_ref[...])
    cos = cos_ref[...]
    sin = sin_ref[...]
    for c in range(4):
        lo = 1024 + LANES * c
        q_ref[:, LANES * c:LANES * (c + 1)] = (
            _rope2(y[:, lo:lo + LANES], cos, sin) * (HEAD_DIM ** -0.5)).astype(q_ref.dtype)
    k_ref[...] = _rope2(y[:, 1536:1664], cos, sin)
    v_ref[...] = y[:, 1664:1792]


def _ab_in(x2d, w, cos, sin, g, b, *, t, tm):
    rows = x2d.shape[0]
    nt = t // tm
    row_spec = lambda w_: pl.BlockSpec((tm, w_), lambda i: (i, 0))
    full = lambda a: pl.BlockSpec(a.shape, lambda i: (0,) * a.ndim)
    tab = pl.BlockSpec((tm, LANES), lambda i: (i % nt, 0))
    return pl.pallas_call(
        _ab_in_kernel,
        grid=(rows // tm,),
        in_specs=[row_spec(1024), full(w), tab, tab, full(g), full(b)],
        out_specs=[row_spec(512), row_spec(512), row_spec(512), row_spec(128), row_spec(128)],
        out_shape=[
            jax.ShapeDtypeStruct((rows, 512), F32),
            jax.ShapeDtypeStruct((rows, 512), F32),
            jax.ShapeDtypeStruct((rows, 512), MXU_DTYPE),
            jax.ShapeDtypeStruct((rows, 128), F32),
            jax.ShapeDtypeStruct((rows, 128), F32),
        ],
        compiler_params=pltpu.CompilerParams(
            dimension_semantics=("arbitrary",), vmem_limit_bytes=VMEM_LIMIT),
        name="ab_in",
    )(x2d, w, cos, sin, g, b)


def _ab_mix_kernel(sink_ref, x_ref, u_ref, va_ref, q_ref, k_ref, v_ref, hk_ref, hv_ref,
                   ws_ref, bias_ref, wout_ref, g_ref, b_ref, o_ref, cat_ref, *, rows, cs, mask_first):
    t = pl.program_id(1)
    lane = lax.broadcasted_iota(I32, (1, LANES), 1)
    lo_half = lane < HEAD_DIM

    r_i = lax.broadcasted_iota(I32, (cs, cs), 0)
    c_i = lax.broadcasted_iota(I32, (cs, cs), 1)
    tril = r_i >= c_i
    w_tril = [jnp.where(tril, ws_ref[g], 0.0).astype(MXU_DTYPE) for g in range(A_GROUPS)]
    for c in range(rows // cs):
        rs = slice(c * cs, (c + 1) * cs)
        for p in range(A_GROUPS // 2):
            ls = slice(LANES * p, LANES * (p + 1))
            vp = va_ref[0, rs, ls].astype(MXU_DTYPE)
            gate = jnp.where(lo_half, _dot(w_tril[2 * p], vp), _dot(w_tril[2 * p + 1], vp)) + bias_ref[:, ls]
            cat_ref[rs, ls] = (u_ref[0, rs, ls] * gate).astype(cat_ref.dtype)

    kcat = jnp.concatenate([hk_ref[0], k_ref[0]], axis=0).astype(MXU_DTYPE)
    vcat = jnp.concatenate([hv_ref[0], v_ref[0]], axis=0).astype(MXU_DTYPE)
    nwin = B_WINDOW + CHUNK
    head_of_row = lax.broadcasted_iota(I32, (B_GROUP * CHUNK, 1), 0) // CHUNK
    col = lax.broadcasted_iota(I32, (1, nwin), 1)
    for j in range(rows // CHUNK):
        rs = slice(CHUNK * j, CHUNK * (j + 1))
        kwin = kcat[CHUNK * j:CHUNK * j + nwin]
        vwin = vcat[CHUNK * j:CHUNK * j + nwin]
        outs = []
        for h in range(B_KV_HEADS):
            half = lo_half if h == 0 else jnp.logical_not(lo_half)
            q4 = jnp.concatenate(
                [jnp.where(half, q_ref[0, rs, LANES * g:LANES * (g + 1)], 0).astype(MXU_DTYPE)
                 for g in range(B_GROUP)], axis=0)
            s = _dot_nt(q4, kwin)
            if mask_first:
                s = jnp.where(t * rows + CHUNK * j - B_WINDOW + col >= 0, s, -jnp.inf)
            sink = jnp.zeros((B_GROUP * CHUNK, 1), F32)
            for g in range(B_GROUP):
                sink = jnp.where(head_of_row == g, sink_ref[B_GROUP * h + g], sink)
            m = jnp.maximum(jnp.max(s, axis=-1, keepdims=True), sink)
            e = jnp.exp(s - m)
            p = e / (jnp.sum(e, axis=-1, keepdims=True) + jnp.exp(sink - m))
            outs.append(_dot(p.astype(MXU_DTYPE), vwin))
        for g in range(B_GROUP):
            gs = slice(CHUNK * g, CHUNK * (g + 1))
            cat_ref[rs, 512 + LANES * g:512 + LANES * (g + 1)] = jnp.where(
                lo_half, outs[0][gs], outs[1][gs]).astype(cat_ref.dtype)

    y = _dot(cat_ref[...], wout_ref[...])
    o_ref[0] = _ln(ALPHA * x_ref[0] + y, g_ref[...], b_ref[...])


def _ab_mix(sinks, x, u, va, q, k, v, hist_k, hist_v, ws, bias, wout, g, b, *, rows, cs, hist_from_self):
    bsz, t, _ = x.shape
    full = lambda a: pl.BlockSpec(a.shape, lambda bi, ti: (0,) * a.ndim)
    blk = lambda w_: pl.BlockSpec((1, rows, w_), lambda bi, ti: (bi, ti, 0))
    if hist_from_self:
        per = rows // B_WINDOW
        hist = pl.BlockSpec((1, B_WINDOW, 128), lambda bi, ti: (bi, jnp.maximum(ti * per - 1, 0), 0))
    else:
        hist = pl.BlockSpec((1, B_WINDOW, 128), lambda bi, ti: (bi, 0, 0))
    kern = functools.partial(_ab_mix_kernel, rows=rows, cs=cs, mask_first=hist_from_self)
    return pl.pallas_call(
        kern,
        grid=(bsz, t // rows),
        in_specs=[pl.BlockSpec(memory_space=pltpu.SMEM),
                  blk(1024), blk(512), blk(512), blk(512), blk(128), blk(128), hist, hist,
                  full(ws), full(bias), full(wout), full(g), full(b)],
        out_specs=blk(1024),
        out_shape=jax.ShapeDtypeStruct((bsz, t, 1024), F32),
        scratch_shapes=[pltpu.VMEM((rows, 1024), MXU_DTYPE)],
        compiler_params=pltpu.CompilerParams(
            dimension_semantics=("arbitrary", "arbitrary"), vmem_limit_bytes=VMEM_LIMIT),
        name="ab_mix",
    )(sinks, x, u, va, q, k, v, hist_k, hist_v, ws, bias, wout, g, b)


def _mlp_kernel(x_ref, w1_ref, w2_ref, g_ref, b_ref, o_ref, *, ff_tile):
    x = x_ref[...]
    xb = x.astype(MXU_DTYPE)
    acc = jnp.zeros(x.shape, F32)
    for c in range(w1_ref.shape[1] // ff_tile):
        h = _dot(xb, w1_ref[:, c * ff_tile:(c + 1) * ff_tile])
        h = jnp.square(jnp.maximum(h, 0.0)).astype(MXU_DTYPE)
        acc = acc + _dot(h, w2_ref[c * ff_tile:(c + 1) * ff_tile, :])
    o_ref[...] = _ln(ALPHA * x + acc, g_ref[...], b_ref[...])


def _mlp(x2d, w1, w2, g, b, *, tm):
    rows, d = x2d.shape
    full = lambda a: pl.BlockSpec(a.shape, lambda i: (0,) * a.ndim)
    row = pl.BlockSpec((tm, d), lambda i: (i, 0))
    return pl.pallas_call(
        functools.partial(_mlp_kernel, ff_tile=1024),
        grid=(rows // tm,),
        in_specs=[row, full(w1), full(w2), full(g), full(b)],
        out_specs=row,
        out_shape=jax.ShapeDtypeStruct((rows, d), F32),
        compiler_params=pltpu.CompilerParams(
            dimension_semantics=("arbitrary",), vmem_limit_bytes=VMEM_LIMIT),
        name="mlp",
    )(x2d, w1, w2, g, b)


def _c_in_kernel(x_ref, w_ref, cos_ref, sin_ref, q_ref, k_ref, kb_ref, v_ref, qi_ref, kk_ref, kkb_ref, wi_ref):
    y = _dot(x_ref[...].astype(MXU_DTYPE), w_ref[...])
    cos = cos_ref[...]
    sin = sin_ref[...]
    for c in range(8):
        q_ref[:, LANES * c:LANES * (c + 1)] = (
            _rope2(y[:, LANES * c:LANES * (c + 1)], cos, sin) * (LOG2E * HEAD_DIM ** -0.5)).astype(q_ref.dtype)
    for c in range(2):
        kr = _rope2(y[:, 1024 + LANES * c:1024 + LANES * (c + 1)], cos, sin)
        k_ref[:, LANES * c:LANES * (c + 1)] = kr
        kb_ref[:, LANES * c:LANES * (c + 1)] = kr.astype(kb_ref.dtype)
    v_ref[...] = y[:, 1280:1536]
    for c in range(2):
        qi_ref[:, LANES * c:LANES * (c + 1)] = _rope2(
            y[:, 1536 + LANES * c:1536 + LANES * (c + 1)], cos, sin).astype(qi_ref.dtype)
    kk = _rope2(y[:, 1792:1920], cos, sin)
    kk_ref[...] = kk
    kkb_ref[...] = kk.astype(kkb_ref.dtype)
    wi_ref[...] = y[:, 1920:2048]


def _c_in(x2d, w, cos, sin, *, t, tm):
    rows = x2d.shape[0]
    nt = t // tm
    row_spec = lambda w_: pl.BlockSpec((tm, w_), lambda i: (i, 0))
    full = lambda a: pl.BlockSpec(a.shape, lambda i: (0,) * a.ndim)
    tab = pl.BlockSpec((tm, LANES), lambda i: (i % nt, 0))
    widths = [(1024, MXU_DTYPE), (256, F32), (256, MXU_DTYPE), (256, F32), (256, MXU_DTYPE),
              (128, F32), (128, MXU_DTYPE), (128, F32)]
    return pl.pallas_call(
        _c_in_kernel,
        grid=(rows // tm,),
        in_specs=[row_spec(1024), full(w), tab, tab],
        out_specs=[row_spec(w_) for w_, _ in widths],
        out_shape=[jax.ShapeDtypeStruct((rows, w_), dt) for w_, dt in widths],
        compiler_params=pltpu.CompilerParams(
            dimension_semantics=("arbitrary",), vmem_limit_bytes=VMEM_LIMIT),
        name="c_in",
    )(x2d, w, cos, sin)


def _dsa_kernel(x_ref, q_ref, qi_ref, w_ref, k_ref, vt_ref, kk_ref, wout_ref, g_ref, b_ref, o_ref,
                key_ref, bias_ref, rhs_ref, m_ref, acc_ref, ot_ref, *, causal, n_keys, ksel):
    tk, st_ = KEY_TILE, SUB_TILE
    i = pl.program_id(1)
    lane = lax.broadcasted_iota(I32, (1, LANES), 1)
    lo_half = lane < HEAD_DIM
    hi_half = jnp.logical_not(lo_half)
    if causal:
        limit = CHUNK * (2 * i + jnp.where(lo_half, jnp.int32(0), jnp.int32(1)) + 1)
        n_tiles = (QUERY_BLOCK * (i + 1) + tk - 1) // tk
    else:
        limit = jnp.full((1, LANES), n_keys, I32)
        n_tiles = (n_keys + tk - 1) // tk
    sub = lax.broadcasted_iota(I32, (st_, LANES), 0)

    def sub_off(t, s):
        return pl.multiple_of(t * tk + s * st_, st_)

    qi_blk = qi_ref[0]
    qi4 = jnp.concatenate(
        [jnp.where(lo_half if h % 2 == 0 else hi_half, qi_blk[:, LANES * (h // 2):LANES * (h // 2 + 1)], 0)
         for h in range(IDX_HEADS)], axis=0)
    w = w_ref[0]

    def score_tile(t, carry):
        for s in range(tk // st_):
            off = sub_off(t, s)
            sc4 = _dot_nt(kk_ref[0, pl.ds(off, st_), :], qi4)
            sc = jnp.maximum(sc4[:, 0:LANES], 0.0) * w[0:1]
            for h in range(1, IDX_HEADS):
                sc = sc + jnp.maximum(sc4[:, LANES * h:LANES * (h + 1)], 0.0) * w[h:h + 1]
            bits = pltpu.bitcast(sc, I32)
            key = jnp.where(bits < 0, bits ^ 0x7FFFFFFF, bits)
            key_ref[pl.ds(off, st_), :] = jnp.where(sub + off < limit, key, INT_MIN)
        return carry

    lax.fori_loop(0, n_tiles, score_tile, 0)

    one = jnp.ones((tk, LANES), I32)
    zero = jnp.zeros((tk, LANES), I32)

    def count(pred):
        def body(t, acc):
            ones = pred(key_ref[pl.ds(pl.multiple_of(t * tk, tk), tk), :])
            return acc + jnp.sum(ones.reshape(tk // 8, 8, LANES), axis=0)
        acc = lax.fori_loop(0, n_tiles, body, jnp.zeros((8, LANES), I32))
        return jnp.sum(acc, axis=0, keepdims=True)

    c0 = count(lambda blk: jnp.where(blk >= 0, one, zero))
    thr = jnp.where(c0 >= ksel, jnp.int32(0), jnp.int32(INT_MIN))

    def thr_step(s, thr):
        cand = thr + lax.shift_left(jnp.int32(1), 30 - s)
        c = count(lambda blk: jnp.where(blk >= cand, one, zero))
        return jnp.where(c >= ksel, cand, thr)

    thr = lax.fori_loop(0, 31, thr_step, thr)

    need = ksel - count(lambda blk: jnp.where(blk > thr, one, zero))
    need = jnp.where(thr == INT_MIN, jnp.int32(0), need).astype(F32)
    tril = (lax.broadcasted_iota(I32, (st_, st_), 0) >= lax.broadcasted_iota(I32, (st_, st_), 1)
            ).astype(F32).astype(MXU_DTYPE)
    keep = jnp.zeros((st_, LANES), F32)
    drop = jnp.full((st_, LANES), NEG_BIG, F32)
    f_one = jnp.ones((st_, LANES), F32)

    def bias_tile(t, seen):
        for s in range(tk // st_):
            off = sub_off(t, s)
            blk = key_ref[pl.ds(off, st_), :]
            tied = blk == thr
            rank = _dot(tril, jnp.where(tied, f_one, keep).astype(MXU_DTYPE)) + seen
            tie_bias = jnp.where(tied, jnp.where(rank <= need, keep, drop), drop)
            bias_ref[pl.ds(off, st_), :] = jnp.where(blk > thr, keep, tie_bias).astype(bias_ref.dtype)
            seen = rank[st_ - 1:st_, :]
        return seen

    lax.fori_loop(0, n_tiles, bias_tile, jnp.zeros((1, LANES), F32))

    eye4 = (lax.broadcasted_iota(I32, (C_GROUP * LANES, LANES), 0) % LANES
            == lax.broadcasted_iota(I32, (C_GROUP * LANES, LANES), 1)).astype(F32).astype(MXU_DTYPE)
    for h in range(C_KV_HEADS):
        half = lo_half if h % 2 == 0 else hi_half
        q4 = jnp.concatenate(
            [jnp.where(half, q_ref[0, :, LANES * (4 * (h // 2) + g):LANES * (4 * (h // 2) + g + 1)], 0)
             for g in range(C_GROUP)], axis=0)
        rhs_ref[h] = jnp.concatenate([q4, eye4], axis=1)
    m_ref[...] = jnp.full(m_ref.shape, NEG_BIG, F32)
    acc_ref[...] = jnp.zeros(acc_ref.shape, F32)

    def attn_tile(t, carry):
        offs = [sub_off(t, s) for s in range(tk // st_)]
        scs = []
        for off in offs:
            bias = bias_ref[pl.ds(off, st_), :]
            for h in range(C_KV_HEADS):
                kcol = slice(LANES * (h // 2), LANES * (h // 2 + 1))
                lhs = jnp.concatenate([k_ref[0, pl.ds(off, st_), kcol], bias], axis=1)
                scs.append(_dot_nt(lhs, rhs_ref[h]))
        for s, off in enumerate(offs):
            for h in range(C_KV_HEADS):
                sc = scs[C_KV_HEADS * s + h]
                m_old = m_ref[h]
                m_new = jnp.maximum(m_old, jnp.max(sc, axis=0, keepdims=True))
                p = jnp.exp2(sc - m_new).astype(MXU_DTYPE)
                vt = vt_ref[0, V_ROWS * h:V_ROWS * (h + 1), pl.ds(off, st_)]
                acc_ref[h] = jnp.exp2(m_old - m_new) * acc_ref[h] + _dot(vt, p)
                m_ref[h] = m_new
        return carry

    lax.fori_loop(0, n_tiles, attn_tile, 0)

    for h in range(C_KV_HEADS):
        acc = acc_ref[h]
        o = acc[0:HEAD_DIM] / acc[HEAD_DIM:HEAD_DIM + 1]
        for g in range(C_GROUP):
            slot = 8 * (h // 2) + 2 * g + (h % 2)
            ot_ref[HEAD_DIM * slot:HEAD_DIM * (slot + 1), :] = o[:, LANES * g:LANES * (g + 1)]

    attn = jnp.transpose(ot_ref[...]).astype(MXU_DTYPE)
    y = _dot(attn, wout_ref[...])
    o_ref[0] = _ln(ALPHA * x_ref[0] + y, g_ref[...], b_ref[...])


def _dsa(x, q, qi, w_t, k, v_t, kk, wout, g, b, *, causal, n_keys, ksel):
    bsz, t, _ = x.shape
    nk = k.shape[1]
    qb = QUERY_BLOCK
    full = lambda a: pl.BlockSpec(a.shape, lambda bi, qi_: (0,) * a.ndim)
    blk = lambda w_: pl.BlockSpec((1, qb, w_), lambda bi, qi_: (bi, qi_, 0))
    per_b = lambda a: pl.BlockSpec((1,) + a.shape[1:], lambda bi, qi_: (bi, 0, 0))
    kern = functools.partial(_dsa_kernel, causal=causal, n_keys=n_keys, ksel=ksel)
    return pl.pallas_call(
        kern,
        grid=(bsz, t // qb),
        in_specs=[blk(1024), blk(1024), blk(256),
                  pl.BlockSpec((1, 8, qb), lambda bi, qi_: (bi, 0, qi_)),
                  per_b(k), per_b(v_t), per_b(kk), full(wout), full(g), full(b)],
        out_specs=blk(1024),
        out_shape=jax.ShapeDtypeStruct((bsz, t, 1024), F32),
        scratch_shapes=[pltpu.VMEM((nk, LANES), I32),
                        pltpu.VMEM((nk, LANES), MXU_DTYPE),
                        pltpu.VMEM((C_KV_HEADS, C_GROUP * LANES, 2 * LANES), MXU_DTYPE),
                        pltpu.VMEM((C_KV_HEADS, 1, C_GROUP * LANES), F32),
                        pltpu.VMEM((C_KV_HEADS, V_ROWS, C_GROUP * LANES), F32),
                        pltpu.VMEM((1024, LANES), F32)],
        compiler_params=pltpu.CompilerParams(
            dimension_semantics=("arbitrary", "arbitrary"), vmem_limit_bytes=VMEM_LIMIT),
        name="dsa",
    )(x, q, qi, w_t, k, v_t, kk, wout, g, b)


def _slot_perm(head_of_slot):
    return np.concatenate([np.arange(HEAD_DIM * j, HEAD_DIM * (j + 1)) for j in head_of_slot])


_AB_Q_PERM = _slot_perm([B_GROUP * (s % 2) + s // 2 for s in range(8)])
_C_Q_PERM = _slot_perm([4 * (2 * (s // 8) + (s % 8) % 2) + (s % 8) // 2 for s in range(16)])


def _rope_tables(pos):
    half = HEAD_DIM // 2
    inv_freq = jnp.exp(-math.log(ROPE_THETA) * jnp.arange(half, dtype=F32) / half)
    ang = pos.astype(F32)[:, None] * inv_freq[None, :]
    cos, sin = jnp.cos(ang), jnp.sin(ang)
    return jnp.concatenate([cos] * 4, axis=1), jnp.concatenate([-sin, sin, -sin, sin], axis=1)


def _value_operand(v):
    bsz, n, _ = v.shape
    vt = jnp.swapaxes(v, 1, 2).reshape(bsz, C_KV_HEADS, HEAD_DIM, n)
    ones = jnp.ones((bsz, C_KV_HEADS, 1, n), v.dtype)
    zeros = jnp.zeros((bsz, C_KV_HEADS, V_ROWS - HEAD_DIM - 1, n), v.dtype)
    return jnp.concatenate([vt, ones, zeros], axis=2).reshape(bsz, C_KV_HEADS * V_ROWS, n)


def _round_up(n, m):
    return (n + m - 1) // m * m


def _trunk(x, pos, caches, wts):
    (ab_w_in, a_ln_g, a_ln_b, a_ws, a_bs, b_sinks, ab_w_out, c_w_in, c_w_out,
     ln1_g, ln1_b, ln2_g, ln2_b, ff_w1, ff_w2) = wts
    bsz, t, d = x.shape
    sample = caches is not None
    cos, sin = _rope_tables(pos)
    tm = min(512, t)
    row = lambda a: a.reshape(1, -1)

    w_in = jnp.concatenate([ab_w_in[0][:, :1024], ab_w_in[0][:, 1024:1536][:, _AB_Q_PERM],
                            ab_w_in[0][:, 1536:]], axis=1).astype(MXU_DTYPE)
    w_out = jnp.concatenate([ab_w_out[0][:512], ab_w_out[0][512:][_AB_Q_PERM]], axis=0).astype(MXU_DTYPE)
    u, va, q, k, v = _ab_in(x.reshape(bsz * t, d), w_in, cos, sin, row(a_ln_g[0]), row(a_ln_b[0]), t=t, tm=tm)
    r3 = lambda a: a.reshape(bsz, t, a.shape[-1])
    u, va, q, k, v = r3(u), r3(va), r3(q), r3(k), r3(v)
    cs = min(A_CHUNK, t)
    ws = a_ws[0][:, :cs, :cs]
    bias = jnp.repeat(a_bs[0][:, :cs].T, HEAD_DIM, axis=1)
    if sample:
        hist_k = caches[0][0].reshape(bsz, B_WINDOW, 128)
        hist_v = caches[1][0].reshape(bsz, B_WINDOW, 128)
        rows = t
    else:
        hist_k, hist_v = k, v
        rows = min(256, t)
    x = _ab_mix(b_sinks[0], x, u, va, q, k, v, hist_k, hist_v, ws, bias, w_out,
                row(ln1_g[0]), row(ln1_b[0]), rows=rows, cs=cs, hist_from_self=not sample)
    x = _mlp(x.reshape(bsz * t, d), ff_w1[0].astype(MXU_DTYPE), ff_w2[0].astype(MXU_DTYPE),
             row(ln2_g[0]), row(ln2_b[0]), tm=tm).reshape(bsz, t, d)
    b_k = k.reshape(bsz, t, B_KV_HEADS, HEAD_DIM)
    b_v = v.reshape(bsz, t, B_KV_HEADS, HEAD_DIM)

    cw = c_w_in[0]
    w_in = jnp.concatenate(
        [cw[:, :1024][:, _C_Q_PERM], cw[:, 1024:1792], cw[:, 1792:1856], cw[:, 1792:1856], cw[:, 1856:1860],
         jnp.zeros((d, LANES - IDX_HEADS), F32)], axis=1).astype(MXU_DTYPE)
    w_out = c_w_out[0][_C_Q_PERM].astype(MXU_DTYPE)
    q, k, kb, v, qi, kk, kkb, wi = _c_in(x.reshape(bsz * t, d), w_in, cos, sin, t=t, tm=tm)
    q, k, kb, v, qi, kk, kkb, wi = (r3(a) for a in (q, k, kb, v, qi, kk, kkb, wi))
    ki = kk[:, :, :IDX_DIM]
    w_t = jnp.swapaxes(wi[:, :, :8], 1, 2) * (IDX_DIM ** -0.5 * IDX_HEADS ** -0.5)
    if sample:
        keys_k = jnp.concatenate([caches[2][0].reshape(bsz, -1, 256).astype(MXU_DTYPE), kb], axis=1)
        keys_v = jnp.concatenate([caches[3][0].reshape(bsz, -1, 256), v], axis=1).astype(MXU_DTYPE)
        ci = caches[4][0].astype(MXU_DTYPE)
        keys_i = jnp.concatenate([jnp.concatenate([ci, ci], axis=-1), kkb], axis=1)
        n_keys = keys_k.shape[1]
        pad_k = _round_up(n_keys, KEY_TILE) - n_keys
        keys_k, keys_v, keys_i = (jnp.pad(a, ((0, 0), (0, pad_k), (0, 0))) for a in (keys_k, keys_v, keys_i))
        pad_q = QUERY_BLOCK - t
        padq = lambda a: jnp.pad(a, ((0, 0), (0, pad_q), (0, 0)))
        xo = _dsa(padq(x), padq(q), padq(qi), jnp.pad(w_t, ((0, 0), (0, 0), (0, pad_q))), keys_k,
                  _value_operand(keys_v), keys_i, w_out, row(ln1_g[1]), row(ln1_b[1]),
                  causal=False, n_keys=n_keys, ksel=min(TOPK_MAX, n_keys // 4))[:, :t]
    else:
        xo = _dsa(x, q, qi, w_t, kb, _value_operand(v.astype(MXU_DTYPE)), kkb, w_out,
                  row(ln1_g[1]), row(ln1_b[1]), causal=True, n_keys=t, ksel=min(TOPK_MAX, t // 4))
    x = _mlp(xo.reshape(bsz * t, d), ff_w1[1].astype(MXU_DTYPE), ff_w2[1].astype(MXU_DTYPE),
             row(ln2_g[1]), row(ln2_b[1]), tm=tm).reshape(bsz, t, d)
    c_k = k.reshape(bsz, t, C_KV_HEADS, HEAD_DIM)
    c_v = v.reshape(bsz, t, C_KV_HEADS, HEAD_DIM)
    return x, va[None], b_k[None], b_v[None], c_k[None], c_v[None], ki[None]


def kernel(x_prompt, x_sample, cache_b_k, cache_b_v, cache_c_k, cache_c_v, cache_c_idx, ab_w_in, a_ln_g, a_ln_b, a_ws, a_bs, b_sinks, ab_w_out, c_w_in, c_w_out, ln1_g, ln1_b, ln2_g, ln2_b, ff_w1, ff_w2):
    wts = (ab_w_in, a_ln_g, a_ln_b, a_ws, a_bs, b_sinks, ab_w_out, c_w_in, c_w_out,
           ln1_g, ln1_b, ln2_g, ln2_b, ff_w1, ff_w2)
    past_len = cache_c_k.shape[2]
    pos_p = jnp.arange(x_prompt.shape[1], dtype=jnp.int32)
    pos_s = past_len + jnp.arange(x_sample.shape[1], dtype=jnp.int32)
    y_p, _, p_b_k, p_b_v, p_c_k, p_c_v, p_c_idx = _trunk(x_prompt, pos_p, None, wts)
    y_s, s_a_v, s_b_k, s_b_v, s_c_k, s_c_v, s_c_idx = _trunk(
        x_sample, pos_s, (cache_b_k, cache_b_v, cache_c_k, cache_c_v, cache_c_idx), wts)
    return (y_p, y_s, p_b_k[:, :, -B_WINDOW:], p_b_v[:, :, -B_WINDOW:], p_c_k, p_c_v, p_c_idx,
            s_a_v, s_b_k, s_b_v, s_c_k, s_c_v, s_c_idx)
```

```python
import functools
import math

import numpy as np
import jax
import jax.numpy as jnp
from jax import lax
from jax.experimental import pallas as pl
from jax.experimental.pallas import tpu as pltpu

F32 = jnp.float32
I32 = jnp.int32
MXU_DTYPE = jnp.bfloat16

HEAD_DIM = 64
CHUNK = 64
ROPE_THETA = 10000.0
LN_EPS = 1e-5
DEPTH = 2
ALPHA = (2.0 * DEPTH) ** 0.25
A_GROUPS = 8
A_CHUNK = 128
B_KV_HEADS = 2
B_GROUP = 4
B_WINDOW = 128
C_KV_HEADS = 4
C_GROUP = 4
IDX_HEADS = 4
IDX_DIM = 64
TOPK_MAX = 256

LANES = 128
KEY_TILE = 512
SUB_TILE = 256
QUERY_BLOCK = 128
INT_MIN = -(2 ** 31)
NEG_BIG = -(2.0 ** 100)
LOG2E = math.log2(math.e)
V_ROWS = 80
VMEM_LIMIT = 56 * 1024 * 1024


def _gelu(x):
    c = math.sqrt(2.0 / math.pi)
    return 0.5 * x * (1.0 + jnp.tanh(c * (x + 0.044715 * (x * x * x))))


def _ln(z, g, b):
    mu = jnp.mean(z, axis=-1, keepdims=True)
    d = z - mu
    var = jnp.mean(d * d, axis=-1, keepdims=True)
    return d * lax.rsqrt(var + LN_EPS) * g + b


def _rope2(x, cos, sin):
    lane = lax.broadcasted_iota(I32, (1, LANES), 1)
    first = (lane % HEAD_DIM) < (HEAD_DIM // 2)
    swapped = jnp.where(first, pltpu.roll(x, LANES - HEAD_DIM // 2, 1), pltpu.roll(x, HEAD_DIM // 2, 1))
    return x * cos + swapped * sin


def _dot(a, b):
    return jnp.dot(a, b, preferred_element_type=F32)


def _dot_nt(a, b):
    return lax.dot_general(a, b, (((1,), (1,)), ((), ())), preferred_element_type=F32)


def _ab_in_kernel(x_ref, w_ref, cos_ref, sin_ref, g_ref, b_ref, u_ref, va_ref, q_ref, k_ref, v_ref):
    y = _dot(x_ref[...].astype(MXU_DTYPE), w_ref[...])
    u_ref[...] = _gelu(y[:, 0:512])
    va_ref[...] = _ln(_gelu(y[:, 512:1024]), g_ref[...], b_ref[...])
    cos = cos_ref[...]
    sin = sin_ref[...]
    for c in range(4):
        lo = 1024 + LANES * c
        q_ref[:, LANES * c:LANES * (c + 1)] = (
            _rope2(y[:, lo:lo + LANES], cos, sin) * (HEAD_DIM ** -0.5)).astype(q_ref.dtype)
    k_ref[...] = _rope2(y[:, 1536:1664], cos, sin)
    v_ref[...] = y[:, 1664:1792]


def _ab_in(x2d, w, cos, sin, g, b, *, t, tm):
    rows = x2d.shape[0]
    nt = t // tm
    row_spec = lambda w_: pl.BlockSpec((tm, w_), lambda i: (i, 0))
    full = lambda a: pl.BlockSpec(a.shape, lambda i: (0,) * a.ndim)
    tab = pl.BlockSpec((tm, LANES), lambda i: (i % nt, 0))
    return pl.pallas_call(
        _ab_in_kernel,
        grid=(rows // tm,),
        in_specs=[row_spec(1024), full(w), tab, tab, full(g), full(b)],
        out_specs=[row_spec(512), row_spec(512), row_spec(512), row_spec(128), row_spec(128)],
        out_shape=[
            jax.ShapeDtypeStruct((rows, 512), F32),
            jax.ShapeDtypeStruct((rows, 512), F32),
            jax.ShapeDtypeStruct((rows, 512), MXU_DTYPE),
            jax.ShapeDtypeStruct((rows, 128), F32),
            jax.ShapeDtypeStruct((rows, 128), F32),
        ],
        compiler_params=pltpu.CompilerParams(
            dimension_semantics=("arbitrary",), vmem_limit_bytes=VMEM_LIMIT),
        name="ab_in",
    )(x2d, w, cos, sin, g, b)


def _ab_mix_kernel(sink_ref, x_ref, u_ref, va_ref, q_ref, k_ref, v_ref, hk_ref, hv_ref,
                   ws_ref, bias_ref, wout_ref, g_ref, b_ref, o_ref, cat_ref, *, rows, cs, mask_first):
    t = pl.program_id(1)
    lane = lax.broadcasted_iota(I32, (1, LANES), 1)
    lo_half = lane < HEAD_DIM

    r_i = lax.broadcasted_iota(I32, (cs, cs), 0)
    c_i = lax.broadcasted_iota(I32, (cs, cs), 1)
    tril = r_i >= c_i
    w_tril = [jnp.where(tril, ws_ref[g], 0.0).astype(MXU_DTYPE) for g in range(A_GROUPS)]
    for c in range(rows // cs):
        rs = slice(c * cs, (c + 1) * cs)
        for p in range(A_GROUPS // 2):
            ls = slice(LANES * p, LANES * (p + 1))
            vp = va_ref[0, rs, ls].astype(MXU_DTYPE)
            gate = jnp.where(lo_half, _dot(w_tril[2 * p], vp), _dot(w_tril[2 * p + 1], vp)) + bias_ref[:, ls]
            cat_ref[rs, ls] = (u_ref[0, rs, ls] * gate).astype(cat_ref.dtype)

    kcat = jnp.concatenate([hk_ref[0], k_ref[0]], axis=0).astype(MXU_DTYPE)
    vcat = jnp.concatenate([hv_ref[0], v_ref[0]], axis=0).astype(MXU_DTYPE)
    nwin = B_WINDOW + CHUNK
    head_of_row = lax.broadcasted_iota(I32, (B_GROUP * CHUNK, 1), 0) // CHUNK
    col = lax.broadcasted_iota(I32, (1, nwin), 1)
    for j in range(rows // CHUNK):
        rs = slice(CHUNK * j, CHUNK * (j + 1))
        kwin = kcat[CHUNK * j:CHUNK * j + nwin]
        vwin = vcat[CHUNK * j:CHUNK * j + nwin]
        outs = []
        for h in range(B_KV_HEADS):
            half = lo_half if h == 0 else jnp.logical_not(lo_half)
            q4 = jnp.concatenate(
                [jnp.where(half, q_ref[0, rs, LANES * g:LANES * (g + 1)], 0).astype(MXU_DTYPE)
                 for g in range(B_GROUP)], axis=0)
            s = _dot_nt(q4, kwin)
            if mask_first:
                s = jnp.where(t * rows + CHUNK * j - B_WINDOW + col >= 0, s, -jnp.inf)
            sink = jnp.zeros((B_GROUP * CHUNK, 1), F32)
            for g in range(B_GROUP):
                sink = jnp.where(head_of_row == g, sink_ref[B_GROUP * h + g], sink)
            m = jnp.maximum(jnp.max(s, axis=-1, keepdims=True), sink)
            e = jnp.exp(s - m)
            p = e / (jnp.sum(e, axis=-1, keepdims=True) + jnp.exp(sink - m))
            outs.append(_dot(p.astype(MXU_DTYPE), vwin))
        for g in range(B_GROUP):
            gs = slice(CHUNK * g, CHUNK * (g + 1))
            cat_ref[rs, 512 + LANES * g:512 + LANES * (g + 1)] = jnp.where(
                lo_half, outs[0][gs], outs[1][gs]).astype(cat_ref.dtype)

    y = _dot(cat_ref[...], wout_ref[...])
    o_ref[0] = _ln(ALPHA * x_ref[0] + y, g_ref[...], b_ref[...])


def _ab_mix(sinks, x, u, va, q, k, v, hist_k, hist_v, ws, bias, wout, g, b, *, rows, cs, hist_from_self):
    bsz, t, _ = x.shape
    full = lambda a: pl.BlockSpec(a.shape, lambda bi, ti: (0,) * a.ndim)
    blk = lambda w_: pl.BlockSpec((1, rows, w_), lambda bi, ti: (bi, ti, 0))
    if hist_from_self:
        per = rows // B_WINDOW
        hist = pl.BlockSpec((1, B_WINDOW, 128), lambda bi, ti: (bi, jnp.maximum(ti * per - 1, 0), 0))
    else:
        hist = pl.BlockSpec((1, B_WINDOW, 128), lambda bi, ti: (bi, 0, 0))
    kern = functools.partial(_ab_mix_kernel, rows=rows, cs=cs, mask_first=hist_from_self)
    return pl.pallas_call(
        kern,
        grid=(bsz, t // rows),
        in_specs=[pl.BlockSpec(memory_space=pltpu.SMEM),
                  blk(1024), blk(512), blk(512), blk(512), blk(128), blk(128), hist, hist,
                  full(ws), full(bias), full(wout), full(g), full(b)],
        out_specs=blk(1024),
        out_shape=jax.ShapeDtypeStruct((bsz, t, 1024), F32),
        scratch_shapes=[pltpu.VMEM((rows, 1024), MXU_DTYPE)],
        compiler_params=pltpu.CompilerParams(
            dimension_semantics=("arbitrary", "arbitrary"), vmem_limit_bytes=VMEM_LIMIT),
        name="ab_mix",
    )(sinks, x, u, va, q, k, v, hist_k, hist_v, ws, bias, wout, g, b)


def _mlp_kernel(x_ref, w1_ref, w2_ref, g_ref, b_ref, o_ref, *, ff_tile):
    x = x_ref[...]
    xb = x.astype(MXU_DTYPE)
    acc = jnp.zeros(x.shape, F32)
    for c in range(w1_ref.shape[1] // ff_tile):
        h = _dot(xb, w1_ref[:, c * ff_tile:(c + 1) * ff_tile])
        h = jnp.square(jnp.maximum(h, 0.0)).astype(MXU_DTYPE)
        acc = acc + _dot(h, w2_ref[c * ff_tile:(c + 1) * ff_tile, :])
    o_ref[...] = _ln(ALPHA * x + acc, g_ref[...], b_ref[...])


def _mlp(x2d, w1, w2, g, b, *, tm):
    rows, d = x2d.shape
    full = lambda a: pl.BlockSpec(a.shape, lambda i: (0,) * a.ndim)
    row = pl.BlockSpec((tm, d), lambda i: (i, 0))
    return pl.pallas_call(
        functools.partial(_mlp_kernel, ff_tile=1024),
        grid=(rows // tm,),
        in_specs=[row, full(w1), full(w2), full(g), full(b)],
        out_specs=row,
        out_shape=jax.ShapeDtypeStruct((rows, d), F32),
        compiler_params=pltpu.CompilerParams(
            dimension_semantics=("arbitrary",), vmem_limit_bytes=VMEM_LIMIT),
        name="mlp",
    )(x2d, w1, w2, g, b)


def _c_in_kernel(x_ref, w_ref, cos_ref, sin_ref, q_ref, k_ref, kb_ref, v_ref, qi_ref, kk_ref, kkb_ref, wi_ref):
    y = _dot(x_ref[...].astype(MXU_DTYPE), w_ref[...])
    cos = cos_ref[...]
    sin = sin_ref[...]
    for c in range(8):
        q_ref[:, LANES * c:LANES * (c + 1)] = (
            _rope2(y[:, LANES * c:LANES * (c + 1)], cos, sin) * (LOG2E * HEAD_DIM ** -0.5)).astype(q_ref.dtype)
    for c in range(2):
        kr = _rope2(y[:, 1024 + LANES * c:1024 + LANES * (c + 1)], cos, sin)
        k_ref[:, LANES * c:LANES * (c + 1)] = kr
        kb_ref[:, LANES * c:LANES * (c + 1)] = kr.astype(kb_ref.dtype)
    v_ref[...] = y[:, 1280:1536]
    for c in range(2):
        qi_ref[:, LANES * c:LANES * (c + 1)] = _rope2(
            y[:, 1536 + LANES * c:1536 + LANES * (c + 1)], cos, sin).astype(qi_ref.dtype)
    kk = _rope2(y[:, 1792:1920], cos, sin)
    kk_ref[...] = kk
    kkb_ref[...] = kk.astype(kkb_ref.dtype)
    wi_ref[...] = y[:, 1920:2048]


def _c_in(x2d, w, cos, sin, *, t, tm):
    rows = x2d.shape[0]
    nt = t // tm
    row_spec = lambda w_: pl.BlockSpec((tm, w_), lambda i: (i, 0))
    full = lambda a: pl.BlockSpec(a.shape, lambda i: (0,) * a.ndim)
    tab = pl.BlockSpec((tm, LANES), lambda i: (i % nt, 0))
    widths = [(1024, MXU_DTYPE), (256, F32), (256, MXU_DTYPE), (256, F32), (256, MXU_DTYPE),
              (128, F32), (128, MXU_DTYPE), (128, F32)]
    return pl.pallas_call(
        _c_in_kernel,
        grid=(rows // tm,),
        in_specs=[row_spec(1024), full(w), tab, tab],
        out_specs=[row_spec(w_) for w_, _ in widths],
        out_shape=[jax.ShapeDtypeStruct((rows, w_), dt) for w_, dt in widths],
        compiler_params=pltpu.CompilerParams(
            dimension_semantics=("arbitrary",), vmem_limit_bytes=VMEM_LIMIT),
        name="c_in",
    )(x2d, w, cos, sin)


def _bit_transpose32(words):
    a = list(words)
    j, m = 16, 0x0000FFFF
    while j:
        k = 0
        while k < 32:
            t = (a[k] ^ lax.shift_right_logical(a[k + j], jnp.int32(j))) & m
            a[k] = a[k] ^ t
            a[k + j] = a[k + j] ^ lax.shift_left(t, jnp.int32(j))
            k = (k + j + 1) & ~j
        j >>= 1
        m = (m ^ (m << j)) & 0xFFFFFFFF
    return a


def _dsa_kernel(x_ref, q_ref, qi_ref, w_ref, k_ref, vt_ref, kk_ref, wout_ref, g_ref, b_ref, o_ref,
                key_ref, planes_ref, alive_ref, bias_ref, rhs_ref, m_ref, acc_ref, ot_ref,
                *, causal, n_keys, ksel):
    tk, st_ = KEY_TILE, SUB_TILE
    i = pl.program_id(1)
    lane = lax.broadcasted_iota(I32, (1, LANES), 1)
    lo_half = lane < HEAD_DIM
    hi_half = jnp.logical_not(lo_half)
    if causal:
        limit = CHUNK * (2 * i + jnp.where(lo_half, jnp.int32(0), jnp.int32(1)) + 1)
        n_tiles = (QUERY_BLOCK * (i + 1) + tk - 1) // tk
    else:
        limit = jnp.full((1, LANES), n_keys, I32)
        n_tiles = (n_keys + tk - 1) // tk
    sub = lax.broadcasted_iota(I32, (st_, LANES), 0)

    def sub_off(t, s):
        return pl.multiple_of(t * tk + s * st_, st_)

    qi_blk = qi_ref[0]
    qi4 = jnp.concatenate(
        [jnp.where(lo_half if h % 2 == 0 else hi_half, qi_blk[:, LANES * (h // 2):LANES * (h // 2 + 1)], 0)
         for h in range(IDX_HEADS)], axis=0)
    w = w_ref[0]

    def score_tile(t, carry):
        for s in range(tk // st_):
            off = sub_off(t, s)
            sc4 = _dot_nt(kk_ref[0, pl.ds(off, st_), :], qi4)
            sc = jnp.maximum(sc4[:, 0:LANES], 0.0) * w[0:1]
            for h in range(1, IDX_HEADS):
                sc = sc + jnp.maximum(sc4[:, LANES * h:LANES * (h + 1)], 0.0) * w[h:h + 1]
            bits = pltpu.bitcast(sc, I32)
            key = jnp.where(bits < 0, bits ^ 0x7FFFFFFF, bits)
            key = jnp.where(sub + off < limit, key, INT_MIN)
            key_ref[pl.ds(off, st_), :] = key
            ukey = key ^ INT_MIN
            planes = _bit_transpose32([ukey[8 * r:8 * (r + 1)] for r in range(32)])
            row = pl.multiple_of(t * (tk // 32) + s * (st_ // 32), 8)
            for b in range(32):
                planes_ref[b, pl.ds(row, 8), :] = planes[b]
        return carry

    lax.fori_loop(0, n_tiles, score_tile, 0)

    def plane_count(b, flip, first):
        def body(t, acc):
            for s in range(tk // st_):
                row = pl.multiple_of(t * (tk // 32) + s * (st_ // 32), 8)
                if first:
                    alive = jnp.full((8, LANES), -1, I32)
                else:
                    alive = alive_ref[pl.ds(row, 8), :] & (planes_ref[b - 1, pl.ds(row, 8), :] ^ flip)
                alive_ref[pl.ds(row, 8), :] = alive
                acc = acc + lax.population_count(alive & planes_ref[b, pl.ds(row, 8), :])
            return acc
        acc = lax.fori_loop(0, n_tiles, body, jnp.zeros((8, LANES), I32))
        return jnp.sum(acc, axis=0, keepdims=True)

    def decide(b, cnt, above, ubits):
        take = above + cnt >= ksel
        ubits = ubits | jnp.where(take, lax.shift_left(jnp.int32(1), jnp.int32(31) - b), jnp.int32(0))
        above = jnp.where(take, above, above + cnt)
        return above, ubits, jnp.where(take, jnp.int32(0), jnp.int32(-1))

    zero_row = jnp.zeros((1, LANES), I32)
    state = decide(0, plane_count(0, None, True), zero_row, zero_row)

    def bit_step(b, state):
        above, ubits, flip = state
        return decide(b, plane_count(b, flip, False), above, ubits)

    above, ubits, _ = lax.fori_loop(1, 32, bit_step, state)
    thr = ubits ^ INT_MIN

    need = ksel - above
    need = jnp.where(thr == INT_MIN, jnp.int32(0), need).astype(F32)
    tril = (lax.broadcasted_iota(I32, (st_, st_), 0) >= lax.broadcasted_iota(I32, (st_, st_), 1)
            ).astype(F32).astype(MXU_DTYPE)
    keep = jnp.zeros((st_, LANES), F32)
    drop = jnp.full((st_, LANES), NEG_BIG, F32)
    f_one = jnp.ones((st_, LANES), F32)

    def bias_tile(t, seen):
        offs = [sub_off(t, s) for s in range(tk // st_)]
        blks = [key_ref[pl.ds(off, st_), :] for off in offs]
        ranks = [_dot(tril, jnp.where(blk == thr, f_one, keep).astype(MXU_DTYPE)) for blk in blks]
        for off, blk, rank in zip(offs, blks, ranks):
            rank = rank + seen
            tie_bias = jnp.where(blk == thr, jnp.where(rank <= need, keep, drop), drop)
            bias_ref[pl.ds(off, st_), :] = jnp.where(blk > thr, keep, tie_bias).astype(bias_ref.dtype)
            seen = rank[st_ - 1:st_, :]
        return seen

    lax.fori_loop(0, n_tiles, bias_tile, jnp.zeros((1, LANES), F32))

    eye4 = (lax.broadcasted_iota(I32, (LANES, C_GROUP * LANES), 1) % LANES
            == lax.broadcasted_iota(I32, (LANES, C_GROUP * LANES), 0)).astype(F32).astype(MXU_DTYPE)
    for h in range(C_KV_HEADS):
        half = lo_half if h % 2 == 0 else hi_half
        q4t = jnp.concatenate(
            [jnp.transpose(jnp.where(half, q_ref[0, :, LANES * (4 * (h // 2) + g):LANES * (4 * (h // 2) + g + 1)],
                                     0).astype(F32)) for g in range(C_GROUP)], axis=1)
        rhs_ref[h] = jnp.concatenate([q4t.astype(MXU_DTYPE), eye4], axis=0)
    m_ref[...] = jnp.full(m_ref.shape, NEG_BIG, F32)
    acc_ref[...] = jnp.zeros(acc_ref.shape, F32)

    chains = [(h, s) for s in range(tk // st_) for h in range(C_KV_HEADS)]

    def tile_scores(t):
        out = []
        for h, s in chains:
            off = sub_off(t, s)
            kcol = slice(LANES * (h // 2), LANES * (h // 2 + 1))
            lhs = jnp.concatenate([k_ref[0, pl.ds(off, st_), kcol], bias_ref[pl.ds(off, st_), :]], axis=1)
            out.append(_dot(lhs, rhs_ref[h]))
        return out

    def values(t, h, s):
        return vt_ref[0, V_ROWS * h:V_ROWS * (h + 1), pl.ds(sub_off(t, s), st_)]

    def attn_tile(t, carry):
        for (h, s), sc in zip(chains, tile_scores(t)):
            m_old = m_ref[h]
            m_new = jnp.maximum(m_old, jnp.max(sc, axis=0, keepdims=True))
            p = jnp.exp2(sc - m_new).astype(MXU_DTYPE)
            acc_ref[h] = jnp.exp2(m_old - m_new) * acc_ref[h] + _dot(values(t, h, s), p)
            m_ref[h] = m_new
        return carry

    lax.fori_loop(0, n_tiles, attn_tile, 0)

    for h in range(C_KV_HEADS):
        acc = acc_ref[h]
        o = acc[0:HEAD_DIM] / acc[HEAD_DIM:HEAD_DIM + 1]
        for g in range(C_GROUP):
            slot = 8 * (h // 2) + 2 * g + (h % 2)
            ot_ref[HEAD_DIM * slot:HEAD_DIM * (slot + 1), :] = o[:, LANES * g:LANES * (g + 1)]

    attn = jnp.transpose(ot_ref[...]).astype(MXU_DTYPE)
    y = _dot(attn, wout_ref[...])
    o_ref[0] = _ln(ALPHA * x_ref[0] + y, g_ref[...], b_ref[...])


def _dsa(x, q, qi, w_t, k, v_t, kk, wout, g, b, *, causal, n_keys, ksel):
    bsz, t, _ = x.shape
    nk = k.shape[1]
    qb = QUERY_BLOCK
    full = lambda a: pl.BlockSpec(a.shape, lambda bi, qi_: (0,) * a.ndim)
    blk = lambda w_: pl.BlockSpec((1, qb, w_), lambda bi, qi_: (bi, qi_, 0))
    per_b = lambda a: pl.BlockSpec((1,) + a.shape[1:], lambda bi, qi_: (bi, 0, 0))
    kern = functools.partial(_dsa_kernel, causal=causal, n_keys=n_keys, ksel=ksel)
    return pl.pallas_call(
        kern,
        grid=(bsz, t // qb),
        in_specs=[blk(1024), blk(1024), blk(256),
                  pl.BlockSpec((1, 8, qb), lambda bi, qi_: (bi, 0, qi_)),
                  per_b(k), per_b(v_t), per_b(kk), full(wout), full(g), full(b)],
        out_specs=blk(1024),
        out_shape=jax.ShapeDtypeStruct((bsz, t, 1024), F32),
        scratch_shapes=[pltpu.VMEM((nk, LANES), I32),
                        pltpu.VMEM((32, nk // 32, LANES), I32),
                        pltpu.VMEM((nk // 32, LANES), I32),
                        pltpu.VMEM((nk, LANES), MXU_DTYPE),
                        pltpu.VMEM((C_KV_HEADS, 2 * LANES, C_GROUP * LANES), MXU_DTYPE),
                        pltpu.VMEM((C_KV_HEADS, 1, C_GROUP * LANES), F32),
                        pltpu.VMEM((C_KV_HEADS, V_ROWS, C_GROUP * LANES), F32),
                        pltpu.VMEM((1024, LANES), F32)],
        compiler_params=pltpu.CompilerParams(
            dimension_semantics=("arbitrary", "arbitrary"), vmem_limit_bytes=VMEM_LIMIT),
        name="dsa",
    )(x, q, qi, w_t, k, v_t, kk, wout, g, b)


def _slot_perm(head_of_slot):
    return np.concatenate([np.arange(HEAD_DIM * j, HEAD_DIM * (j + 1)) for j in head_of_slot])


_AB_Q_PERM = _slot_perm([B_GROUP * (s % 2) + s // 2 for s in range(8)])
_C_Q_PERM = _slot_perm([4 * (2 * (s // 8) + (s % 8) % 2) + (s % 8) // 2 for s in range(16)])


def _rope_tables(pos):
    half = HEAD_DIM // 2
    inv_freq = jnp.exp(-math.log(ROPE_THETA) * jnp.arange(half, dtype=F32) / half)
    ang = pos.astype(F32)[:, None] * inv_freq[None, :]
    cos, sin = jnp.cos(ang), jnp.sin(ang)
    return jnp.concatenate([cos] * 4, axis=1), jnp.concatenate([-sin, sin, -sin, sin], axis=1)


def _value_operand(v):
    bsz, n, _ = v.shape
    vt = jnp.swapaxes(v, 1, 2).reshape(bsz, C_KV_HEADS, HEAD_DIM, n)
    ones = jnp.ones((bsz, C_KV_HEADS, 1, n), v.dtype)
    zeros = jnp.zeros((bsz, C_KV_HEADS, V_ROWS - HEAD_DIM - 1, n), v.dtype)
    return jnp.concatenate([vt, ones, zeros], axis=2).reshape(bsz, C_KV_HEADS * V_ROWS, n)


def _round_up(n, m):
    return (n + m - 1) // m * m


def _trunk(x, pos, caches, wts):
    (ab_w_in, a_ln_g, a_ln_b, a_ws, a_bs, b_sinks, ab_w_out, c_w_in, c_w_out,
     ln1_g, ln1_b, ln2_g, ln2_b, ff_w1, ff_w2) = wts
    bsz, t, d = x.shape
    sample = caches is not None
    cos, sin = _rope_tables(pos)
    tm = min(512, t)
    row = lambda a: a.reshape(1, -1)

    w_in = jnp.concatenate([ab_w_in[0][:, :1024], ab_w_in[0][:, 1024:1536][:, _AB_Q_PERM],
                            ab_w_in[0][:, 1536:]], axis=1).astype(MXU_DTYPE)
    w_out = jnp.concatenate([ab_w_out[0][:512], ab_w_out[0][512:][_AB_Q_PERM]], axis=0).astype(MXU_DTYPE)
    u, va, q, k, v = _ab_in(x.reshape(bsz * t, d), w_in, cos, sin, row(a_ln_g[0]), row(a_ln_b[0]), t=t, tm=tm)
    r3 = lambda a: a.reshape(bsz, t, a.shape[-1])
    u, va, q, k, v = r3(u), r3(va), r3(q), r3(k), r3(v)
    cs = min(A_CHUNK, t)
    ws = a_ws[0][:, :cs, :cs]
    bias = jnp.repeat(a_bs[0][:, :cs].T, HEAD_DIM, axis=1)
    if sample:
        hist_k = caches[0][0].reshape(bsz, B_WINDOW, 128)
        hist_v = caches[1][0].reshape(bsz, B_WINDOW, 128)
        rows = t
    else:
        hist_k, hist_v = k, v
        rows = min(256, t)
    x = _ab_mix(b_sinks[0], x, u, va, q, k, v, hist_k, hist_v, ws, bias, w_out,
                row(ln1_g[0]), row(ln1_b[0]), rows=rows, cs=cs, hist_from_self=not sample)
    x = _mlp(x.reshape(bsz * t, d), ff_w1[0].astype(MXU_DTYPE), ff_w2[0].astype(MXU_DTYPE),
             row(ln2_g[0]), row(ln2_b[0]), tm=tm).reshape(bsz, t, d)
    b_k = k.reshape(bsz, t, B_KV_HEADS, HEAD_DIM)
    b_v = v.reshape(bsz, t, B_KV_HEADS, HEAD_DIM)

    cw = c_w_in[0]
    w_in = jnp.concatenate(
        [cw[:, :1024][:, _C_Q_PERM], cw[:, 1024:1792], cw[:, 1792:1856], cw[:, 1792:1856], cw[:, 1856:1860],
         jnp.zeros((d, LANES - IDX_HEADS), F32)], axis=1).astype(MXU_DTYPE)
    w_out = c_w_out[0][_C_Q_PERM].astype(MXU_DTYPE)
    q, k, kb, v, qi, kk, kkb, wi = _c_in(x.reshape(bsz * t, d), w_in, cos, sin, t=t, tm=tm)
    q, k, kb, v, qi, kk, kkb, wi = (r3(a) for a in (q, k, kb, v, qi, kk, kkb, wi))
    ki = kk[:, :, :IDX_DIM]
    w_t = jnp.swapaxes(wi[:, :, :8], 1, 2) * (IDX_DIM ** -0.5 * IDX_HEADS ** -0.5)
    if sample:
        keys_k = jnp.concatenate([caches[2][0].reshape(bsz, -1, 256).astype(MXU_DTYPE), kb], axis=1)
        keys_v = jnp.concatenate([caches[3][0].reshape(bsz, -1, 256), v], axis=1).astype(MXU_DTYPE)
        ci = caches[4][0].astype(MXU_DTYPE)
        keys_i = jnp.concatenate([jnp.concatenate([ci, ci], axis=-1), kkb], axis=1)
        n_keys = keys_k.shape[1]
        pad_k = _round_up(n_keys, KEY_TILE) - n_keys
        keys_k, keys_v, keys_i = (jnp.pad(a, ((0, 0), (0, pad_k), (0, 0))) for a in (keys_k, keys_v, keys_i))
        pad_q = QUERY_BLOCK - t
        padq = lambda a: jnp.pad(a, ((0, 0), (0, pad_q), (0, 0)))
        xo = _dsa(padq(x), padq(q), padq(qi), jnp.pad(w_t, ((0, 0), (0, 0), (0, pad_q))), keys_k,
                  _value_operand(keys_v), keys_i, w_out, row(ln1_g[1]), row(ln1_b[1]),
                  causal=False, n_keys=n_keys, ksel=min(TOPK_MAX, n_keys // 4))[:, :t]
    else:
        xo = _dsa(x, q, qi, w_t, kb, _value_operand(v.astype(MXU_DTYPE)), kkb, w_out,
                  row(ln1_g[1]), row(ln1_b[1]), causal=True, n_keys=t, ksel=min(TOPK_MAX, t // 4))
    x = _mlp(xo.reshape(bsz * t, d), ff_w1[1].astype(MXU_DTYPE), ff_w2[1].astype(MXU_DTYPE),
             row(ln2_g[1]), row(ln2_b[1]), tm=tm).reshape(bsz, t, d)
    c_k = k.reshape(bsz, t, C_KV_HEADS, HEAD_DIM)
    c_v = v.reshape(bsz, t, C_KV_HEADS, HEAD_DIM)
    return x, va[None], b_k[None], b_v[None], c_k[None], c_v[None], ki[None]


def kernel(x_prompt, x_sample, cache_b_k, cache_b_v, cache_c_k, cache_c_v, cache_c_idx, ab_w_in, a_ln_g, a_ln_b, a_ws, a_bs, b_sinks, ab_w_out, c_w_in, c_w_out, ln1_g, ln1_b, ln2_g, ln2_b, ff_w1, ff_w2):
    wts = (ab_w_in, a_ln_g, a_ln_b, a_ws, a_bs, b_sinks, ab_w_out, c_w_in, c_w_out,
           ln1_g, ln1_b, ln2_g, ln2_b, ff_w1, ff_w2)
    past_len = cache_c_k.shape[2]
    pos_p = jnp.arange(x_prompt.shape[1], dtype=jnp.int32)
    pos_s = past_len + jnp.arange(x_sample.shape[1], dtype=jnp.int32)
    y_p, _, p_b_k, p_b_v, p_c_k, p_c_v, p_c_idx = _trunk(x_prompt, pos_p, None, wts)
    y_s, s_a_v, s_b_k, s_b_v, s_c_k, s_c_v, s_c_idx = _trunk(
        x_sample, pos_s, (cache_b_k, cache_b_v, cache_c_k, cache_c_v, cache_c_idx), wts)
    return (y_p, y_s, p_b_k[:, :, -B_WINDOW:], p_b_v[:, :, -B_WINDOW:], p_c_k, p_c_v, p_c_idx,
            s_a_v, s_b_k, s_b_v, s_c_k, s_c_v, s_c_idx)
```

```python
import functools
import math

import numpy as np
import jax
import jax.numpy as jnp
from jax import lax
from jax.experimental import pallas as pl
from jax.experimental.pallas import tpu as pltpu

F32 = jnp.float32
I32 = jnp.int32
MXU_DTYPE = jnp.bfloat16

HEAD_DIM = 64
CHUNK = 64
ROPE_THETA = 10000.0
LN_EPS = 1e-5
DEPTH = 2
ALPHA = (2.0 * DEPTH) ** 0.25
A_GROUPS = 8
A_CHUNK = 128
B_KV_HEADS = 2
B_GROUP = 4
B_WINDOW = 128
C_KV_HEADS = 4
C_GROUP = 4
IDX_HEADS = 4
IDX_DIM = 64
TOPK_MAX = 256

LANES = 128
KEY_TILE = 1024
SUB_TILE = 256
QUERY_BLOCK = 128
INT_MIN = -(2 ** 31)
NEG_BIG = -(2.0 ** 100)
LOG2E = math.log2(math.e)
V_ROWS = 80
VMEM_LIMIT = 56 * 1024 * 1024


def _gelu(x):
    c = math.sqrt(2.0 / math.pi)
    return 0.5 * x * (1.0 + jnp.tanh(c * (x + 0.044715 * (x * x * x))))


def _ln(z, g, b):
    mu = jnp.mean(z, axis=-1, keepdims=True)
    d = z - mu
    var = jnp.mean(d * d, axis=-1, keepdims=True)
    return d * lax.rsqrt(var + LN_EPS) * g + b


def _rope2(x, cos, sin):
    lane = lax.broadcasted_iota(I32, (1, LANES), 1)
    first = (lane % HEAD_DIM) < (HEAD_DIM // 2)
    swapped = jnp.where(first, pltpu.roll(x, LANES - HEAD_DIM // 2, 1), pltpu.roll(x, HEAD_DIM // 2, 1))
    return x * cos + swapped * sin


def _dot(a, b):
    return jnp.dot(a, b, preferred_element_type=F32)


def _dot_nt(a, b):
    return lax.dot_general(a, b, (((1,), (1,)), ((), ())), preferred_element_type=F32)


def _ab_in_kernel(x_ref, w_ref, cos_ref, sin_ref, g_ref, b_ref, u_ref, va_ref, q_ref, k_ref, v_ref):
    y = _dot(x_ref[...].astype(MXU_DTYPE), w_ref[...])
    u_ref[...] = _gelu(y[:, 0:512])
    va_ref[...] = _ln(_gelu(y[:, 512:1024]), g_ref[...], b_ref[...])
    cos = cos_ref[...]
    sin = sin_ref[...]
    for c in range(4):
        lo = 1024 + LANES * c
        q_ref[:, LANES * c:LANES * (c + 1)] = (
            _rope2(y[:, lo:lo + LANES], cos, sin) * (HEAD_DIM ** -0.5)).astype(q_ref.dtype)
    k_ref[...] = _rope2(y[:, 1536:1664], cos, sin)
    v_ref[...] = y[:, 1664:1792]


def _ab_in(x2d, w, cos, sin, g, b, *, t, tm):
    rows = x2d.shape[0]
    nt = t // tm
    row_spec = lambda w_: pl.BlockSpec((tm, w_), lambda i: (i, 0))
    full = lambda a: pl.BlockSpec(a.shape, lambda i: (0,) * a.ndim)
    tab = pl.BlockSpec((tm, LANES), lambda i: (i % nt, 0))
    return pl.pallas_call(
        _ab_in_kernel,
        grid=(rows // tm,),
        in_specs=[row_spec(1024), full(w), tab, tab, full(g), full(b)],
        out_specs=[row_spec(512), row_spec(512), row_spec(512), row_spec(128), row_spec(128)],
        out_shape=[
            jax.ShapeDtypeStruct((rows, 512), F32),
            jax.ShapeDtypeStruct((rows, 512), F32),
            jax.ShapeDtypeStruct((rows, 512), MXU_DTYPE),
            jax.ShapeDtypeStruct((rows, 128), F32),
            jax.ShapeDtypeStruct((rows, 128), F32),
        ],
        compiler_params=pltpu.CompilerParams(
            dimension_semantics=("arbitrary",), vmem_limit_bytes=VMEM_LIMIT),
        name="ab_in",
    )(x2d, w, cos, sin, g, b)


def _ab_mix_kernel(sink_ref, x_ref, u_ref, va_ref, q_ref, k_ref, v_ref, hk_ref, hv_ref,
                   ws_ref, bias_ref, wout_ref, g_ref, b_ref, o_ref, cat_ref, *, rows, cs, mask_first):
    t = pl.program_id(1)
    lane = lax.broadcasted_iota(I32, (1, LANES), 1)
    lo_half = lane < HEAD_DIM

    r_i = lax.broadcasted_iota(I32, (cs, cs), 0)
    c_i = lax.broadcasted_iota(I32, (cs, cs), 1)
    tril = r_i >= c_i
    w_tril = [jnp.where(tril, ws_ref[g], 0.0).astype(MXU_DTYPE) for g in range(A_GROUPS)]
    for c in range(rows // cs):
        rs = slice(c * cs, (c + 1) * cs)
        for p in range(A_GROUPS // 2):
            ls = slice(LANES * p, LANES * (p + 1))
            vp = va_ref[0, rs, ls].astype(MXU_DTYPE)
            gate = jnp.where(lo_half, _dot(w_tril[2 * p], vp), _dot(w_tril[2 * p + 1], vp)) + bias_ref[:, ls]
            cat_ref[rs, ls] = (u_ref[0, rs, ls] * gate).astype(cat_ref.dtype)

    kcat = jnp.concatenate([hk_ref[0], k_ref[0]], axis=0).astype(MXU_DTYPE)
    vcat = jnp.concatenate([hv_ref[0], v_ref[0]], axis=0).astype(MXU_DTYPE)
    nwin = B_WINDOW + CHUNK
    head_of_row = lax.broadcasted_iota(I32, (B_GROUP * CHUNK, 1), 0) // CHUNK
    col = lax.broadcasted_iota(I32, (1, nwin), 1)
    for j in range(rows // CHUNK):
        rs = slice(CHUNK * j, CHUNK * (j + 1))
        kwin = kcat[CHUNK * j:CHUNK * j + nwin]
        vwin = vcat[CHUNK * j:CHUNK * j + nwin]
        outs = []
        for h in range(B_KV_HEADS):
            half = lo_half if h == 0 else jnp.logical_not(lo_half)
            q4 = jnp.concatenate(
                [jnp.where(half, q_ref[0, rs, LANES * g:LANES * (g + 1)], 0).astype(MXU_DTYPE)
                 for g in range(B_GROUP)], axis=0)
            s = _dot_nt(q4, kwin)
            if mask_first:
                s = jnp.where(t * rows + CHUNK * j - B_WINDOW + col >= 0, s, -jnp.inf)
            sink = jnp.zeros((B_GROUP * CHUNK, 1), F32)
            for g in range(B_GROUP):
                sink = jnp.where(head_of_row == g, sink_ref[B_GROUP * h + g], sink)
            m = jnp.maximum(jnp.max(s, axis=-1, keepdims=True), sink)
            e = jnp.exp(s - m)
            p = e / (jnp.sum(e, axis=-1, keepdims=True) + jnp.exp(sink - m))
            outs.append(_dot(p.astype(MXU_DTYPE), vwin))
        for g in range(B_GROUP):
            gs = slice(CHUNK * g, CHUNK * (g + 1))
            cat_ref[rs, 512 + LANES * g:512 + LANES * (g + 1)] = jnp.where(
                lo_half, outs[0][gs], outs[1][gs]).astype(cat_ref.dtype)

    y = _dot(cat_ref[...], wout_ref[...])
    o_ref[0] = _ln(ALPHA * x_ref[0] + y, g_ref[...], b_ref[...])


def _ab_mix(sinks, x, u, va, q, k, v, hist_k, hist_v, ws, bias, wout, g, b, *, rows, cs, hist_from_self):
    bsz, t, _ = x.shape
    full = lambda a: pl.BlockSpec(a.shape, lambda bi, ti: (0,) * a.ndim)
    blk = lambda w_: pl.BlockSpec((1, rows, w_), lambda bi, ti: (bi, ti, 0))
    if hist_from_self:
        per = rows // B_WINDOW
        hist = pl.BlockSpec((1, B_WINDOW, 128), lambda bi, ti: (bi, jnp.maximum(ti * per - 1, 0), 0))
    else:
        hist = pl.BlockSpec((1, B_WINDOW, 128), lambda bi, ti: (bi, 0, 0))
    kern = functools.partial(_ab_mix_kernel, rows=rows, cs=cs, mask_first=hist_from_self)
    return pl.pallas_call(
        kern,
        grid=(bsz, t // rows),
        in_specs=[pl.BlockSpec(memory_space=pltpu.SMEM),
                  blk(1024), blk(512), blk(512), blk(512), blk(128), blk(128), hist, hist,
                  full(ws), full(bias), full(wout), full(g), full(b)],
        out_specs=blk(1024),
        out_shape=jax.ShapeDtypeStruct((bsz, t, 1024), F32),
        scratch_shapes=[pltpu.VMEM((rows, 1024), MXU_DTYPE)],
        compiler_params=pltpu.CompilerParams(
            dimension_semantics=("arbitrary", "arbitrary"), vmem_limit_bytes=VMEM_LIMIT),
        name="ab_mix",
    )(sinks, x, u, va, q, k, v, hist_k, hist_v, ws, bias, wout, g, b)


def _mlp_kernel(x_ref, w1_ref, w2_ref, g_ref, b_ref, o_ref, *, ff_tile):
    x = x_ref[...]
    xb = x.astype(MXU_DTYPE)
    acc = jnp.zeros(x.shape, F32)
    for c in range(w1_ref.shape[1] // ff_tile):
        h = _dot(xb, w1_ref[:, c * ff_tile:(c + 1) * ff_tile])
        h = jnp.square(jnp.maximum(h, 0.0)).astype(MXU_DTYPE)
        acc = acc + _dot(h, w2_ref[c * ff_tile:(c + 1) * ff_tile, :])
    o_ref[...] = _ln(ALPHA * x + acc, g_ref[...], b_ref[...])


def _mlp(x2d, w1, w2, g, b, *, tm):
    rows, d = x2d.shape
    full = lambda a: pl.BlockSpec(a.shape, lambda i: (0,) * a.ndim)
    row = pl.BlockSpec((tm, d), lambda i: (i, 0))
    return pl.pallas_call(
        functools.partial(_mlp_kernel, ff_tile=1024),
        grid=(rows // tm,),
        in_specs=[row, full(w1), full(w2), full(g), full(b)],
        out_specs=row,
        out_shape=jax.ShapeDtypeStruct((rows, d), F32),
        compiler_params=pltpu.CompilerParams(
            dimension_semantics=("arbitrary",), vmem_limit_bytes=VMEM_LIMIT),
        name="mlp",
    )(x2d, w1, w2, g, b)


def _c_in_kernel(x_ref, w_ref, cos_ref, sin_ref, q_ref, k_ref, kb_ref, v_ref, qi_ref, kk_ref, kkb_ref, wi_ref):
    y = _dot(x_ref[...].astype(MXU_DTYPE), w_ref[...])
    cos = cos_ref[...]
    sin = sin_ref[...]
    for c in range(8):
        q_ref[:, LANES * c:LANES * (c + 1)] = (
            _rope2(y[:, LANES * c:LANES * (c + 1)], cos, sin) * (LOG2E * HEAD_DIM ** -0.5)).astype(q_ref.dtype)
    for c in range(2):
        kr = _rope2(y[:, 1024 + LANES * c:1024 + LANES * (c + 1)], cos, sin)
        k_ref[:, LANES * c:LANES * (c + 1)] = kr
        kb_ref[:, LANES * c:LANES * (c + 1)] = kr.astype(kb_ref.dtype)
    v_ref[...] = y[:, 1280:1536]
    for c in range(2):
        qi_ref[:, LANES * c:LANES * (c + 1)] = _rope2(
            y[:, 1536 + LANES * c:1536 + LANES * (c + 1)], cos, sin).astype(qi_ref.dtype)
    kk = _rope2(y[:, 1792:1920], cos, sin)
    kk_ref[...] = kk
    kkb_ref[...] = kk.astype(kkb_ref.dtype)
    wi_ref[...] = y[:, 1920:2048]


def _c_in(x2d, w, cos, sin, *, t, tm):
    rows = x2d.shape[0]
    nt = t // tm
    row_spec = lambda w_: pl.BlockSpec((tm, w_), lambda i: (i, 0))
    full = lambda a: pl.BlockSpec(a.shape, lambda i: (0,) * a.ndim)
    tab = pl.BlockSpec((tm, LANES), lambda i: (i % nt, 0))
    widths = [(1024, MXU_DTYPE), (256, F32), (256, MXU_DTYPE), (256, F32), (256, MXU_DTYPE),
              (128, F32), (128, MXU_DTYPE), (128, F32)]
    return pl.pallas_call(
        _c_in_kernel,
        grid=(rows // tm,),
        in_specs=[row_spec(1024), full(w), tab, tab],
        out_specs=[row_spec(w_) for w_, _ in widths],
        out_shape=[jax.ShapeDtypeStruct((rows, w_), dt) for w_, dt in widths],
        compiler_params=pltpu.CompilerParams(
            dimension_semantics=("arbitrary",), vmem_limit_bytes=VMEM_LIMIT),
        name="c_in",
    )(x2d, w, cos, sin)


def _bit_transpose32(words):
    a = list(words)
    j, m = 16, 0x0000FFFF
    while j:
        k = 0
        while k < 32:
            t = (a[k] ^ lax.shift_right_logical(a[k + j], jnp.int32(j))) & m
            a[k] = a[k] ^ t
            a[k + j] = a[k + j] ^ lax.shift_left(t, jnp.int32(j))
            k = (k + j + 1) & ~j
        j >>= 1
        m = (m ^ (m << j)) & 0xFFFFFFFF
    return a


def _dsa_kernel(x_ref, q_ref, qi_ref, w_ref, k_ref, vt_ref, kk_ref, wout_ref, g_ref, b_ref, o_ref,
                key_ref, planes_ref, alive_ref, bias_ref, rhs_ref, m_ref, acc_ref, ot_ref,
                *, causal, n_keys, ksel):
    tk, st_ = KEY_TILE, SUB_TILE
    i = pl.program_id(1)
    lane = lax.broadcasted_iota(I32, (1, LANES), 1)
    lo_half = lane < HEAD_DIM
    hi_half = jnp.logical_not(lo_half)
    if causal:
        limit = CHUNK * (2 * i + jnp.where(lo_half, jnp.int32(0), jnp.int32(1)) + 1)
        n_tiles = (QUERY_BLOCK * (i + 1) + tk - 1) // tk
    else:
        limit = jnp.full((1, LANES), n_keys, I32)
        n_tiles = (n_keys + tk - 1) // tk
    sub = lax.broadcasted_iota(I32, (st_, LANES), 0)

    def sub_off(t, s):
        return pl.multiple_of(t * tk + s * st_, st_)

    qi_blk = qi_ref[0]
    qi4 = jnp.concatenate(
        [jnp.where(lo_half if h % 2 == 0 else hi_half, qi_blk[:, LANES * (h // 2):LANES * (h // 2 + 1)], 0)
         for h in range(IDX_HEADS)], axis=0)
    w = w_ref[0]

    def score_tile(t, carry):
        for s in range(tk // st_):
            off = sub_off(t, s)
            sc4 = _dot_nt(kk_ref[0, pl.ds(off, st_), :], qi4)
            sc = jnp.maximum(sc4[:, 0:LANES], 0.0) * w[0:1]
            for h in range(1, IDX_HEADS):
                sc = sc + jnp.maximum(sc4[:, LANES * h:LANES * (h + 1)], 0.0) * w[h:h + 1]
            bits = pltpu.bitcast(sc, I32)
            key = jnp.where(bits < 0, bits ^ 0x7FFFFFFF, bits)
            key = jnp.where(sub + off < limit, key, INT_MIN)
            key_ref[pl.ds(off, st_), :] = key
            ukey = key ^ INT_MIN
            planes = _bit_transpose32([ukey[8 * r:8 * (r + 1)] for r in range(32)])
            row = pl.multiple_of(t * (tk // 32) + s * (st_ // 32), 8)
            for b in range(32):
                planes_ref[b, pl.ds(row, 8), :] = planes[b]
        return carry

    lax.fori_loop(0, n_tiles, score_tile, 0)

    def plane_count(b, flip, first):
        def body(t, acc):
            for s in range(tk // st_):
                row = pl.multiple_of(t * (tk // 32) + s * (st_ // 32), 8)
                if first:
                    alive = jnp.full((8, LANES), -1, I32)
                else:
                    alive = alive_ref[pl.ds(row, 8), :] & (planes_ref[b - 1, pl.ds(row, 8), :] ^ flip)
                alive_ref[pl.ds(row, 8), :] = alive
                acc = acc + lax.population_count(alive & planes_ref[b, pl.ds(row, 8), :])
            return acc
        acc = lax.fori_loop(0, n_tiles, body, jnp.zeros((8, LANES), I32))
        return jnp.sum(acc, axis=0, keepdims=True)

    def decide(b, cnt, above, ubits):
        take = above + cnt >= ksel
        ubits = ubits | jnp.where(take, lax.shift_left(jnp.int32(1), jnp.int32(31) - b), jnp.int32(0))
        above = jnp.where(take, above, above + cnt)
        return above, ubits, jnp.where(take, jnp.int32(0), jnp.int32(-1))

    zero_row = jnp.zeros((1, LANES), I32)
    state = decide(0, plane_count(0, None, True), zero_row, zero_row)

    def bit_step(b, state):
        above, ubits, flip = state
        return decide(b, plane_count(b, flip, False), above, ubits)

    above, ubits, _ = lax.fori_loop(1, 32, bit_step, state)
    thr = ubits ^ INT_MIN

    need = ksel - above
    need = jnp.where(thr == INT_MIN, jnp.int32(0), need).astype(F32)
    tril = (lax.broadcasted_iota(I32, (st_, st_), 0) >= lax.broadcasted_iota(I32, (st_, st_), 1)
            ).astype(F32).astype(MXU_DTYPE)
    keep = jnp.zeros((st_, LANES), F32)
    drop = jnp.full((st_, LANES), NEG_BIG, F32)
    f_one = jnp.ones((st_, LANES), F32)

    def bias_tile(t, seen):
        offs = [sub_off(t, s) for s in range(tk // st_)]
        blks = [key_ref[pl.ds(off, st_), :] for off in offs]
        ranks = [_dot(tril, jnp.where(blk == thr, f_one, keep).astype(MXU_DTYPE)) for blk in blks]
        for off, blk, rank in zip(offs, blks, ranks):
            rank = rank + seen
            tie_bias = jnp.where(blk == thr, jnp.where(rank <= need, keep, drop), drop)
            bias_ref[pl.ds(off, st_), :] = jnp.where(blk > thr, keep, tie_bias).astype(bias_ref.dtype)
            seen = rank[st_ - 1:st_, :]
        return seen

    lax.fori_loop(0, n_tiles, bias_tile, jnp.zeros((1, LANES), F32))

    eye4 = (lax.broadcasted_iota(I32, (LANES, C_GROUP * LANES), 1) % LANES
            == lax.broadcasted_iota(I32, (LANES, C_GROUP * LANES), 0)).astype(F32).astype(MXU_DTYPE)
    for h in range(C_KV_HEADS):
        half = lo_half if h % 2 == 0 else hi_half
        q4t = jnp.concatenate(
            [jnp.transpose(jnp.where(half, q_ref[0, :, LANES * (4 * (h // 2) + g):LANES * (4 * (h // 2) + g + 1)],
                                     0).astype(F32)) for g in range(C_GROUP)], axis=1)
        rhs_ref[h] = jnp.concatenate([q4t.astype(MXU_DTYPE), eye4], axis=0)
    m_ref[...] = jnp.full(m_ref.shape, NEG_BIG, F32)
    acc_ref[...] = jnp.zeros(acc_ref.shape, F32)

    chains = [(h, s) for s in range(tk // st_) for h in range(C_KV_HEADS)]

    def tile_scores(t):
        out = []
        for h, s in chains:
            off = sub_off(t, s)
            kcol = slice(LANES * (h // 2), LANES * (h // 2 + 1))
            lhs = jnp.concatenate([k_ref[0, pl.ds(off, st_), kcol], bias_ref[pl.ds(off, st_), :]], axis=1)
            out.append(_dot(lhs, rhs_ref[h]))
        return out

    def values(t, h, s):
        return vt_ref[0, V_ROWS * h:V_ROWS * (h + 1), pl.ds(sub_off(t, s), st_)]

    def attn_tile(t, carry):
        for (h, s), sc in zip(chains, tile_scores(t)):
            m_old = m_ref[h]
            m_new = jnp.maximum(m_old, jnp.max(sc, axis=0, keepdims=True))
            p = jnp.exp2(sc - m_new).astype(MXU_DTYPE)
            acc_ref[h] = jnp.exp2(m_old - m_new) * acc_ref[h] + _dot(values(t, h, s), p)
            m_ref[h] = m_new
        return carry

    lax.fori_loop(0, n_tiles, attn_tile, 0)

    for h in range(C_KV_HEADS):
        acc = acc_ref[h]
        o = acc[0:HEAD_DIM] / acc[HEAD_DIM:HEAD_DIM + 1]
        for g in range(C_GROUP):
            slot = 8 * (h // 2) + 2 * g + (h % 2)
            ot_ref[HEAD_DIM * slot:HEAD_DIM * (slot + 1), :] = o[:, LANES * g:LANES * (g + 1)]

    attn = jnp.transpose(ot_ref[...]).astype(MXU_DTYPE)
    y = _dot(attn, wout_ref[...])
    o_ref[0] = _ln(ALPHA * x_ref[0] + y, g_ref[...], b_ref[...])


def _dsa(x, q, qi, w_t, k, v_t, kk, wout, g, b, *, causal, n_keys, ksel):
    bsz, t, _ = x.shape
    nk = k.shape[1]
    qb = QUERY_BLOCK
    full = lambda a: pl.BlockSpec(a.shape, lambda bi, qi_: (0,) * a.ndim)
    blk = lambda w_: pl.BlockSpec((1, qb, w_), lambda bi, qi_: (bi, qi_, 0))
    per_b = lambda a: pl.BlockSpec((1,) + a.shape[1:], lambda bi, qi_: (bi, 0, 0))
    kern = functools.partial(_dsa_kernel, causal=causal, n_keys=n_keys, ksel=ksel)
    return pl.pallas_call(
        kern,
        grid=(bsz, t // qb),
        in_specs=[blk(1024), blk(1024), blk(256),
                  pl.BlockSpec((1, 8, qb), lambda bi, qi_: (bi, 0, qi_)),
                  per_b(k), per_b(v_t), per_b(kk), full(wout), full(g), full(b)],
        out_specs=blk(1024),
        out_shape=jax.ShapeDtypeStruct((bsz, t, 1024), F32),
        scratch_shapes=[pltpu.VMEM((nk, LANES), I32),
                        pltpu.VMEM((32, nk // 32, LANES), I32),
                        pltpu.VMEM((nk // 32, LANES), I32),
                        pltpu.VMEM((nk, LANES), MXU_DTYPE),
                        pltpu.VMEM((C_KV_HEADS, 2 * LANES, C_GROUP * LANES), MXU_DTYPE),
                        pltpu.VMEM((C_KV_HEADS, 1, C_GROUP * LANES), F32),
                        pltpu.VMEM((C_KV_HEADS, V_ROWS, C_GROUP * LANES), F32),
                        pltpu.VMEM((1024, LANES), F32)],
        compiler_params=pltpu.CompilerParams(
            dimension_semantics=("arbitrary", "arbitrary"), vmem_limit_bytes=VMEM_LIMIT),
        name="dsa",
    )(x, q, qi, w_t, k, v_t, kk, wout, g, b)


def _slot_perm(head_of_slot):
    return np.concatenate([np.arange(HEAD_DIM * j, HEAD_DIM * (j + 1)) for j in head_of_slot])


_AB_Q_PERM = _slot_perm([B_GROUP * (s % 2) + s // 2 for s in range(8)])
_C_Q_PERM = _slot_perm([4 * (2 * (s // 8) + (s % 8) % 2) + (s % 8) // 2 for s in range(16)])


def _rope_tables(pos):
    half = HEAD_DIM // 2
    inv_freq = jnp.exp(-math.log(ROPE_THETA) * jnp.arange(half, dtype=F32) / half)
    ang = pos.astype(F32)[:, None] * inv_freq[None, :]
    cos, sin = jnp.cos(ang), jnp.sin(ang)
    return jnp.concatenate([cos] * 4, axis=1), jnp.concatenate([-sin, sin, -sin, sin], axis=1)


def _value_operand(v):
    bsz, n, _ = v.shape
    vt = jnp.swapaxes(v, 1, 2).reshape(bsz, C_KV_HEADS, HEAD_DIM, n)
    ones = jnp.ones((bsz, C_KV_HEADS, 1, n), v.dtype)
    zeros = jnp.zeros((bsz, C_KV_HEADS, V_ROWS - HEAD_DIM - 1, n), v.dtype)
    return jnp.concatenate([vt, ones, zeros], axis=2).reshape(bsz, C_KV_HEADS * V_ROWS, n)


def _round_up(n, m):
    return (n + m - 1) // m * m


def _trunk(x, pos, caches, wts):
    (ab_w_in, a_ln_g, a_ln_b, a_ws, a_bs, b_sinks, ab_w_out, c_w_in, c_w_out,
     ln1_g, ln1_b, ln2_g, ln2_b, ff_w1, ff_w2) = wts
    bsz, t, d = x.shape
    sample = caches is not None
    cos, sin = _rope_tables(pos)
    tm = min(512, t)
    row = lambda a: a.reshape(1, -1)

    w_in = jnp.concatenate([ab_w_in[0][:, :1024], ab_w_in[0][:, 1024:1536][:, _AB_Q_PERM],
                            ab_w_in[0][:, 1536:]], axis=1).astype(MXU_DTYPE)
    w_out = jnp.concatenate([ab_w_out[0][:512], ab_w_out[0][512:][_AB_Q_PERM]], axis=0).astype(MXU_DTYPE)
    u, va, q, k, v = _ab_in(x.reshape(bsz * t, d), w_in, cos, sin, row(a_ln_g[0]), row(a_ln_b[0]), t=t, tm=tm)
    r3 = lambda a: a.reshape(bsz, t, a.shape[-1])
    u, va, q, k, v = r3(u), r3(va), r3(q), r3(k), r3(v)
    cs = min(A_CHUNK, t)
    ws = a_ws[0][:, :cs, :cs]
    bias = jnp.repeat(a_bs[0][:, :cs].T, HEAD_DIM, axis=1)
    if sample:
        hist_k = caches[0][0].reshape(bsz, B_WINDOW, 128)
        hist_v = caches[1][0].reshape(bsz, B_WINDOW, 128)
        rows = t
    else:
        hist_k, hist_v = k, v
        rows = min(256, t)
    x = _ab_mix(b_sinks[0], x, u, va, q, k, v, hist_k, hist_v, ws, bias, w_out,
                row(ln1_g[0]), row(ln1_b[0]), rows=rows, cs=cs, hist_from_self=not sample)
    x = _mlp(x.reshape(bsz * t, d), ff_w1[0].astype(MXU_DTYPE), ff_w2[0].astype(MXU_DTYPE),
             row(ln2_g[0]), row(ln2_b[0]), tm=tm).reshape(bsz, t, d)
    b_k = k.reshape(bsz, t, B_KV_HEADS, HEAD_DIM)
    b_v = v.reshape(bsz, t, B_KV_HEADS, HEAD_DIM)

    cw = c_w_in[0]
    w_in = jnp.concatenate(
        [cw[:, :1024][:, _C_Q_PERM], cw[:, 1024:1792], cw[:, 1792:1856], cw[:, 1792:1856], cw[:, 1856:1860],
         jnp.zeros((d, LANES - IDX_HEADS), F32)], axis=1).astype(MXU_DTYPE)
    w_out = c_w_out[0][_C_Q_PERM].astype(MXU_DTYPE)
    q, k, kb, v, qi, kk, kkb, wi = _c_in(x.reshape(bsz * t, d), w_in, cos, sin, t=t, tm=tm)
    q, k, kb, v, qi, kk, kkb, wi = (r3(a) for a in (q, k, kb, v, qi, kk, kkb, wi))
    ki = kk[:, :, :IDX_DIM]
    w_t = jnp.swapaxes(wi[:, :, :8], 1, 2) * (IDX_DIM ** -0.5 * IDX_HEADS ** -0.5)
    if sample:
        keys_k = jnp.concatenate([caches[2][0].reshape(bsz, -1, 256).astype(MXU_DTYPE), kb], axis=1)
        keys_v = jnp.concatenate([caches[3][0].reshape(bsz, -1, 256), v], axis=1).astype(MXU_DTYPE)
        ci = caches[4][0].astype(MXU_DTYPE)
        keys_i = jnp.concatenate([jnp.concatenate([ci, ci], axis=-1), kkb], axis=1)
        n_keys = keys_k.shape[1]
        pad_k = _round_up(n_keys, KEY_TILE) - n_keys
        keys_k, keys_v, keys_i = (jnp.pad(a, ((0, 0), (0, pad_k), (0, 0))) for a in (keys_k, keys_v, keys_i))
        pad_q = QUERY_BLOCK - t
        padq = lambda a: jnp.pad(a, ((0, 0), (0, pad_q), (0, 0)))
        xo = _dsa(padq(x), padq(q), padq(qi), jnp.pad(w_t, ((0, 0), (0, 0), (0, pad_q))), keys_k,
                  _value_operand(keys_v), keys_i, w_out, row(ln1_g[1]), row(ln1_b[1]),
                  causal=False, n_keys=n_keys, ksel=min(TOPK_MAX, n_keys // 4))[:, :t]
    else:
        xo = _dsa(x, q, qi, w_t, kb, _value_operand(v.astype(MXU_DTYPE)), kkb, w_out,
                  row(ln1_g[1]), row(ln1_b[1]), causal=True, n_keys=t, ksel=min(TOPK_MAX, t // 4))
    x = _mlp(xo.reshape(bsz * t, d), ff_w1[1].astype(MXU_DTYPE), ff_w2[1].astype(MXU_DTYPE),
             row(ln2_g[1]), row(ln2_b[1]), tm=tm).reshape(bsz, t, d)
    c_k = k.reshape(bsz, t, C_KV_HEADS, HEAD_DIM)
    c_v = v.reshape(bsz, t, C_KV_HEADS, HEAD_DIM)
    return x, va[None], b_k[None], b_v[None], c_k[None], c_v[None], ki[None]


def kernel(x_prompt, x_sample, cache_b_k, cache_b_v, cache_c_k, cache_c_v, cache_c_idx, ab_w_in, a_ln_g, a_ln_b, a_ws, a_bs, b_sinks, ab_w_out, c_w_in, c_w_out, ln1_g, ln1_b, ln2_g, ln2_b, ff_w1, ff_w2):
    wts = (ab_w_in, a_ln_g, a_ln_b, a_ws, a_bs, b_sinks, ab_w_out, c_w_in, c_w_out,
           ln1_g, ln1_b, ln2_g, ln2_b, ff_w1, ff_w2)
    past_len = cache_c_k.shape[2]
    pos_p = jnp.arange(x_prompt.shape[1], dtype=jnp.int32)
    pos_s = past_len + jnp.arange(x_sample.shape[1], dtype=jnp.int32)
    y_p, _, p_b_k, p_b_v, p_c_k, p_c_v, p_c_idx = _trunk(x_prompt, pos_p, None, wts)
    y_s, s_a_v, s_b_k, s_b_v, s_c_k, s_c_v, s_c_idx = _trunk(
        x_sample, pos_s, (cache_b_k, cache_b_v, cache_c_k, cache_c_v, cache_c_idx), wts)
    return (y_p, y_s, p_b_k[:, :, -B_WINDOW:], p_b_v[:, :, -B_WINDOW:], p_c_k, p_c_v, p_c_idx,
            s_a_v, s_b_k, s_b_v, s_c_k, s_c_v, s_c_idx)
```

```python
import functools
import math

import numpy as np
import jax
import jax.numpy as jnp
from jax import lax
from jax.experimental import pallas as pl
from jax.experimental.pallas import tpu as pltpu

F32 = jnp.float32
I32 = jnp.int32
MXU_DTYPE = jnp.bfloat16

HEAD_DIM = 64
CHUNK = 64
ROPE_THETA = 10000.0
LN_EPS = 1e-5
DEPTH = 2
ALPHA = (2.0 * DEPTH) ** 0.25
A_GROUPS = 8
A_CHUNK = 128
B_KV_HEADS = 2
B_GROUP = 4
B_WINDOW = 128
C_KV_HEADS = 4
C_GROUP = 4
IDX_HEADS = 4
IDX_DIM = 64
TOPK_MAX = 256

LANES = 128
KEY_TILE = 1024
SUB_TILE = 256
QUERY_BLOCK = 128
INT_MIN = -(2 ** 31)
NEG_BIG = -(2.0 ** 100)
LOG2E = math.log2(math.e)
V_ROWS = 80
VMEM_LIMIT = 56 * 1024 * 1024


def _gelu(x):
    c = math.sqrt(2.0 / math.pi)
    return 0.5 * x * (1.0 + jnp.tanh(c * (x + 0.044715 * (x * x * x))))


def _ln(z, g, b):
    mu = jnp.mean(z, axis=-1, keepdims=True)
    d = z - mu
    var = jnp.mean(d * d, axis=-1, keepdims=True)
    return d * lax.rsqrt(var + LN_EPS) * g + b


def _rope2(x, cos, sin):
    lane = lax.broadcasted_iota(I32, (1, LANES), 1)
    first = (lane % HEAD_DIM) < (HEAD_DIM // 2)
    swapped = jnp.where(first, pltpu.roll(x, LANES - HEAD_DIM // 2, 1), pltpu.roll(x, HEAD_DIM // 2, 1))
    return x * cos + swapped * sin


def _dot(a, b):
    return jnp.dot(a, b, preferred_element_type=F32)


def _dot_nt(a, b):
    return lax.dot_general(a, b, (((1,), (1,)), ((), ())), preferred_element_type=F32)


def _ab_in_kernel(x_ref, w_ref, cos_ref, sin_ref, g_ref, b_ref, u_ref, va_ref, q_ref, k_ref, v_ref):
    y = _dot(x_ref[...].astype(MXU_DTYPE), w_ref[...])
    u_ref[...] = _gelu(y[:, 0:512])
    va_ref[...] = _ln(_gelu(y[:, 512:1024]), g_ref[...], b_ref[...])
    cos = cos_ref[...]
    sin = sin_ref[...]
    for c in range(4):
        lo = 1024 + LANES * c
        q_ref[:, LANES * c:LANES * (c + 1)] = (
            _rope2(y[:, lo:lo + LANES], cos, sin) * (HEAD_DIM ** -0.5)).astype(q_ref.dtype)
    k_ref[...] = _rope2(y[:, 1536:1664], cos, sin)
    v_ref[...] = y[:, 1664:1792]


def _ab_in(x2d, w, cos, sin, g, b, *, t, tm):
    rows = x2d.shape[0]
    nt = t // tm
    row_spec = lambda w_: pl.BlockSpec((tm, w_), lambda i: (i, 0))
    full = lambda a: pl.BlockSpec(a.shape, lambda i: (0,) * a.ndim)
    tab = pl.BlockSpec((tm, LANES), lambda i: (i % nt, 0))
    return pl.pallas_call(
        _ab_in_kernel,
        grid=(rows // tm,),
        in_specs=[row_spec(1024), full(w), tab, tab, full(g), full(b)],
        out_specs=[row_spec(512), row_spec(512), row_spec(512), row_spec(128), row_spec(128)],
        out_shape=[
            jax.ShapeDtypeStruct((rows, 512), F32),
            jax.ShapeDtypeStruct((rows, 512), F32),
            jax.ShapeDtypeStruct((rows, 512), MXU_DTYPE),
            jax.ShapeDtypeStruct((rows, 128), F32),
            jax.ShapeDtypeStruct((rows, 128), F32),
        ],
        compiler_params=pltpu.CompilerParams(
            dimension_semantics=("arbitrary",), vmem_limit_bytes=VMEM_LIMIT),
        name="ab_in",
    )(x2d, w, cos, sin, g, b)


def _ab_mix_kernel(sink_ref, x_ref, u_ref, va_ref, q_ref, k_ref, v_ref, hk_ref, hv_ref,
                   ws_ref, bias_ref, wout_ref, g_ref, b_ref, o_ref, cat_ref, *, rows, cs, mask_first):
    t = pl.program_id(1)
    lane = lax.broadcasted_iota(I32, (1, LANES), 1)
    lo_half = lane < HEAD_DIM

    r_i = lax.broadcasted_iota(I32, (cs, cs), 0)
    c_i = lax.broadcasted_iota(I32, (cs, cs), 1)
    tril = r_i >= c_i
    w_tril = [jnp.where(tril, ws_ref[g], 0.0).astype(MXU_DTYPE) for g in range(A_GROUPS)]
    for c in range(rows // cs):
        rs = slice(c * cs, (c + 1) * cs)
        for p in range(A_GROUPS // 2):
            ls = slice(LANES * p, LANES * (p + 1))
            vp = va_ref[0, rs, ls].astype(MXU_DTYPE)
            gate = jnp.where(lo_half, _dot(w_tril[2 * p], vp), _dot(w_tril[2 * p + 1], vp)) + bias_ref[:, ls]
            cat_ref[rs, ls] = (u_ref[0, rs, ls] * gate).astype(cat_ref.dtype)

    kcat = jnp.concatenate([hk_ref[0], k_ref[0]], axis=0).astype(MXU_DTYPE)
    vcat = jnp.concatenate([hv_ref[0], v_ref[0]], axis=0).astype(MXU_DTYPE)
    nwin = B_WINDOW + CHUNK
    head_of_row = lax.broadcasted_iota(I32, (B_GROUP * CHUNK, 1), 0) // CHUNK
    col = lax.broadcasted_iota(I32, (1, nwin), 1)
    for j in range(rows // CHUNK):
        rs = slice(CHUNK * j, CHUNK * (j + 1))
        kwin = kcat[CHUNK * j:CHUNK * j + nwin]
        vwin = vcat[CHUNK * j:CHUNK * j + nwin]
        outs = []
        for h in range(B_KV_HEADS):
            half = lo_half if h == 0 else jnp.logical_not(lo_half)
            q4 = jnp.concatenate(
                [jnp.where(half, q_ref[0, rs, LANES * g:LANES * (g + 1)], 0).astype(MXU_DTYPE)
                 for g in range(B_GROUP)], axis=0)
            s = _dot_nt(q4, kwin)
            if mask_first:
                s = jnp.where(t * rows + CHUNK * j - B_WINDOW + col >= 0, s, -jnp.inf)
            sink = jnp.zeros((B_GROUP * CHUNK, 1), F32)
            for g in range(B_GROUP):
                sink = jnp.where(head_of_row == g, sink_ref[B_GROUP * h + g], sink)
            m = jnp.maximum(jnp.max(s, axis=-1, keepdims=True), sink)
            e = jnp.exp(s - m)
            p = e / (jnp.sum(e, axis=-1, keepdims=True) + jnp.exp(sink - m))
            outs.append(_dot(p.astype(MXU_DTYPE), vwin))
        for g in range(B_GROUP):
            gs = slice(CHUNK * g, CHUNK * (g + 1))
            cat_ref[rs, 512 + LANES * g:512 + LANES * (g + 1)] = jnp.where(
                lo_half, outs[0][gs], outs[1][gs]).astype(cat_ref.dtype)

    y = _dot(cat_ref[...], wout_ref[...])
    o_ref[0] = _ln(ALPHA * x_ref[0] + y, g_ref[...], b_ref[...])


def _ab_mix(sinks, x, u, va, q, k, v, hist_k, hist_v, ws, bias, wout, g, b, *, rows, cs, hist_from_self):
    bsz, t, _ = x.shape
    full = lambda a: pl.BlockSpec(a.shape, lambda bi, ti: (0,) * a.ndim)
    blk = lambda w_: pl.BlockSpec((1, rows, w_), lambda bi, ti: (bi, ti, 0))
    if hist_from_self:
        per = rows // B_WINDOW
        hist = pl.BlockSpec((1, B_WINDOW, 128), lambda bi, ti: (bi, jnp.maximum(ti * per - 1, 0), 0))
    else:
        hist = pl.BlockSpec((1, B_WINDOW, 128), lambda bi, ti: (bi, 0, 0))
    kern = functools.partial(_ab_mix_kernel, rows=rows, cs=cs, mask_first=hist_from_self)
    return pl.pallas_call(
        kern,
        grid=(bsz, t // rows),
        in_specs=[pl.BlockSpec(memory_space=pltpu.SMEM),
                  blk(1024), blk(512), blk(512), blk(512), blk(128), blk(128), hist, hist,
                  full(ws), full(bias), full(wout), full(g), full(b)],
        out_specs=blk(1024),
        out_shape=jax.ShapeDtypeStruct((bsz, t, 1024), F32),
        scratch_shapes=[pltpu.VMEM((rows, 1024), MXU_DTYPE)],
        compiler_params=pltpu.CompilerParams(
            dimension_semantics=("arbitrary", "arbitrary"), vmem_limit_bytes=VMEM_LIMIT),
        name="ab_mix",
    )(sinks, x, u, va, q, k, v, hist_k, hist_v, ws, bias, wout, g, b)


def _mlp_kernel(x_ref, w1_ref, w2_ref, g_ref, b_ref, o_ref, *, ff_tile):
    x = x_ref[...]
    xb = x.astype(MXU_DTYPE)
    acc = jnp.zeros(x.shape, F32)
    for c in range(w1_ref.shape[1] // ff_tile):
        h = _dot(xb, w1_ref[:, c * ff_tile:(c + 1) * ff_tile])
        h = jnp.square(jnp.maximum(h, 0.0)).astype(MXU_DTYPE)
        acc = acc + _dot(h, w2_ref[c * ff_tile:(c + 1) * ff_tile, :])
    o_ref[...] = _ln(ALPHA * x + acc, g_ref[...], b_ref[...])


def _mlp(x2d, w1, w2, g, b, *, tm):
    rows, d = x2d.shape
    full = lambda a: pl.BlockSpec(a.shape, lambda i: (0,) * a.ndim)
    row = pl.BlockSpec((tm, d), lambda i: (i, 0))
    return pl.pallas_call(
        functools.partial(_mlp_kernel, ff_tile=1024),
        grid=(rows // tm,),
        in_specs=[row, full(w1), full(w2), full(g), full(b)],
        out_specs=row,
        out_shape=jax.ShapeDtypeStruct((rows, d), F32),
        compiler_params=pltpu.CompilerParams(
            dimension_semantics=("arbitrary",), vmem_limit_bytes=VMEM_LIMIT),
        name="mlp",
    )(x2d, w1, w2, g, b)


def _c_in_kernel(x_ref, w_ref, cos_ref, sin_ref, q_ref, k_ref, kb_ref, v_ref, qi_ref, kk_ref, kkb_ref, wi_ref):
    y = _dot(x_ref[...].astype(MXU_DTYPE), w_ref[...])
    cos = cos_ref[...]
    sin = sin_ref[...]
    for c in range(8):
        q_ref[:, LANES * c:LANES * (c + 1)] = (
            _rope2(y[:, LANES * c:LANES * (c + 1)], cos, sin) * (LOG2E * HEAD_DIM ** -0.5)).astype(q_ref.dtype)
    for c in range(2):
        kr = _rope2(y[:, 1024 + LANES * c:1024 + LANES * (c + 1)], cos, sin)
        k_ref[:, LANES * c:LANES * (c + 1)] = kr
        kb_ref[:, LANES * c:LANES * (c + 1)] = kr.astype(kb_ref.dtype)
    v_ref[...] = y[:, 1280:1536]
    for c in range(2):
        qi_ref[:, LANES * c:LANES * (c + 1)] = _rope2(
            y[:, 1536 + LANES * c:1536 + LANES * (c + 1)], cos, sin).astype(qi_ref.dtype)
    kk = _rope2(y[:, 1792:1920], cos, sin)
    kk_ref[...] = kk
    kkb_ref[...] = kk.astype(kkb_ref.dtype)
    wi_ref[...] = y[:, 1920:2048]


def _c_in(x2d, w, cos, sin, *, t, tm):
    rows = x2d.shape[0]
    nt = t // tm
    row_spec = lambda w_: pl.BlockSpec((tm, w_), lambda i: (i, 0))
    full = lambda a: pl.BlockSpec(a.shape, lambda i: (0,) * a.ndim)
    tab = pl.BlockSpec((tm, LANES), lambda i: (i % nt, 0))
    widths = [(1024, MXU_DTYPE), (256, F32), (256, MXU_DTYPE), (256, F32), (256, MXU_DTYPE),
              (128, F32), (128, MXU_DTYPE), (128, F32)]
    return pl.pallas_call(
        _c_in_kernel,
        grid=(rows // tm,),
        in_specs=[row_spec(1024), full(w), tab, tab],
        out_specs=[row_spec(w_) for w_, _ in widths],
        out_shape=[jax.ShapeDtypeStruct((rows, w_), dt) for w_, dt in widths],
        compiler_params=pltpu.CompilerParams(
            dimension_semantics=("arbitrary",), vmem_limit_bytes=VMEM_LIMIT),
        name="c_in",
    )(x2d, w, cos, sin)


def _bit_transpose32(words):
    a = list(words)
    j, m = 16, 0x0000FFFF
    while j:
        k = 0
        while k < 32:
            t = (a[k] ^ lax.shift_right_logical(a[k + j], jnp.int32(j))) & m
            a[k] = a[k] ^ t
            a[k + j] = a[k + j] ^ lax.shift_left(t, jnp.int32(j))
            k = (k + j + 1) & ~j
        j >>= 1
        m = (m ^ (m << j)) & 0xFFFFFFFF
    return a


def _dsa_kernel(x_ref, q_ref, qi_ref, w_ref, k_ref, vt_ref, kk_ref, wout_ref, g_ref, b_ref, o_ref,
                key_ref, planes_ref, alive_ref, rhs_ref, m_ref, acc_ref, ot_ref,
                *, causal, n_keys, ksel):
    tk, st_ = KEY_TILE, SUB_TILE
    i = pl.program_id(1)
    lane = lax.broadcasted_iota(I32, (1, LANES), 1)
    lo_half = lane < HEAD_DIM
    hi_half = jnp.logical_not(lo_half)
    if causal:
        limit = CHUNK * (2 * i + jnp.where(lo_half, jnp.int32(0), jnp.int32(1)) + 1)
        n_hi = QUERY_BLOCK * (i + 1)
    else:
        limit = jnp.full((1, LANES), n_keys, I32)
        n_hi = n_keys
    sub = lax.broadcasted_iota(I32, (st_, LANES), 0)
    per_tile = tk // st_
    n_sub_tiles = (n_hi + st_ - 1) // st_
    n_full = n_sub_tiles // per_tile
    n_rem = n_sub_tiles - n_full * per_tile

    def over_keys(body, carry):
        carry = lax.fori_loop(0, n_full, lambda t, c: body(t * per_tile, per_tile, c), carry)
        return lax.fori_loop(0, n_rem, lambda r, c: body(n_full * per_tile + r, 1, c), carry)

    def key_off(j):
        return pl.multiple_of(j * st_, st_)

    def plane_row(j):
        return pl.multiple_of(j * (st_ // 32), 8)

    qi_blk = qi_ref[0]
    qi4 = jnp.concatenate(
        [jnp.where(lo_half if h % 2 == 0 else hi_half, qi_blk[:, LANES * (h // 2):LANES * (h // 2 + 1)], 0)
         for h in range(IDX_HEADS)], axis=0)
    w = w_ref[0]

    def score_tile(j0, n_sub, carry):
        for s in range(n_sub):
            off = key_off(j0 + s)
            sc4 = _dot_nt(kk_ref[0, pl.ds(off, st_), :], qi4)
            sc = jnp.maximum(sc4[:, 0:LANES], 0.0) * w[0:1]
            for h in range(1, IDX_HEADS):
                sc = sc + jnp.maximum(sc4[:, LANES * h:LANES * (h + 1)], 0.0) * w[h:h + 1]
            bits = pltpu.bitcast(sc, I32)
            key = jnp.where(bits < 0, bits ^ 0x7FFFFFFF, bits)
            key = jnp.where(sub + off < limit, key, INT_MIN)
            key_ref[pl.ds(off, st_), :] = key
            ukey = key ^ INT_MIN
            planes = _bit_transpose32([ukey[8 * r:8 * (r + 1)] for r in range(32)])
            row = plane_row(j0 + s)
            for b in range(32):
                planes_ref[b, pl.ds(row, 8), :] = planes[b]
        return carry

    over_keys(score_tile, 0)

    def plane_count(b, flip, first):
        def body(j0, n_sub, acc):
            for s in range(n_sub):
                row = plane_row(j0 + s)
                if first:
                    alive = jnp.full((8, LANES), -1, I32)
                else:
                    alive = alive_ref[pl.ds(row, 8), :] & (planes_ref[b - 1, pl.ds(row, 8), :] ^ flip)
                alive_ref[pl.ds(row, 8), :] = alive
                acc = acc + lax.population_count(alive & planes_ref[b, pl.ds(row, 8), :])
            return acc
        return jnp.sum(over_keys(body, jnp.zeros((8, LANES), I32)), axis=0, keepdims=True)

    def decide(b, cnt, above, ubits):
        take = above + cnt >= ksel
        ubits = ubits | jnp.where(take, lax.shift_left(jnp.int32(1), jnp.int32(31) - b), jnp.int32(0))
        above = jnp.where(take, above, above + cnt)
        return above, ubits, jnp.where(take, jnp.int32(0), jnp.int32(-1))

    zero_row = jnp.zeros((1, LANES), I32)
    state = decide(0, plane_count(0, None, True), zero_row, zero_row)

    def bit_step(b, state):
        above, ubits, flip = state
        return decide(b, plane_count(b, flip, False), above, ubits)

    above, ubits, _ = lax.fori_loop(1, 32, bit_step, state)
    thr = ubits ^ INT_MIN

    need = ksel - above
    need = jnp.where(thr == INT_MIN, jnp.int32(0), need).astype(F32)
    tril = (lax.broadcasted_iota(I32, (st_, st_), 0) >= lax.broadcasted_iota(I32, (st_, st_), 1)
            ).astype(F32).astype(MXU_DTYPE)
    keep = jnp.zeros((st_, LANES), F32)
    drop = jnp.full((st_, LANES), NEG_BIG, F32)
    f_one = jnp.ones((st_, LANES), F32)

    def tile_bias(j0, n_sub, seen):
        blks = [key_ref[pl.ds(key_off(j0 + s), st_), :] for s in range(n_sub)]
        ranks = [_dot(tril, jnp.where(blk == thr, f_one, keep).astype(MXU_DTYPE)) for blk in blks]
        out = []
        for blk, rank in zip(blks, ranks):
            rank = rank + seen
            tie_bias = jnp.where(blk == thr, jnp.where(rank <= need, keep, drop), drop)
            out.append(jnp.where(blk > thr, keep, tie_bias).astype(MXU_DTYPE))
            seen = rank[st_ - 1:st_, :]
        return out, seen

    eye4 = (lax.broadcasted_iota(I32, (LANES, C_GROUP * LANES), 1) % LANES
            == lax.broadcasted_iota(I32, (LANES, C_GROUP * LANES), 0)).astype(F32).astype(MXU_DTYPE)
    for h in range(C_KV_HEADS):
        half = lo_half if h % 2 == 0 else hi_half
        q4t = jnp.concatenate(
            [jnp.transpose(jnp.where(half, q_ref[0, :, LANES * (4 * (h // 2) + g):LANES * (4 * (h // 2) + g + 1)],
                                     0).astype(F32)) for g in range(C_GROUP)], axis=1)
        rhs_ref[h] = jnp.concatenate([q4t.astype(MXU_DTYPE), eye4], axis=0)
    m_ref[...] = jnp.full(m_ref.shape, NEG_BIG, F32)
    acc_ref[...] = jnp.zeros(acc_ref.shape, F32)

    def attn_tile(j0, n_sub, seen):
        chains = [(h, s) for s in range(n_sub) for h in range(C_KV_HEADS)]
        bias, seen = tile_bias(j0, n_sub, seen)
        scores = []
        for h, s in chains:
            kcol = slice(LANES * (h // 2), LANES * (h // 2 + 1))
            lhs = jnp.concatenate([k_ref[0, pl.ds(key_off(j0 + s), st_), kcol], bias[s]], axis=1)
            scores.append(_dot(lhs, rhs_ref[h]))
        for (h, s), sc in zip(chains, scores):
            m_old = m_ref[h]
            m_new = jnp.maximum(m_old, jnp.max(sc, axis=0, keepdims=True))
            p = jnp.exp2(sc - m_new).astype(MXU_DTYPE)
            vt = vt_ref[0, V_ROWS * h:V_ROWS * (h + 1), pl.ds(key_off(j0 + s), st_)]
            acc_ref[h] = jnp.exp2(m_old - m_new) * acc_ref[h] + _dot(vt, p)
            m_ref[h] = m_new
        return seen

    over_keys(attn_tile, jnp.zeros((1, LANES), F32))

    for h in range(C_KV_HEADS):
        acc = acc_ref[h]
        o = acc[0:HEAD_DIM] / acc[HEAD_DIM:HEAD_DIM + 1]
        for g in range(C_GROUP):
            slot = 8 * (h // 2) + 2 * g + (h % 2)
            ot_ref[HEAD_DIM * slot:HEAD_DIM * (slot + 1), :] = o[:, LANES * g:LANES * (g + 1)]

    attn = jnp.transpose(ot_ref[...]).astype(MXU_DTYPE)
    y = _dot(attn, wout_ref[...])
    o_ref[0] = _ln(ALPHA * x_ref[0] + y, g_ref[...], b_ref[...])


def _dsa(x, q, qi, w_t, k, v_t, kk, wout, g, b, *, causal, n_keys, ksel):
    bsz, t, _ = x.shape
    nk = k.shape[1]
    qb = QUERY_BLOCK
    full = lambda a: pl.BlockSpec(a.shape, lambda bi, qi_: (0,) * a.ndim)
    blk = lambda w_: pl.BlockSpec((1, qb, w_), lambda bi, qi_: (bi, qi_, 0))
    per_b = lambda a: pl.BlockSpec((1,) + a.shape[1:], lambda bi, qi_: (bi, 0, 0))
    kern = functools.partial(_dsa_kernel, causal=causal, n_keys=n_keys, ksel=ksel)
    return pl.pallas_call(
        kern,
        grid=(bsz, t // qb),
        in_specs=[blk(1024), blk(1024), blk(256),
                  pl.BlockSpec((1, 8, qb), lambda bi, qi_: (bi, 0, qi_)),
                  per_b(k), per_b(v_t), per_b(kk), full(wout), full(g), full(b)],
        out_specs=blk(1024),
        out_shape=jax.ShapeDtypeStruct((bsz, t, 1024), F32),
        scratch_shapes=[pltpu.VMEM((nk, LANES), I32),
                        pltpu.VMEM((32, nk // 32, LANES), I32),
                        pltpu.VMEM((nk // 32, LANES), I32),
                        pltpu.VMEM((C_KV_HEADS, 2 * LANES, C_GROUP * LANES), MXU_DTYPE),
                        pltpu.VMEM((C_KV_HEADS, 1, C_GROUP * LANES), F32),
                        pltpu.VMEM((C_KV_HEADS, V_ROWS, C_GROUP * LANES), F32),
                        pltpu.VMEM((1024, LANES), F32)],
        compiler_params=pltpu.CompilerParams(
            dimension_semantics=("arbitrary", "arbitrary"), vmem_limit_bytes=VMEM_LIMIT),
        name="dsa",
    )(x, q, qi, w_t, k, v_t, kk, wout, g, b)


def _slot_perm(head_of_slot):
    return np.concatenate([np.arange(HEAD_DIM * j, HEAD_DIM * (j + 1)) for j in head_of_slot])


_AB_Q_PERM = _slot_perm([B_GROUP * (s % 2) + s // 2 for s in range(8)])
_C_Q_PERM = _slot_perm([4 * (2 * (s // 8) + (s % 8) % 2) + (s % 8) // 2 for s in range(16)])


def _rope_tables(pos):
    half = HEAD_DIM // 2
    inv_freq = jnp.exp(-math.log(ROPE_THETA) * jnp.arange(half, dtype=F32) / half)
    ang = pos.astype(F32)[:, None] * inv_freq[None, :]
    cos, sin = jnp.cos(ang), jnp.sin(ang)
    return jnp.concatenate([cos] * 4, axis=1), jnp.concatenate([-sin, sin, -sin, sin], axis=1)


def _value_operand(v):
    bsz, n, _ = v.shape
    vt = jnp.swapaxes(v, 1, 2).reshape(bsz, C_KV_HEADS, HEAD_DIM, n)
    ones = jnp.ones((bsz, C_KV_HEADS, 1, n), v.dtype)
    zeros = jnp.zeros((bsz, C_KV_HEADS, V_ROWS - HEAD_DIM - 1, n), v.dtype)
    return jnp.concatenate([vt, ones, zeros], axis=2).reshape(bsz, C_KV_HEADS * V_ROWS, n)


def _round_up(n, m):
    return (n + m - 1) // m * m


def _trunk(x, pos, caches, wts):
    (ab_w_in, a_ln_g, a_ln_b, a_ws, a_bs, b_sinks, ab_w_out, c_w_in, c_w_out,
     ln1_g, ln1_b, ln2_g, ln2_b, ff_w1, ff_w2) = wts
    bsz, t, d = x.shape
    sample = caches is not None
    cos, sin = _rope_tables(pos)
    tm = min(512, t)
    row = lambda a: a.reshape(1, -1)

    w_in = jnp.concatenate([ab_w_in[0][:, :1024], ab_w_in[0][:, 1024:1536][:, _AB_Q_PERM],
                            ab_w_in[0][:, 1536:]], axis=1).astype(MXU_DTYPE)
    w_out = jnp.concatenate([ab_w_out[0][:512], ab_w_out[0][512:][_AB_Q_PERM]], axis=0).astype(MXU_DTYPE)
    u, va, q, k, v = _ab_in(x.reshape(bsz * t, d), w_in, cos, sin, row(a_ln_g[0]), row(a_ln_b[0]), t=t, tm=tm)
    r3 = lambda a: a.reshape(bsz, t, a.shape[-1])
    u, va, q, k, v = r3(u), r3(va), r3(q), r3(k), r3(v)
    cs = min(A_CHUNK, t)
    ws = a_ws[0][:, :cs, :cs]
    bias = jnp.repeat(a_bs[0][:, :cs].T, HEAD_DIM, axis=1)
    if sample:
        hist_k = caches[0][0].reshape(bsz, B_WINDOW, 128)
        hist_v = caches[1][0].reshape(bsz, B_WINDOW, 128)
        rows = t
    else:
        hist_k, hist_v = k, v
        rows = min(256, t)
    x = _ab_mix(b_sinks[0], x, u, va, q, k, v, hist_k, hist_v, ws, bias, w_out,
                row(ln1_g[0]), row(ln1_b[0]), rows=rows, cs=cs, hist_from_self=not sample)
    x = _mlp(x.reshape(bsz * t, d), ff_w1[0].astype(MXU_DTYPE), ff_w2[0].astype(MXU_DTYPE),
             row(ln2_g[0]), row(ln2_b[0]), tm=tm).reshape(bsz, t, d)
    b_k = k.reshape(bsz, t, B_KV_HEADS, HEAD_DIM)
    b_v = v.reshape(bsz, t, B_KV_HEADS, HEAD_DIM)

    cw = c_w_in[0]
    w_in = jnp.concatenate(
        [cw[:, :1024][:, _C_Q_PERM], cw[:, 1024:1792], cw[:, 1792:1856], cw[:, 1792:1856], cw[:, 1856:1860],
         jnp.zeros((d, LANES - IDX_HEADS), F32)], axis=1).astype(MXU_DTYPE)
    w_out = c_w_out[0][_C_Q_PERM].astype(MXU_DTYPE)
    q, k, kb, v, qi, kk, kkb, wi = _c_in(x.reshape(bsz * t, d), w_in, cos, sin, t=t, tm=tm)
    q, k, kb, v, qi, kk, kkb, wi = (r3(a) for a in (q, k, kb, v, qi, kk, kkb, wi))
    ki = kk[:, :, :IDX_DIM]
    w_t = jnp.swapaxes(wi[:, :, :8], 1, 2) * (IDX_DIM ** -0.5 * IDX_HEADS ** -0.5)
    if sample:
        keys_k = jnp.concatenate([caches[2][0].reshape(bsz, -1, 256).astype(MXU_DTYPE), kb], axis=1)
        keys_v = jnp.concatenate([caches[3][0].reshape(bsz, -1, 256), v], axis=1).astype(MXU_DTYPE)
        ci = caches[4][0].astype(MXU_DTYPE)
        keys_i = jnp.concatenate([jnp.concatenate([ci, ci], axis=-1), kkb], axis=1)
        n_keys = keys_k.shape[1]
        pad_k = _round_up(n_keys, SUB_TILE) - n_keys
        keys_k, keys_v, keys_i = (jnp.pad(a, ((0, 0), (0, pad_k), (0, 0))) for a in (keys_k, keys_v, keys_i))
        pad_q = QUERY_BLOCK - t
        padq = lambda a: jnp.pad(a, ((0, 0), (0, pad_q), (0, 0)))
        xo = _dsa(padq(x), padq(q), padq(qi), jnp.pad(w_t, ((0, 0), (0, 0), (0, pad_q))), keys_k,
                  _value_operand(keys_v), keys_i, w_out, row(ln1_g[1]), row(ln1_b[1]),
                  causal=False, n_keys=n_keys, ksel=min(TOPK_MAX, n_keys // 4))[:, :t]
    else:
        xo = _dsa(x, q, qi, w_t, kb, _value_operand(v.astype(MXU_DTYPE)), kkb, w_out,
                  row(ln1_g[1]), row(ln1_b[1]), causal=True, n_keys=t, ksel=min(TOPK_MAX, t // 4))
    x = _mlp(xo.reshape(bsz * t, d), ff_w1[1].astype(MXU_DTYPE), ff_w2[1].astype(MXU_DTYPE),
             row(ln2_g[1]), row(ln2_b[1]), tm=tm).reshape(bsz, t, d)
    c_k = k.reshape(bsz, t, C_KV_HEADS, HEAD_DIM)
    c_v = v.reshape(bsz, t, C_KV_HEADS, HEAD_DIM)
    return x, va[None], b_k[None], b_v[None], c_k[None], c_v[None], ki[None]


def kernel(x_prompt, x_sample, cache_b_k, cache_b_v, cache_c_k, cache_c_v, cache_c_idx, ab_w_in, a_ln_g, a_ln_b, a_ws, a_bs, b_sinks, ab_w_out, c_w_in, c_w_out, ln1_g, ln1_b, ln2_g, ln2_b, ff_w1, ff_w2):
    wts = (ab_w_in, a_ln_g, a_ln_b, a_ws, a_bs, b_sinks, ab_w_out, c_w_in, c_w_out,
           ln1_g, ln1_b, ln2_g, ln2_b, ff_w1, ff_w2)
    past_len = cache_c_k.shape[2]
    pos_p = jnp.arange(x_prompt.shape[1], dtype=jnp.int32)
    pos_s = past_len + jnp.arange(x_sample.shape[1], dtype=jnp.int32)
    y_p, _, p_b_k, p_b_v, p_c_k, p_c_v, p_c_idx = _trunk(x_prompt, pos_p, None, wts)
    y_s, s_a_v, s_b_k, s_b_v, s_c_k, s_c_v, s_c_idx = _trunk(
        x_sample, pos_s, (cache_b_k, cache_b_v, cache_c_k, cache_c_v, cache_c_idx), wts)
    return (y_p, y_s, p_b_k[:, :, -B_WINDOW:], p_b_v[:, :, -B_WINDOW:], p_c_k, p_c_v, p_c_idx,
            s_a_v, s_b_k, s_b_v, s_c_k, s_c_v, s_c_idx)
```

```python
import functools
import math

import numpy as np
import jax
import jax.numpy as jnp
from jax import lax
from jax.experimental import pallas as pl
from jax.experimental.pallas import tpu as pltpu

F32 = jnp.float32
I32 = jnp.int32
MXU_DTYPE = jnp.bfloat16

HEAD_DIM = 64
CHUNK = 64
ROPE_THETA = 10000.0
LN_EPS = 1e-5
DEPTH = 2
ALPHA = (2.0 * DEPTH) ** 0.25
A_GROUPS = 8
A_CHUNK = 128
B_KV_HEADS = 2
B_GROUP = 4
B_WINDOW = 128
C_KV_HEADS = 4
C_GROUP = 4
IDX_HEADS = 4
IDX_DIM = 64
TOPK_MAX = 256

LANES = 128
KEY_TILE = 1024
SUB_TILE = 256
QUERY_BLOCK = 128
INT_MIN = -(2 ** 31)
NEG_BIG = -(2.0 ** 100)
LOG2E = math.log2(math.e)
V_ROWS = 80
VMEM_LIMIT = 56 * 1024 * 1024


def _gelu(x):
    c = math.sqrt(2.0 / math.pi)
    return 0.5 * x * (1.0 + jnp.tanh(c * (x + 0.044715 * (x * x * x))))


def _ln(z, g, b):
    mu = jnp.mean(z, axis=-1, keepdims=True)
    d = z - mu
    var = jnp.mean(d * d, axis=-1, keepdims=True)
    return d * lax.rsqrt(var + LN_EPS) * g + b


def _rope2(x, cos, sin):
    lane = lax.broadcasted_iota(I32, (1, LANES), 1)
    first = (lane % HEAD_DIM) < (HEAD_DIM // 2)
    swapped = jnp.where(first, pltpu.roll(x, LANES - HEAD_DIM // 2, 1), pltpu.roll(x, HEAD_DIM // 2, 1))
    return x * cos + swapped * sin


def _dot(a, b):
    return jnp.dot(a, b, preferred_element_type=F32)


def _dot_nt(a, b):
    return lax.dot_general(a, b, (((1,), (1,)), ((), ())), preferred_element_type=F32)


def _ab_in_kernel(x_ref, w_ref, cos_ref, sin_ref, g_ref, b_ref, u_ref, va_ref, q_ref, k_ref, v_ref):
    y = _dot(x_ref[...].astype(MXU_DTYPE), w_ref[...])
    u_ref[...] = _gelu(y[:, 0:512])
    va_ref[...] = _ln(_gelu(y[:, 512:1024]), g_ref[...], b_ref[...])
    cos = cos_ref[...]
    sin = sin_ref[...]
    for c in range(4):
        lo = 1024 + LANES * c
        q_ref[:, LANES * c:LANES * (c + 1)] = (
            _rope2(y[:, lo:lo + LANES], cos, sin) * (HEAD_DIM ** -0.5)).astype(q_ref.dtype)
    k_ref[...] = _rope2(y[:, 1536:1664], cos, sin)
    v_ref[...] = y[:, 1664:1792]


def _ab_in(x2d, w, cos, sin, g, b, *, t, tm):
    rows = x2d.shape[0]
    nt = t // tm
    row_spec = lambda w_: pl.BlockSpec((tm, w_), lambda i: (i, 0))
    full = lambda a: pl.BlockSpec(a.shape, lambda i: (0,) * a.ndim)
    tab = pl.BlockSpec((tm, LANES), lambda i: (i % nt, 0))
    return pl.pallas_call(
        _ab_in_kernel,
        grid=(rows // tm,),
        in_specs=[row_spec(1024), full(w), tab, tab, full(g), full(b)],
        out_specs=[row_spec(512), row_spec(512), row_spec(512), row_spec(128), row_spec(128)],
        out_shape=[
            jax.ShapeDtypeStruct((rows, 512), F32),
            jax.ShapeDtypeStruct((rows, 512), F32),
            jax.ShapeDtypeStruct((rows, 512), MXU_DTYPE),
            jax.ShapeDtypeStruct((rows, 128), F32),
            jax.ShapeDtypeStruct((rows, 128), F32),
        ],
        compiler_params=pltpu.CompilerParams(
            dimension_semantics=("arbitrary",), vmem_limit_bytes=VMEM_LIMIT),
        name="ab_in",
    )(x2d, w, cos, sin, g, b)


def _ab_mix_kernel(sink_ref, u_ref, va_ref, q_ref, k_ref, v_ref, hk_ref, hv_ref,
                   ws_ref, bias_ref, o_ref, *, rows, cs, mask_first):
    t = pl.program_id(1)
    lane = lax.broadcasted_iota(I32, (1, LANES), 1)
    lo_half = lane < HEAD_DIM

    r_i = lax.broadcasted_iota(I32, (cs, cs), 0)
    c_i = lax.broadcasted_iota(I32, (cs, cs), 1)
    tril = r_i >= c_i
    w_tril = [jnp.where(tril, ws_ref[g], 0.0).astype(MXU_DTYPE) for g in range(A_GROUPS)]
    for c in range(rows // cs):
        rs = slice(c * cs, (c + 1) * cs)
        for p in range(A_GROUPS // 2):
            ls = slice(LANES * p, LANES * (p + 1))
            vp = va_ref[0, rs, ls].astype(MXU_DTYPE)
            gate = jnp.where(lo_half, _dot(w_tril[2 * p], vp), _dot(w_tril[2 * p + 1], vp)) + bias_ref[:, ls]
            o_ref[0, rs, ls] = (u_ref[0, rs, ls] * gate).astype(o_ref.dtype)

    kcat = jnp.concatenate([hk_ref[0], k_ref[0]], axis=0).astype(MXU_DTYPE)
    vcat = jnp.concatenate([hv_ref[0], v_ref[0]], axis=0).astype(MXU_DTYPE)
    nwin = B_WINDOW + CHUNK
    head_of_row = lax.broadcasted_iota(I32, (B_GROUP * CHUNK, 1), 0) // CHUNK
    col = lax.broadcasted_iota(I32, (1, nwin), 1)
    for j in range(rows // CHUNK):
        rs = slice(CHUNK * j, CHUNK * (j + 1))
        kwin = kcat[CHUNK * j:CHUNK * j + nwin]
        vwin = vcat[CHUNK * j:CHUNK * j + nwin]
        outs = []
        for h in range(B_KV_HEADS):
            half = lo_half if h == 0 else jnp.logical_not(lo_half)
            q4 = jnp.concatenate(
                [jnp.where(half, q_ref[0, rs, LANES * g:LANES * (g + 1)], 0).astype(MXU_DTYPE)
                 for g in range(B_GROUP)], axis=0)
            s = _dot_nt(q4, kwin)
            if mask_first:
                s = jnp.where(t * rows + CHUNK * j - B_WINDOW + col >= 0, s, -jnp.inf)
            sink = jnp.zeros((B_GROUP * CHUNK, 1), F32)
            for g in range(B_GROUP):
                sink = jnp.where(head_of_row == g, sink_ref[B_GROUP * h + g], sink)
            m = jnp.maximum(jnp.max(s, axis=-1, keepdims=True), sink)
            e = jnp.exp(s - m)
            p = e / (jnp.sum(e, axis=-1, keepdims=True) + jnp.exp(sink - m))
            outs.append(_dot(p.astype(MXU_DTYPE), vwin))
        for g in range(B_GROUP):
            gs = slice(CHUNK * g, CHUNK * (g + 1))
            o_ref[0, rs, 512 + LANES * g:512 + LANES * (g + 1)] = jnp.where(
                lo_half, outs[0][gs], outs[1][gs]).astype(o_ref.dtype)


def _ab_mix(sinks, u, va, q, k, v, hist_k, hist_v, ws, bias, *, rows, cs, hist_from_self):
    bsz, t, _ = u.shape
    full = lambda a: pl.BlockSpec(a.shape, lambda bi, ti: (0,) * a.ndim)
    blk = lambda w_: pl.BlockSpec((1, rows, w_), lambda bi, ti: (bi, ti, 0))
    if hist_from_self:
        per = rows // B_WINDOW
        hist = pl.BlockSpec((1, B_WINDOW, 128), lambda bi, ti: (bi, jnp.maximum(ti * per - 1, 0), 0))
    else:
        hist = pl.BlockSpec((1, B_WINDOW, 128), lambda bi, ti: (bi, 0, 0))
    kern = functools.partial(_ab_mix_kernel, rows=rows, cs=cs, mask_first=hist_from_self)
    return pl.pallas_call(
        kern,
        grid=(bsz, t // rows),
        in_specs=[pl.BlockSpec(memory_space=pltpu.SMEM),
                  blk(512), blk(512), blk(512), blk(128), blk(128), hist, hist,
                  full(ws), full(bias)],
        out_specs=blk(1024),
        out_shape=jax.ShapeDtypeStruct((bsz, t, 1024), MXU_DTYPE),
        compiler_params=pltpu.CompilerParams(
            dimension_semantics=("arbitrary", "arbitrary"), vmem_limit_bytes=VMEM_LIMIT),
        name="ab_mix",
    )(sinks, u, va, q, k, v, hist_k, hist_v, ws, bias)


def _post_kernel(x_ref, a_ref, wo_ref, g1_ref, b1_ref, w1_ref, w2_ref, g2_ref, b2_ref, o_ref, *, ff_tile):
    x = _ln(ALPHA * x_ref[...] + _dot(a_ref[...], wo_ref[...]), g1_ref[...], b1_ref[...])
    xb = x.astype(MXU_DTYPE)
    acc = jnp.zeros(x.shape, F32)
    for c in range(w1_ref.shape[1] // ff_tile):
        h = _dot(xb, w1_ref[:, c * ff_tile:(c + 1) * ff_tile])
        h = jnp.square(jnp.maximum(h, 0.0)).astype(MXU_DTYPE)
        acc = acc + _dot(h, w2_ref[c * ff_tile:(c + 1) * ff_tile, :])
    o_ref[...] = _ln(ALPHA * x + acc, g2_ref[...], b2_ref[...])


def _post(x2d, a2d, wo, g1, b1, w1, w2, g2, b2, *, tm):
    rows, d = x2d.shape
    full = lambda a: pl.BlockSpec(a.shape, lambda i: (0,) * a.ndim, pipeline_mode=pl.Buffered(1))
    row = pl.BlockSpec((tm, d), lambda i: (i, 0))
    return pl.pallas_call(
        functools.partial(_post_kernel, ff_tile=1024),
        grid=(rows // tm,),
        in_specs=[row, row, full(wo), full(g1), full(b1), full(w1), full(w2), full(g2), full(b2)],
        out_specs=row,
        out_shape=jax.ShapeDtypeStruct((rows, d), F32),
        compiler_params=pltpu.CompilerParams(
            dimension_semantics=("arbitrary",), vmem_limit_bytes=VMEM_LIMIT),
        name="post",
    )(x2d, a2d, wo, g1, b1, w1, w2, g2, b2)


def _c_in_kernel(x_ref, w_ref, cos_ref, sin_ref, q_ref, k_ref, kb_ref, v_ref, qi_ref, kk_ref, kkb_ref, wi_ref):
    y = _dot(x_ref[...].astype(MXU_DTYPE), w_ref[...])
    cos = cos_ref[...]
    sin = sin_ref[...]
    for c in range(8):
        q_ref[:, LANES * c:LANES * (c + 1)] = (
            _rope2(y[:, LANES * c:LANES * (c + 1)], cos, sin) * (LOG2E * HEAD_DIM ** -0.5)).astype(q_ref.dtype)
    for c in range(2):
        kr = _rope2(y[:, 1024 + LANES * c:1024 + LANES * (c + 1)], cos, sin)
        k_ref[:, LANES * c:LANES * (c + 1)] = kr
        kb_ref[:, LANES * c:LANES * (c + 1)] = kr.astype(kb_ref.dtype)
    v_ref[...] = y[:, 1280:1536]
    for c in range(2):
        qi_ref[:, LANES * c:LANES * (c + 1)] = _rope2(
            y[:, 1536 + LANES * c:1536 + LANES * (c + 1)], cos, sin).astype(qi_ref.dtype)
    kk = _rope2(y[:, 1792:1920], cos, sin)
    kk_ref[...] = kk
    kkb_ref[...] = kk.astype(kkb_ref.dtype)
    wi_ref[...] = y[:, 1920:2048]


def _c_in(x2d, w, cos, sin, *, t, tm):
    rows = x2d.shape[0]
    nt = t // tm
    row_spec = lambda w_: pl.BlockSpec((tm, w_), lambda i: (i, 0))
    full = lambda a: pl.BlockSpec(a.shape, lambda i: (0,) * a.ndim)
    tab = pl.BlockSpec((tm, LANES), lambda i: (i % nt, 0))
    widths = [(1024, MXU_DTYPE), (256, F32), (256, MXU_DTYPE), (256, F32), (256, MXU_DTYPE),
              (128, F32), (128, MXU_DTYPE), (128, F32)]
    return pl.pallas_call(
        _c_in_kernel,
        grid=(rows // tm,),
        in_specs=[row_spec(1024), full(w), tab, tab],
        out_specs=[row_spec(w_) for w_, _ in widths],
        out_shape=[jax.ShapeDtypeStruct((rows, w_), dt) for w_, dt in widths],
        compiler_params=pltpu.CompilerParams(
            dimension_semantics=("arbitrary",), vmem_limit_bytes=VMEM_LIMIT),
        name="c_in",
    )(x2d, w, cos, sin)


def _bit_transpose32(words):
    a = list(words)
    j, m = 16, 0x0000FFFF
    while j:
        k = 0
        while k < 32:
            t = (a[k] ^ lax.shift_right_logical(a[k + j], jnp.int32(j))) & m
            a[k] = a[k] ^ t
            a[k + j] = a[k + j] ^ lax.shift_left(t, jnp.int32(j))
            k = (k + j + 1) & ~j
        j >>= 1
        m = (m ^ (m << j)) & 0xFFFFFFFF
    return a


def _dsa_kernel(q_ref, qi_ref, w_ref, k_ref, vt_ref, kk_ref, o_ref,
                key_ref, planes_ref, alive_ref, rhs_ref, m_ref, acc_ref, ot_ref,
                *, causal, n_keys, ksel):
    tk, st_ = KEY_TILE, SUB_TILE
    i = pl.program_id(1)
    lane = lax.broadcasted_iota(I32, (1, LANES), 1)
    lo_half = lane < HEAD_DIM
    hi_half = jnp.logical_not(lo_half)
    if causal:
        limit = CHUNK * (2 * i + jnp.where(lo_half, jnp.int32(0), jnp.int32(1)) + 1)
        n_hi = QUERY_BLOCK * (i + 1)
    else:
        limit = jnp.full((1, LANES), n_keys, I32)
        n_hi = n_keys
    sub = lax.broadcasted_iota(I32, (st_, LANES), 0)
    per_tile = tk // st_
    n_sub_tiles = (n_hi + st_ - 1) // st_
    n_full = n_sub_tiles // per_tile
    n_rem = n_sub_tiles - n_full * per_tile

    def over_keys(body, carry):
        carry = lax.fori_loop(0, n_full, lambda t, c: body(t * per_tile, per_tile, c), carry)
        return lax.fori_loop(0, n_rem, lambda r, c: body(n_full * per_tile + r, 1, c), carry)

    def key_off(j):
        return pl.multiple_of(j * st_, st_)

    def plane_row(j):
        return pl.multiple_of(j * (st_ // 32), 8)

    qi_blk = qi_ref[0]
    qi4 = jnp.concatenate(
        [jnp.where(lo_half if h % 2 == 0 else hi_half, qi_blk[:, LANES * (h // 2):LANES * (h // 2 + 1)], 0)
         for h in range(IDX_HEADS)], axis=0)
    w = w_ref[0]

    def score_tile(j0, n_sub, carry):
        for s in range(n_sub):
            off = key_off(j0 + s)
            sc4 = _dot_nt(kk_ref[0, pl.ds(off, st_), :], qi4)
            sc = jnp.maximum(sc4[:, 0:LANES], 0.0) * w[0:1]
            for h in range(1, IDX_HEADS):
                sc = sc + jnp.maximum(sc4[:, LANES * h:LANES * (h + 1)], 0.0) * w[h:h + 1]
            bits = pltpu.bitcast(sc, I32)
            key = jnp.where(bits < 0, bits ^ 0x7FFFFFFF, bits)
            key = jnp.where(sub + off < limit, key, INT_MIN)
            key_ref[pl.ds(off, st_), :] = key
            ukey = key ^ INT_MIN
            planes = _bit_transpose32([ukey[8 * r:8 * (r + 1)] for r in range(32)])
            row = plane_row(j0 + s)
            for b in range(32):
                planes_ref[b, pl.ds(row, 8), :] = planes[b]
        return carry

    over_keys(score_tile, 0)

    def plane_count(b, flip, first):
        def body(j0, n_sub, acc):
            for s in range(n_sub):
                row = plane_row(j0 + s)
                if first:
                    alive = jnp.full((8, LANES), -1, I32)
                else:
                    alive = alive_ref[pl.ds(row, 8), :] & (planes_ref[b - 1, pl.ds(row, 8), :] ^ flip)
                alive_ref[pl.ds(row, 8), :] = alive
                acc = acc + lax.population_count(alive & planes_ref[b, pl.ds(row, 8), :])
            return acc
        return jnp.sum(over_keys(body, jnp.zeros((8, LANES), I32)), axis=0, keepdims=True)

    def decide(b, cnt, above, ubits):
        take = above + cnt >= ksel
        ubits = ubits | jnp.where(take, lax.shift_left(jnp.int32(1), jnp.int32(31) - b), jnp.int32(0))
        above = jnp.where(take, above, above + cnt)
        return above, ubits, jnp.where(take, jnp.int32(0), jnp.int32(-1))

    zero_row = jnp.zeros((1, LANES), I32)
    state = decide(0, plane_count(0, None, True), zero_row, zero_row)

    def bit_step(b, state):
        above, ubits, flip = state
        return decide(b, plane_count(b, flip, False), above, ubits)

    above, ubits, _ = lax.fori_loop(1, 32, bit_step, state)
    thr = ubits ^ INT_MIN

    need = ksel - above
    need = jnp.where(thr == INT_MIN, jnp.int32(0), need).astype(F32)
    tril = (lax.broadcasted_iota(I32, (st_, st_), 0) >= lax.broadcasted_iota(I32, (st_, st_), 1)
            ).astype(F32).astype(MXU_DTYPE)
    keep = jnp.zeros((st_, LANES), F32)
    drop = jnp.full((st_, LANES), NEG_BIG, F32)
    f_one = jnp.ones((st_, LANES), F32)

    def tile_bias(j0, n_sub, seen):
        blks = [key_ref[pl.ds(key_off(j0 + s), st_), :] for s in range(n_sub)]
        ranks = [_dot(tril, jnp.where(blk == thr, f_one, keep).astype(MXU_DTYPE)) for blk in blks]
        out = []
        for blk, rank in zip(blks, ranks):
            rank = rank + seen
            tie_bias = jnp.where(blk == thr, jnp.where(rank <= need, keep, drop), drop)
            out.append(jnp.where(blk > thr, keep, tie_bias).astype(MXU_DTYPE))
            seen = rank[st_ - 1:st_, :]
        return out, seen

    eye4 = (lax.broadcasted_iota(I32, (LANES, C_GROUP * LANES), 1) % LANES
            == lax.broadcasted_iota(I32, (LANES, C_GROUP * LANES), 0)).astype(F32).astype(MXU_DTYPE)
    for h in range(C_KV_HEADS):
        half = lo_half if h % 2 == 0 else hi_half
        q4t = jnp.concatenate(
            [jnp.transpose(jnp.where(half, q_ref[0, :, LANES * (4 * (h // 2) + g):LANES * (4 * (h // 2) + g + 1)],
                                     0).astype(F32)) for g in range(C_GROUP)], axis=1)
        rhs_ref[h] = jnp.concatenate([q4t.astype(MXU_DTYPE), eye4], axis=0)
    m_ref[...] = jnp.full(m_ref.shape, NEG_BIG, F32)
    acc_ref[...] = jnp.zeros(acc_ref.shape, F32)

    def attn_tile(j0, n_sub, seen):
        chains = [(h, s) for s in range(n_sub) for h in range(C_KV_HEADS)]
        bias, seen = tile_bias(j0, n_sub, seen)
        scores = []
        for h, s in chains:
            kcol = slice(LANES * (h // 2), LANES * (h // 2 + 1))
            lhs = jnp.concatenate([k_ref[0, pl.ds(key_off(j0 + s), st_), kcol], bias[s]], axis=1)
            scores.append(_dot(lhs, rhs_ref[h]))
        for (h, s), sc in zip(chains, scores):
            sc = sc.astype(MXU_DTYPE)
            m_old = m_ref[h]
            m_new = jnp.maximum(m_old, jnp.max(sc, axis=0, keepdims=True).astype(F32))
            p = jnp.exp2(sc - m_new.astype(MXU_DTYPE))
            vt = vt_ref[0, V_ROWS * h:V_ROWS * (h + 1), pl.ds(key_off(j0 + s), st_)]
            acc_ref[h] = jnp.exp2(m_old - m_new) * acc_ref[h] + _dot(vt, p)
            m_ref[h] = m_new
        return seen

    over_keys(attn_tile, jnp.zeros((1, LANES), F32))

    for h in range(C_KV_HEADS):
        acc = acc_ref[h]
        o = acc[0:HEAD_DIM] / acc[HEAD_DIM:HEAD_DIM + 1]
        for g in range(C_GROUP):
            slot = 8 * (h // 2) + 2 * g + (h % 2)
            ot_ref[HEAD_DIM * slot:HEAD_DIM * (slot + 1), :] = o[:, LANES * g:LANES * (g + 1)]

    o_ref[0] = jnp.transpose(ot_ref[...]).astype(o_ref.dtype)


def _dsa(q, qi, w_t, k, v_t, kk, *, causal, n_keys, ksel):
    bsz, t, _ = q.shape
    nk = k.shape[1]
    qb = QUERY_BLOCK
    blk = lambda w_: pl.BlockSpec((1, qb, w_), lambda bi, qi_: (bi, qi_, 0))
    per_b = lambda a: pl.BlockSpec((1,) + a.shape[1:], lambda bi, qi_: (bi, 0, 0))
    kern = functools.partial(_dsa_kernel, causal=causal, n_keys=n_keys, ksel=ksel)
    return pl.pallas_call(
        kern,
        grid=(bsz, t // qb),
        in_specs=[blk(1024), blk(256),
                  pl.BlockSpec((1, 8, qb), lambda bi, qi_: (bi, 0, qi_)),
                  per_b(k), per_b(v_t), per_b(kk)],
        out_specs=blk(1024),
        out_shape=jax.ShapeDtypeStruct((bsz, t, 1024), MXU_DTYPE),
        scratch_shapes=[pltpu.VMEM((nk, LANES), I32),
                        pltpu.VMEM((32, nk // 32, LANES), I32),
                        pltpu.VMEM((nk // 32, LANES), I32),
                        pltpu.VMEM((C_KV_HEADS, 2 * LANES, C_GROUP * LANES), MXU_DTYPE),
                        pltpu.VMEM((C_KV_HEADS, 1, C_GROUP * LANES), F32),
                        pltpu.VMEM((C_KV_HEADS, V_ROWS, C_GROUP * LANES), F32),
                        pltpu.VMEM((1024, LANES), F32)],
        compiler_params=pltpu.CompilerParams(
            dimension_semantics=("arbitrary", "arbitrary"), vmem_limit_bytes=VMEM_LIMIT),
        name="dsa",
    )(q, qi, w_t, k, v_t, kk)


def _slot_perm(head_of_slot):
    return np.concatenate([np.arange(HEAD_DIM * j, HEAD_DIM * (j + 1)) for j in head_of_slot])


_AB_Q_PERM = _slot_perm([B_GROUP * (s % 2) + s // 2 for s in range(8)])
_C_Q_PERM = _slot_perm([4 * (2 * (s // 8) + (s % 8) % 2) + (s % 8) // 2 for s in range(16)])


def _rope_tables(pos):
    half = HEAD_DIM // 2
    inv_freq = jnp.exp(-math.log(ROPE_THETA) * jnp.arange(half, dtype=F32) / half)
    ang = pos.astype(F32)[:, None] * inv_freq[None, :]
    cos, sin = jnp.cos(ang), jnp.sin(ang)
    return jnp.concatenate([cos] * 4, axis=1), jnp.concatenate([-sin, sin, -sin, sin], axis=1)


def _value_operand(v):
    bsz, n, _ = v.shape
    vt = jnp.swapaxes(v, 1, 2).reshape(bsz, C_KV_HEADS, HEAD_DIM, n)
    ones = jnp.ones((bsz, C_KV_HEADS, 1, n), v.dtype)
    zeros = jnp.zeros((bsz, C_KV_HEADS, V_ROWS - HEAD_DIM - 1, n), v.dtype)
    return jnp.concatenate([vt, ones, zeros], axis=2).reshape(bsz, C_KV_HEADS * V_ROWS, n)


def _round_up(n, m):
    return (n + m - 1) // m * m


def _trunk(x, pos, caches, wts):
    (ab_w_in, a_ln_g, a_ln_b, a_ws, a_bs, b_sinks, ab_w_out, c_w_in, c_w_out,
     ln1_g, ln1_b, ln2_g, ln2_b, ff_w1, ff_w2) = wts
    bsz, t, d = x.shape
    sample = caches is not None
    cos, sin = _rope_tables(pos)
    tm = min(512, t)
    row = lambda a: a.reshape(1, -1)

    w_in = jnp.concatenate([ab_w_in[0][:, :1024], ab_w_in[0][:, 1024:1536][:, _AB_Q_PERM],
                            ab_w_in[0][:, 1536:]], axis=1).astype(MXU_DTYPE)
    w_out = jnp.concatenate([ab_w_out[0][:512], ab_w_out[0][512:][_AB_Q_PERM]], axis=0).astype(MXU_DTYPE)
    u, va, q, k, v = _ab_in(x.reshape(bsz * t, d), w_in, cos, sin, row(a_ln_g[0]), row(a_ln_b[0]), t=t, tm=tm)
    r3 = lambda a: a.reshape(bsz, t, a.shape[-1])
    u, va, q, k, v = r3(u), r3(va), r3(q), r3(k), r3(v)
    cs = min(A_CHUNK, t)
    ws = a_ws[0][:, :cs, :cs]
    bias = jnp.repeat(a_bs[0][:, :cs].T, HEAD_DIM, axis=1)
    if sample:
        hist_k = caches[0][0].reshape(bsz, B_WINDOW, 128)
        hist_v = caches[1][0].reshape(bsz, B_WINDOW, 128)
        rows = t
    else:
        hist_k, hist_v = k, v
        rows = min(256, t)
    mixed = _ab_mix(b_sinks[0], u, va, q, k, v, hist_k, hist_v, ws, bias,
                    rows=rows, cs=cs, hist_from_self=not sample)
    x = _post(x.reshape(bsz * t, d), mixed.reshape(bsz * t, d), w_out, row(ln1_g[0]), row(ln1_b[0]),
              ff_w1[0].astype(MXU_DTYPE), ff_w2[0].astype(MXU_DTYPE),
              row(ln2_g[0]), row(ln2_b[0]), tm=tm).reshape(bsz, t, d)
    b_k = k.reshape(bsz, t, B_KV_HEADS, HEAD_DIM)
    b_v = v.reshape(bsz, t, B_KV_HEADS, HEAD_DIM)

    cw = c_w_in[0]
    w_in = jnp.concatenate(
        [cw[:, :1024][:, _C_Q_PERM], cw[:, 1024:1792], cw[:, 1792:1856], cw[:, 1792:1856], cw[:, 1856:1860],
         jnp.zeros((d, LANES - IDX_HEADS), F32)], axis=1).astype(MXU_DTYPE)
    w_out = c_w_out[0][_C_Q_PERM].astype(MXU_DTYPE)
    q, k, kb, v, qi, kk, kkb, wi = _c_in(x.reshape(bsz * t, d), w_in, cos, sin, t=t, tm=tm)
    q, k, kb, v, qi, kk, kkb, wi = (r3(a) for a in (q, k, kb, v, qi, kk, kkb, wi))
    ki = kk[:, :, :IDX_DIM]
    w_t = jnp.swapaxes(wi[:, :, :8], 1, 2) * (IDX_DIM ** -0.5 * IDX_HEADS ** -0.5)
    if sample:
        keys_k = jnp.concatenate([caches[2][0].reshape(bsz, -1, 256).astype(MXU_DTYPE), kb], axis=1)
        keys_v = jnp.concatenate([caches[3][0].reshape(bsz, -1, 256), v], axis=1).astype(MXU_DTYPE)
        ci = caches[4][0].astype(MXU_DTYPE)
        keys_i = jnp.concatenate([jnp.concatenate([ci, ci], axis=-1), kkb], axis=1)
        n_keys = keys_k.shape[1]
        pad_k = _round_up(n_keys, SUB_TILE) - n_keys
        keys_k, keys_v, keys_i = (jnp.pad(a, ((0, 0), (0, pad_k), (0, 0))) for a in (keys_k, keys_v, keys_i))
        pad_q = QUERY_BLOCK - t
        padq = lambda a: jnp.pad(a, ((0, 0), (0, pad_q), (0, 0)))
        mixed = _dsa(padq(q), padq(qi), jnp.pad(w_t, ((0, 0), (0, 0), (0, pad_q))), keys_k,
                     _value_operand(keys_v), keys_i,
                     causal=False, n_keys=n_keys, ksel=min(TOPK_MAX, n_keys // 4))[:, :t]
    else:
        mixed = _dsa(q, qi, w_t, kb, _value_operand(v.astype(MXU_DTYPE)), kkb,
                     causal=True, n_keys=t, ksel=min(TOPK_MAX, t // 4))
    x = _post(x.reshape(bsz * t, d), mixed.reshape(bsz * t, d), w_out, row(ln1_g[1]), row(ln1_b[1]),
              ff_w1[1].astype(MXU_DTYPE), ff_w2[1].astype(MXU_DTYPE),
              row(ln2_g[1]), row(ln2_b[1]), tm=tm).reshape(bsz, t, d)
    c_k = k.reshape(bsz, t, C_KV_HEADS, HEAD_DIM)
    c_v = v.reshape(bsz, t, C_KV_HEADS, HEAD_DIM)
    return x, va[None], b_k[None], b_v[None], c_k[None], c_v[None], ki[None]


def kernel(x_prompt, x_sample, cache_b_k, cache_b_v, cache_c_k, cache_c_v, cache_c_idx, ab_w_in, a_ln_g, a_ln_b, a_ws, a_bs, b_sinks, ab_w_out, c_w_in, c_w_out, ln1_g, ln1_b, ln2_g, ln2_b, ff_w1, ff_w2):
    wts = (ab_w_in, a_ln_g, a_ln_b, a_ws, a_bs, b_sinks, ab_w_out, c_w_in, c_w_out,
           ln1_g, ln1_b, ln2_g, ln2_b, ff_w1, ff_w2)
    past_len = cache_c_k.shape[2]
    pos_p = jnp.arange(x_prompt.shape[1], dtype=jnp.int32)
    pos_s = past_len + jnp.arange(x_sample.shape[1], dtype=jnp.int32)
    y_p, _, p_b_k, p_b_v, p_c_k, p_c_v, p_c_idx = _trunk(x_prompt, pos_p, None, wts)
    y_s, s_a_v, s_b_k, s_b_v, s_c_k, s_c_v, s_c_idx = _trunk(
        x_sample, pos_s, (cache_b_k, cache_b_v, cache_c_k, cache_c_v, cache_c_idx), wts)
    return (y_p, y_s, p_b_k[:, :, -B_WINDOW:], p_b_v[:, :, -B_WINDOW:], p_c_k, p_c_v, p_c_idx,
            s_a_v, s_b_k, s_b_v, s_c_k, s_c_v, s_c_idx)
```

```python
import functools
import math

import numpy as np
import jax
import jax.numpy as jnp
from jax import lax
from jax.experimental import pallas as pl
from jax.experimental.pallas import tpu as pltpu

F32 = jnp.float32
I32 = jnp.int32
MXU_DTYPE = jnp.bfloat16

HEAD_DIM = 64
CHUNK = 64
ROPE_THETA = 10000.0
LN_EPS = 1e-5
DEPTH = 2
ALPHA = (2.0 * DEPTH) ** 0.25
A_GROUPS = 8
A_CHUNK = 128
B_KV_HEADS = 2
B_GROUP = 4
B_WINDOW = 128
C_KV_HEADS = 4
C_GROUP = 4
IDX_HEADS = 4
IDX_DIM = 64
TOPK_MAX = 256

LANES = 128
KEY_TILE = 1024
SUB_TILE = 256
QUERY_BLOCK = 128
INT_MIN = -(2 ** 31)
NEG_BIG = -(2.0 ** 100)
LOG2E = math.log2(math.e)
V_ROWS = 80
VMEM_LIMIT = 56 * 1024 * 1024


def _gelu(x):
    c = math.sqrt(2.0 / math.pi)
    return 0.5 * x * (1.0 + jnp.tanh(c * (x + 0.044715 * (x * x * x))))


def _ln(z, g, b):
    mu = jnp.mean(z, axis=-1, keepdims=True)
    d = z - mu
    var = jnp.mean(d * d, axis=-1, keepdims=True)
    return d * lax.rsqrt(var + LN_EPS) * g + b


def _rope2(x, cos, sin):
    lane = lax.broadcasted_iota(I32, (1, LANES), 1)
    first = (lane % HEAD_DIM) < (HEAD_DIM // 2)
    swapped = jnp.where(first, pltpu.roll(x, LANES - HEAD_DIM // 2, 1), pltpu.roll(x, HEAD_DIM // 2, 1))
    return x * cos + swapped * sin


def _dot(a, b):
    return jnp.dot(a, b, preferred_element_type=F32)


def _dot_nt(a, b):
    return lax.dot_general(a, b, (((1,), (1,)), ((), ())), preferred_element_type=F32)


def _ab_in_kernel(x_ref, w_ref, cos_ref, sin_ref, g_ref, b_ref, u_ref, va_ref, q_ref, k_ref, v_ref):
    y = _dot(x_ref[...].astype(MXU_DTYPE), w_ref[...])
    u_ref[...] = _gelu(y[:, 0:512])
    va_ref[...] = _ln(_gelu(y[:, 512:1024]), g_ref[...], b_ref[...])
    cos = cos_ref[...]
    sin = sin_ref[...]
    for c in range(4):
        lo = 1024 + LANES * c
        q_ref[:, LANES * c:LANES * (c + 1)] = (
            _rope2(y[:, lo:lo + LANES], cos, sin) * (HEAD_DIM ** -0.5)).astype(q_ref.dtype)
    k_ref[...] = _rope2(y[:, 1536:1664], cos, sin)
    v_ref[...] = y[:, 1664:1792]


def _ab_in(x2d, w, cos, sin, g, b, *, t, tm):
    rows = x2d.shape[0]
    nt = t // tm
    row_spec = lambda w_: pl.BlockSpec((tm, w_), lambda i: (i, 0))
    full = lambda a: pl.BlockSpec(a.shape, lambda i: (0,) * a.ndim)
    tab = pl.BlockSpec((tm, LANES), lambda i: (i % nt, 0))
    return pl.pallas_call(
        _ab_in_kernel,
        grid=(rows // tm,),
        in_specs=[row_spec(1024), full(w), tab, tab, full(g), full(b)],
        out_specs=[row_spec(512), row_spec(512), row_spec(512), row_spec(128), row_spec(128)],
        out_shape=[
            jax.ShapeDtypeStruct((rows, 512), F32),
            jax.ShapeDtypeStruct((rows, 512), F32),
            jax.ShapeDtypeStruct((rows, 512), MXU_DTYPE),
            jax.ShapeDtypeStruct((rows, 128), F32),
            jax.ShapeDtypeStruct((rows, 128), F32),
        ],
        compiler_params=pltpu.CompilerParams(
            dimension_semantics=("arbitrary",), vmem_limit_bytes=VMEM_LIMIT),
        name="ab_in",
    )(x2d, w, cos, sin, g, b)


def _ab_mix_kernel(sink_ref, u_ref, va_ref, q_ref, k_ref, v_ref, hk_ref, hv_ref,
                   ws_ref, bias_ref, o_ref, *, rows, cs, mask_first):
    t = pl.program_id(1)
    lane = lax.broadcasted_iota(I32, (1, LANES), 1)
    lo_half = lane < HEAD_DIM

    r_i = lax.broadcasted_iota(I32, (cs, cs), 0)
    c_i = lax.broadcasted_iota(I32, (cs, cs), 1)
    tril = r_i >= c_i
    w_tril = [jnp.where(tril, ws_ref[g], 0.0).astype(MXU_DTYPE) for g in range(A_GROUPS)]
    for c in range(rows // cs):
        rs = slice(c * cs, (c + 1) * cs)
        for p in range(A_GROUPS // 2):
            ls = slice(LANES * p, LANES * (p + 1))
            vp = va_ref[0, rs, ls].astype(MXU_DTYPE)
            gate = jnp.where(lo_half, _dot(w_tril[2 * p], vp), _dot(w_tril[2 * p + 1], vp)) + bias_ref[:, ls]
            o_ref[0, rs, ls] = (u_ref[0, rs, ls] * gate).astype(o_ref.dtype)

    kcat = jnp.concatenate([hk_ref[0], k_ref[0]], axis=0).astype(MXU_DTYPE)
    vcat = jnp.concatenate([hv_ref[0], v_ref[0]], axis=0).astype(MXU_DTYPE)
    nwin = B_WINDOW + CHUNK
    head_of_row = lax.broadcasted_iota(I32, (B_GROUP * CHUNK, 1), 0) // CHUNK
    col = lax.broadcasted_iota(I32, (1, nwin), 1)
    for j in range(rows // CHUNK):
        rs = slice(CHUNK * j, CHUNK * (j + 1))
        kwin = kcat[CHUNK * j:CHUNK * j + nwin]
        vwin = vcat[CHUNK * j:CHUNK * j + nwin]
        outs = []
        for h in range(B_KV_HEADS):
            half = lo_half if h == 0 else jnp.logical_not(lo_half)
            q4 = jnp.concatenate(
                [jnp.where(half, q_ref[0, rs, LANES * g:LANES * (g + 1)], 0).astype(MXU_DTYPE)
                 for g in range(B_GROUP)], axis=0)
            s = _dot_nt(q4, kwin)
            if mask_first:
                s = jnp.where(t * rows + CHUNK * j - B_WINDOW + col >= 0, s, -jnp.inf)
            sink = jnp.zeros((B_GROUP * CHUNK, 1), F32)
            for g in range(B_GROUP):
                sink = jnp.where(head_of_row == g, sink_ref[B_GROUP * h + g], sink)
            m = jnp.maximum(jnp.max(s, axis=-1, keepdims=True), sink)
            e = jnp.exp(s - m)
            p = e / (jnp.sum(e, axis=-1, keepdims=True) + jnp.exp(sink - m))
            outs.append(_dot(p.astype(MXU_DTYPE), vwin))
        for g in range(B_GROUP):
            gs = slice(CHUNK * g, CHUNK * (g + 1))
            o_ref[0, rs, 512 + LANES * g:512 + LANES * (g + 1)] = jnp.where(
                lo_half, outs[0][gs], outs[1][gs]).astype(o_ref.dtype)


def _ab_mix(sinks, u, va, q, k, v, hist_k, hist_v, ws, bias, *, rows, cs, hist_from_self):
    bsz, t, _ = u.shape
    full = lambda a: pl.BlockSpec(a.shape, lambda bi, ti: (0,) * a.ndim)
    blk = lambda w_: pl.BlockSpec((1, rows, w_), lambda bi, ti: (bi, ti, 0))
    if hist_from_self:
        per = rows // B_WINDOW
        hist = pl.BlockSpec((1, B_WINDOW, 128), lambda bi, ti: (bi, jnp.maximum(ti * per - 1, 0), 0))
    else:
        hist = pl.BlockSpec((1, B_WINDOW, 128), lambda bi, ti: (bi, 0, 0))
    kern = functools.partial(_ab_mix_kernel, rows=rows, cs=cs, mask_first=hist_from_self)
    return pl.pallas_call(
        kern,
        grid=(bsz, t // rows),
        in_specs=[pl.BlockSpec(memory_space=pltpu.SMEM),
                  blk(512), blk(512), blk(512), blk(128), blk(128), hist, hist,
                  full(ws), full(bias)],
        out_specs=blk(1024),
        out_shape=jax.ShapeDtypeStruct((bsz, t, 1024), MXU_DTYPE),
        compiler_params=pltpu.CompilerParams(
            dimension_semantics=("arbitrary", "arbitrary"), vmem_limit_bytes=VMEM_LIMIT),
        name="ab_mix",
    )(sinks, u, va, q, k, v, hist_k, hist_v, ws, bias)


def _post_kernel(x_ref, a_ref, wo_ref, g1_ref, b1_ref, w1_ref, w2_ref, g2_ref, b2_ref, o_ref, *, ff_tile):
    x = _ln(ALPHA * x_ref[...] + _dot(a_ref[...], wo_ref[...]), g1_ref[...], b1_ref[...])
    xb = x.astype(MXU_DTYPE)
    acc = jnp.zeros(x.shape, F32)
    for c in range(w1_ref.shape[1] // ff_tile):
        h = _dot(xb, w1_ref[:, c * ff_tile:(c + 1) * ff_tile])
        h = jnp.square(jnp.maximum(h, 0.0)).astype(MXU_DTYPE)
        acc = acc + _dot(h, w2_ref[c * ff_tile:(c + 1) * ff_tile, :])
    o_ref[...] = _ln(ALPHA * x + acc, g2_ref[...], b2_ref[...])


def _post(x2d, a2d, wo, g1, b1, w1, w2, g2, b2, *, tm):
    rows, d = x2d.shape
    full = lambda a: pl.BlockSpec(a.shape, lambda i: (0,) * a.ndim, pipeline_mode=pl.Buffered(1))
    row = pl.BlockSpec((tm, d), lambda i: (i, 0))
    return pl.pallas_call(
        functools.partial(_post_kernel, ff_tile=1024),
        grid=(rows // tm,),
        in_specs=[row, row, full(wo), full(g1), full(b1), full(w1), full(w2), full(g2), full(b2)],
        out_specs=row,
        out_shape=jax.ShapeDtypeStruct((rows, d), F32),
        compiler_params=pltpu.CompilerParams(
            dimension_semantics=("arbitrary",), vmem_limit_bytes=VMEM_LIMIT),
        name="post",
    )(x2d, a2d, wo, g1, b1, w1, w2, g2, b2)


def _c_in_kernel(x_ref, w_ref, cos_ref, sin_ref, q_ref, k_ref, kb_ref, v_ref, qi_ref, kk_ref, kkb_ref, wi_ref):
    y = _dot(x_ref[...].astype(MXU_DTYPE), w_ref[...])
    cos = cos_ref[...]
    sin = sin_ref[...]
    for c in range(8):
        q_ref[:, LANES * c:LANES * (c + 1)] = (
            _rope2(y[:, LANES * c:LANES * (c + 1)], cos, sin) * (LOG2E * HEAD_DIM ** -0.5)).astype(q_ref.dtype)
    for c in range(2):
        kr = _rope2(y[:, 1024 + LANES * c:1024 + LANES * (c + 1)], cos, sin)
        k_ref[:, LANES * c:LANES * (c + 1)] = kr
        kb_ref[:, LANES * c:LANES * (c + 1)] = kr.astype(kb_ref.dtype)
    v_ref[...] = y[:, 1280:1536]
    for c in range(2):
        qi_ref[:, LANES * c:LANES * (c + 1)] = _rope2(
            y[:, 1536 + LANES * c:1536 + LANES * (c + 1)], cos, sin).astype(qi_ref.dtype)
    kk = _rope2(y[:, 1792:1920], cos, sin)
    kk_ref[...] = kk
    kkb_ref[...] = kk.astype(kkb_ref.dtype)
    wi_ref[...] = y[:, 1920:2048]


def _c_in(x2d, w, cos, sin, *, t, tm):
    rows = x2d.shape[0]
    nt = t // tm
    row_spec = lambda w_: pl.BlockSpec((tm, w_), lambda i: (i, 0))
    full = lambda a: pl.BlockSpec(a.shape, lambda i: (0,) * a.ndim)
    tab = pl.BlockSpec((tm, LANES), lambda i: (i % nt, 0))
    widths = [(1024, MXU_DTYPE), (256, F32), (256, MXU_DTYPE), (256, F32), (256, MXU_DTYPE),
              (128, F32), (128, MXU_DTYPE), (128, F32)]
    return pl.pallas_call(
        _c_in_kernel,
        grid=(rows // tm,),
        in_specs=[row_spec(1024), full(w), tab, tab],
        out_specs=[row_spec(w_) for w_, _ in widths],
        out_shape=[jax.ShapeDtypeStruct((rows, w_), dt) for w_, dt in widths],
        compiler_params=pltpu.CompilerParams(
            dimension_semantics=("arbitrary",), vmem_limit_bytes=VMEM_LIMIT),
        name="c_in",
    )(x2d, w, cos, sin)


def _bit_transpose32(words):
    a = list(words)
    j, m = 16, 0x0000FFFF
    while j:
        k = 0
        while k < 32:
            t = (a[k] ^ lax.shift_right_logical(a[k + j], jnp.int32(j))) & m
            a[k] = a[k] ^ t
            a[k + j] = a[k + j] ^ lax.shift_left(t, jnp.int32(j))
            k = (k + j + 1) & ~j
        j >>= 1
        m = (m ^ (m << j)) & 0xFFFFFFFF
    return a


def _dsa_kernel(q_ref, qi_ref, w_ref, k_ref, vt_ref, kk_ref, o_ref,
                key_ref, planes_ref, rhs_ref, m_ref, acc_ref, ot_ref,
                *, causal, n_keys, ksel):
    tk, st_ = KEY_TILE, SUB_TILE
    i = pl.program_id(1)
    lane = lax.broadcasted_iota(I32, (1, LANES), 1)
    lo_half = lane < HEAD_DIM
    hi_half = jnp.logical_not(lo_half)
    if causal:
        limit = CHUNK * (2 * i + jnp.where(lo_half, jnp.int32(0), jnp.int32(1)) + 1)
        n_hi = QUERY_BLOCK * (i + 1)
    else:
        limit = jnp.full((1, LANES), n_keys, I32)
        n_hi = n_keys
    sub = lax.broadcasted_iota(I32, (st_, LANES), 0)
    per_tile = tk // st_
    n_sub_tiles = (n_hi + st_ - 1) // st_
    n_full = n_sub_tiles // per_tile
    n_rem = n_sub_tiles - n_full * per_tile

    def over_keys(body, carry):
        carry = lax.fori_loop(0, n_full, lambda t, c: body(t * per_tile, per_tile, c), carry)
        return lax.fori_loop(0, n_rem, lambda r, c: body(n_full * per_tile + r, 1, c), carry)

    def key_off(j):
        return pl.multiple_of(j * st_, st_)

    def plane_row(j):
        return pl.multiple_of(j * (st_ // 32), 8)

    qi_blk = qi_ref[0]
    qi4 = jnp.concatenate(
        [jnp.where(lo_half if h % 2 == 0 else hi_half, qi_blk[:, LANES * (h // 2):LANES * (h // 2 + 1)], 0)
         for h in range(IDX_HEADS)], axis=0)
    w = w_ref[0]

    @pl.when(jnp.logical_and(pl.program_id(0) == 0, i == 0))
    def _():
        planes_ref[...] = jnp.zeros(planes_ref.shape, I32)

    def score_tile(j0, n_sub, carry):
        for s in range(n_sub):
            off = key_off(j0 + s)
            sc4 = _dot_nt(kk_ref[0, pl.ds(off, st_), :], qi4)
            sc = jnp.maximum(sc4[:, 0:LANES], 0.0) * w[0:1]
            for h in range(1, IDX_HEADS):
                sc = sc + jnp.maximum(sc4[:, LANES * h:LANES * (h + 1)], 0.0) * w[h:h + 1]
            bits = pltpu.bitcast(sc, I32)
            key = jnp.where(bits < 0, bits ^ 0x7FFFFFFF, bits)
            key = jnp.where(sub + off < limit, key, INT_MIN)
            key_ref[pl.ds(off, st_), :] = key
            ukey = key ^ INT_MIN
            planes = _bit_transpose32([ukey[8 * r:8 * (r + 1)] for r in range(32)])
            row = plane_row(j0 + s)
            for b in range(32):
                planes_ref[b, pl.ds(row, 8), :] = planes[b]
        return carry

    over_keys(score_tile, 0)

    plane_rows = planes_ref.shape[1]
    group = lax.broadcasted_iota(I32, (plane_rows, LANES), 0) // (st_ // 32)
    alive0 = jnp.where(group < n_sub_tiles, jnp.int32(-1), jnp.int32(0))

    def bit_step(b, state):
        alive, above, ubits = state
        plane = planes_ref[b]
        hits = lax.population_count(alive & plane)
        cnt = jnp.sum(jnp.sum(hits.reshape(plane_rows // 8, 8, LANES), axis=0), axis=0, keepdims=True)
        take = above + cnt >= ksel
        ubits = ubits | jnp.where(take, lax.shift_left(jnp.int32(1), jnp.int32(31) - b), jnp.int32(0))
        above = jnp.where(take, above, above + cnt)
        alive = alive & (plane ^ jnp.where(take, jnp.int32(0), jnp.int32(-1)))
        return alive, above, ubits

    zero_row = jnp.zeros((1, LANES), I32)
    _, above, ubits = lax.fori_loop(0, 32, bit_step, (alive0, zero_row, zero_row))
    thr = ubits ^ INT_MIN

    need = ksel - above
    need = jnp.where(thr == INT_MIN, jnp.int32(0), need).astype(F32)
    tril = (lax.broadcasted_iota(I32, (st_, st_), 0) >= lax.broadcasted_iota(I32, (st_, st_), 1)
            ).astype(F32).astype(MXU_DTYPE)
    keep = jnp.zeros((st_, LANES), F32)
    drop = jnp.full((st_, LANES), NEG_BIG, F32)
    f_one = jnp.ones((st_, LANES), F32)

    def tile_bias(j0, n_sub, seen):
        blks = [key_ref[pl.ds(key_off(j0 + s), st_), :] for s in range(n_sub)]
        ranks = [_dot(tril, jnp.where(blk == thr, f_one, keep).astype(MXU_DTYPE)) for blk in blks]
        out = []
        for blk, rank in zip(blks, ranks):
            rank = rank + seen
            tie_bias = jnp.where(blk == thr, jnp.where(rank <= need, keep, drop), drop)
            out.append(jnp.where(blk > thr, keep, tie_bias).astype(MXU_DTYPE))
            seen = rank[st_ - 1:st_, :]
        return out, seen

    eye4 = (lax.broadcasted_iota(I32, (LANES, C_GROUP * LANES), 1) % LANES
            == lax.broadcasted_iota(I32, (LANES, C_GROUP * LANES), 0)).astype(F32).astype(MXU_DTYPE)
    for h in range(C_KV_HEADS):
        half = lo_half if h % 2 == 0 else hi_half
        q4t = jnp.concatenate(
            [jnp.transpose(jnp.where(half, q_ref[0, :, LANES * (4 * (h // 2) + g):LANES * (4 * (h // 2) + g + 1)],
                                     0).astype(F32)) for g in range(C_GROUP)], axis=1)
        rhs_ref[h] = jnp.concatenate([q4t.astype(MXU_DTYPE), eye4], axis=0)
    m_ref[...] = jnp.full(m_ref.shape, NEG_BIG, F32)
    acc_ref[...] = jnp.zeros(acc_ref.shape, F32)

    def attn_tile(j0, n_sub, seen):
        chains = [(h, s) for s in range(n_sub) for h in range(C_KV_HEADS)]
        bias, seen = tile_bias(j0, n_sub, seen)
        scores = []
        for h, s in chains:
            kcol = slice(LANES * (h // 2), LANES * (h // 2 + 1))
            lhs = jnp.concatenate([k_ref[0, pl.ds(key_off(j0 + s), st_), kcol], bias[s]], axis=1)
            scores.append(_dot(lhs, rhs_ref[h]))
        for (h, s), sc in zip(chains, scores):
            sc = sc.astype(MXU_DTYPE)
            m_old = m_ref[h]
            m_new = jnp.maximum(m_old, jnp.max(sc, axis=0, keepdims=True).astype(F32))
            p = jnp.exp2(sc - m_new.astype(MXU_DTYPE))
            vt = vt_ref[0, V_ROWS * h:V_ROWS * (h + 1), pl.ds(key_off(j0 + s), st_)]
            acc_ref[h] = jnp.exp2(m_old - m_new) * acc_ref[h] + _dot(vt, p)
            m_ref[h] = m_new
        return seen

    over_keys(attn_tile, jnp.zeros((1, LANES), F32))

    for h in range(C_KV_HEADS):
        acc = acc_ref[h]
        o = acc[0:HEAD_DIM] / acc[HEAD_DIM:HEAD_DIM + 1]
        for g in range(C_GROUP):
            slot = 8 * (h // 2) + 2 * g + (h % 2)
            ot_ref[HEAD_DIM * slot:HEAD_DIM * (slot + 1), :] = o[:, LANES * g:LANES * (g + 1)]

    o_ref[0] = jnp.transpose(ot_ref[...]).astype(o_ref.dtype)


def _dsa(q, qi, w_t, k, v_t, kk, *, causal, n_keys, ksel):
    bsz, t, _ = q.shape
    nk = k.shape[1]
    qb = QUERY_BLOCK
    blk = lambda w_: pl.BlockSpec((1, qb, w_), lambda bi, qi_: (bi, qi_, 0))
    per_b = lambda a: pl.BlockSpec((1,) + a.shape[1:], lambda bi, qi_: (bi, 0, 0))
    kern = functools.partial(_dsa_kernel, causal=causal, n_keys=n_keys, ksel=ksel)
    return pl.pallas_call(
        kern,
        grid=(bsz, t // qb),
        in_specs=[blk(1024), blk(256),
                  pl.BlockSpec((1, 8, qb), lambda bi, qi_: (bi, 0, qi_)),
                  per_b(k), per_b(v_t), per_b(kk)],
        out_specs=blk(1024),
        out_shape=jax.ShapeDtypeStruct((bsz, t, 1024), MXU_DTYPE),
        scratch_shapes=[pltpu.VMEM((nk, LANES), I32),
                        pltpu.VMEM((32, nk // 32, LANES), I32),
                        pltpu.VMEM((C_KV_HEADS, 2 * LANES, C_GROUP * LANES), MXU_DTYPE),
                        pltpu.VMEM((C_KV_HEADS, 1, C_GROUP * LANES), F32),
                        pltpu.VMEM((C_KV_HEADS, V_ROWS, C_GROUP * LANES), F32),
                        pltpu.VMEM((1024, LANES), F32)],
        compiler_params=pltpu.CompilerParams(
            dimension_semantics=("arbitrary", "arbitrary"), vmem_limit_bytes=VMEM_LIMIT),
        name="dsa",
    )(q, qi, w_t, k, v_t, kk)


def _slot_perm(head_of_slot):
    return np.concatenate([np.arange(HEAD_DIM * j, HEAD_DIM * (j + 1)) for j in head_of_slot])


_AB_Q_PERM = _slot_perm([B_GROUP * (s % 2) + s // 2 for s in range(8)])
_C_Q_PERM = _slot_perm([4 * (2 * (s // 8) + (s % 8) % 2) + (s % 8) // 2 for s in range(16)])


def _rope_tables(pos):
    half = HEAD_DIM // 2
    inv_freq = jnp.exp(-math.log(ROPE_THETA) * jnp.arange(half, dtype=F32) / half)
    ang = pos.astype(F32)[:, None] * inv_freq[None, :]
    cos, sin = jnp.cos(ang), jnp.sin(ang)
    return jnp.concatenate([cos] * 4, axis=1), jnp.concatenate([-sin, sin, -sin, sin], axis=1)


def _value_operand(v):
    bsz, n, _ = v.shape
    vt = jnp.swapaxes(v, 1, 2).reshape(bsz, C_KV_HEADS, HEAD_DIM, n)
    ones = jnp.ones((bsz, C_KV_HEADS, 1, n), v.dtype)
    zeros = jnp.zeros((bsz, C_KV_HEADS, V_ROWS - HEAD_DIM - 1, n), v.dtype)
    return jnp.concatenate([vt, ones, zeros], axis=2).reshape(bsz, C_KV_HEADS * V_ROWS, n)


def _round_up(n, m):
    return (n + m - 1) // m * m


def _trunk(x, pos, caches, wts):
    (ab_w_in, a_ln_g, a_ln_b, a_ws, a_bs, b_sinks, ab_w_out, c_w_in, c_w_out,
     ln1_g, ln1_b, ln2_g, ln2_b, ff_w1, ff_w2) = wts
    bsz, t, d = x.shape
    sample = caches is not None
    cos, sin = _rope_tables(pos)
    tm = min(512, t)
    row = lambda a: a.reshape(1, -1)

    w_in = jnp.concatenate([ab_w_in[0][:, :1024], ab_w_in[0][:, 1024:1536][:, _AB_Q_PERM],
                            ab_w_in[0][:, 1536:]], axis=1).astype(MXU_DTYPE)
    w_out = jnp.concatenate([ab_w_out[0][:512], ab_w_out[0][512:][_AB_Q_PERM]], axis=0).astype(MXU_DTYPE)
    u, va, q, k, v = _ab_in(x.reshape(bsz * t, d), w_in, cos, sin, row(a_ln_g[0]), row(a_ln_b[0]), t=t, tm=tm)
    r3 = lambda a: a.reshape(bsz, t, a.shape[-1])
    u, va, q, k, v = r3(u), r3(va), r3(q), r3(k), r3(v)
    cs = min(A_CHUNK, t)
    ws = a_ws[0][:, :cs, :cs]
    bias = jnp.repeat(a_bs[0][:, :cs].T, HEAD_DIM, axis=1)
    if sample:
        hist_k = caches[0][0].reshape(bsz, B_WINDOW, 128)
        hist_v = caches[1][0].reshape(bsz, B_WINDOW, 128)
        rows = t
    else:
        hist_k, hist_v = k, v
        rows = min(256, t)
    mixed = _ab_mix(b_sinks[0], u, va, q, k, v, hist_k, hist_v, ws, bias,
                    rows=rows, cs=cs, hist_from_self=not sample)
    x = _post(x.reshape(bsz * t, d), mixed.reshape(bsz * t, d), w_out, row(ln1_g[0]), row(ln1_b[0]),
              ff_w1[0].astype(MXU_DTYPE), ff_w2[0].astype(MXU_DTYPE),
              row(ln2_g[0]), row(ln2_b[0]), tm=tm).reshape(bsz, t, d)
    b_k = k.reshape(bsz, t, B_KV_HEADS, HEAD_DIM)
    b_v = v.reshape(bsz, t, B_KV_HEADS, HEAD_DIM)

    cw = c_w_in[0]
    w_in = jnp.concatenate(
        [cw[:, :1024][:, _C_Q_PERM], cw[:, 1024:1792], cw[:, 1792:1856], cw[:, 1792:1856], cw[:, 1856:1860],
         jnp.zeros((d, LANES - IDX_HEADS), F32)], axis=1).astype(MXU_DTYPE)
    w_out = c_w_out[0][_C_Q_PERM].astype(MXU_DTYPE)
    q, k, kb, v, qi, kk, kkb, wi = _c_in(x.reshape(bsz * t, d), w_in, cos, sin, t=t, tm=tm)
    q, k, kb, v, qi, kk, kkb, wi = (r3(a) for a in (q, k, kb, v, qi, kk, kkb, wi))
    ki = kk[:, :, :IDX_DIM]
    w_t = jnp.swapaxes(wi[:, :, :8], 1, 2) * (IDX_DIM ** -0.5 * IDX_HEADS ** -0.5)
    if sample:
        keys_k = jnp.concatenate([caches[2][0].reshape(bsz, -1, 256).astype(MXU_DTYPE), kb], axis=1)
        keys_v = jnp.concatenate([caches[3][0].reshape(bsz, -1, 256), v], axis=1).astype(MXU_DTYPE)
        ci = caches[4][0].astype(MXU_DTYPE)
        keys_i = jnp.concatenate([jnp.concatenate([ci, ci], axis=-1), kkb], axis=1)
        n_keys = keys_k.shape[1]
        pad_k = _round_up(n_keys, SUB_TILE) - n_keys
        keys_k, keys_v, keys_i = (jnp.pad(a, ((0, 0), (0, pad_k), (0, 0))) for a in (keys_k, keys_v, keys_i))
        pad_q = QUERY_BLOCK - t
        padq = lambda a: jnp.pad(a, ((0, 0), (0, pad_q), (0, 0)))
        mixed = _dsa(padq(q), padq(qi), jnp.pad(w_t, ((0, 0), (0, 0), (0, pad_q))), keys_k,
                     _value_operand(keys_v), keys_i,
                     causal=False, n_keys=n_keys, ksel=min(TOPK_MAX, n_keys // 4))[:, :t]
    else:
        mixed = _dsa(q, qi, w_t, kb, _value_operand(v.astype(MXU_DTYPE)), kkb,
                     causal=True, n_keys=t, ksel=min(TOPK_MAX, t // 4))
    x = _post(x.reshape(bsz * t, d), mixed.reshape(bsz * t, d), w_out, row(ln1_g[1]), row(ln1_b[1]),
              ff_w1[1].astype(MXU_DTYPE), ff_w2[1].astype(MXU_DTYPE),
              row(ln2_g[1]), row(ln2_b[1]), tm=tm).reshape(bsz, t, d)
    c_k = k.reshape(bsz, t, C_KV_HEADS, HEAD_DIM)
    c_v = v.reshape(bsz, t, C_KV_HEADS, HEAD_DIM)
    return x, va[None], b_k[None], b_v[None], c_k[None], c_v[None], ki[None]


def kernel(x_prompt, x_sample, cache_b_k, cache_b_v, cache_c_k, cache_c_v, cache_c_idx, ab_w_in, a_ln_g, a_ln_b, a_ws, a_bs, b_sinks, ab_w_out, c_w_in, c_w_out, ln1_g, ln1_b, ln2_g, ln2_b, ff_w1, ff_w2):
    wts = (ab_w_in, a_ln_g, a_ln_b, a_ws, a_bs, b_sinks, ab_w_out, c_w_in, c_w_out,
           ln1_g, ln1_b, ln2_g, ln2_b, ff_w1, ff_w2)
    past_len = cache_c_k.shape[2]
    pos_p = jnp.arange(x_prompt.shape[1], dtype=jnp.int32)
    pos_s = past_len + jnp.arange(x_sample.shape[1], dtype=jnp.int32)
    y_p, _, p_b_k, p_b_v, p_c_k, p_c_v, p_c_idx = _trunk(x_prompt, pos_p, None, wts)
    y_s, s_a_v, s_b_k, s_b_v, s_c_k, s_c_v, s_c_idx = _trunk(
        x_sample, pos_s, (cache_b_k, cache_b_v, cache_c_k, cache_c_v, cache_c_idx), wts)
    return (y_p, y_s, p_b_k[:, :, -B_WINDOW:], p_b_v[:, :, -B_WINDOW:], p_c_k, p_c_v, p_c_idx,
            s_a_v, s_b_k, s_b_v, s_c_k, s_c_v, s_c_idx)
```

```python
import functools
import math

import numpy as np
import jax
import jax.numpy as jnp
from jax import lax
from jax.experimental import pallas as pl
from jax.experimental.pallas import tpu as pltpu

F32 = jnp.float32
I32 = jnp.int32
MXU_DTYPE = jnp.bfloat16

HEAD_DIM = 64
CHUNK = 64
ROPE_THETA = 10000.0
LN_EPS = 1e-5
DEPTH = 2
ALPHA = (2.0 * DEPTH) ** 0.25
A_GROUPS = 8
A_CHUNK = 128
B_KV_HEADS = 2
B_GROUP = 4
B_WINDOW = 128
C_KV_HEADS = 4
C_GROUP = 4
IDX_HEADS = 4
IDX_DIM = 64
TOPK_MAX = 256

LANES = 128
KEY_TILE = 1024
SUB_TILE = 256
QUERY_BLOCK = 128
INT_MIN = -(2 ** 31)
NEG_BIG = -(2.0 ** 100)
LOG2E = math.log2(math.e)
V_ROWS = 80
VMEM_LIMIT = 56 * 1024 * 1024


def _gelu(x):
    c = math.sqrt(2.0 / math.pi)
    return 0.5 * x * (1.0 + jnp.tanh(c * (x + 0.044715 * (x * x * x))))


def _ln(z, g, b):
    mu = jnp.mean(z, axis=-1, keepdims=True)
    d = z - mu
    var = jnp.mean(d * d, axis=-1, keepdims=True)
    return d * lax.rsqrt(var + LN_EPS) * g + b


def _rope2(x, cos, sin):
    lane = lax.broadcasted_iota(I32, (1, LANES), 1)
    first = (lane % HEAD_DIM) < (HEAD_DIM // 2)
    swapped = jnp.where(first, pltpu.roll(x, LANES - HEAD_DIM // 2, 1), pltpu.roll(x, HEAD_DIM // 2, 1))
    return x * cos + swapped * sin


def _dot(a, b):
    return jnp.dot(a, b, preferred_element_type=F32)


def _dot_nt(a, b):
    return lax.dot_general(a, b, (((1,), (1,)), ((), ())), preferred_element_type=F32)


def _ab_in_kernel(x_ref, w_ref, cos_ref, sin_ref, g_ref, b_ref, u_ref, va_ref, q_ref, k_ref, v_ref):
    y = _dot(x_ref[...].astype(MXU_DTYPE), w_ref[...])
    u_ref[...] = _gelu(y[:, 0:512])
    va_ref[...] = _ln(_gelu(y[:, 512:1024]), g_ref[...], b_ref[...])
    cos = cos_ref[...]
    sin = sin_ref[...]
    for c in range(4):
        lo = 1024 + LANES * c
        q_ref[:, LANES * c:LANES * (c + 1)] = (
            _rope2(y[:, lo:lo + LANES], cos, sin) * (HEAD_DIM ** -0.5)).astype(q_ref.dtype)
    k_ref[...] = _rope2(y[:, 1536:1664], cos, sin)
    v_ref[...] = y[:, 1664:1792]


def _ab_in(x2d, w, cos, sin, g, b, *, t, tm):
    rows = x2d.shape[0]
    nt = t // tm
    row_spec = lambda w_: pl.BlockSpec((tm, w_), lambda i: (i, 0))
    full = lambda a: pl.BlockSpec(a.shape, lambda i: (0,) * a.ndim)
    tab = pl.BlockSpec((tm, LANES), lambda i: (i % nt, 0))
    return pl.pallas_call(
        _ab_in_kernel,
        grid=(rows // tm,),
        in_specs=[row_spec(1024), full(w), tab, tab, full(g), full(b)],
        out_specs=[row_spec(512), row_spec(512), row_spec(512), row_spec(128), row_spec(128)],
        out_shape=[
            jax.ShapeDtypeStruct((rows, 512), F32),
            jax.ShapeDtypeStruct((rows, 512), F32),
            jax.ShapeDtypeStruct((rows, 512), MXU_DTYPE),
            jax.ShapeDtypeStruct((rows, 128), F32),
            jax.ShapeDtypeStruct((rows, 128), F32),
        ],
        compiler_params=pltpu.CompilerParams(
            dimension_semantics=("arbitrary",), vmem_limit_bytes=VMEM_LIMIT),
        name="ab_in",
    )(x2d, w, cos, sin, g, b)


def _ab_mix_kernel(sink_ref, u_ref, va_ref, q_ref, k_ref, v_ref, hk_ref, hv_ref,
                   ws_ref, bias_ref, o_ref, *, rows, cs, mask_first):
    t = pl.program_id(1)
    lane = lax.broadcasted_iota(I32, (1, LANES), 1)
    lo_half = lane < HEAD_DIM

    r_i = lax.broadcasted_iota(I32, (cs, cs), 0)
    c_i = lax.broadcasted_iota(I32, (cs, cs), 1)
    tril = r_i >= c_i
    w_tril = [jnp.where(tril, ws_ref[g], 0.0).astype(MXU_DTYPE) for g in range(A_GROUPS)]
    for c in range(rows // cs):
        rs = slice(c * cs, (c + 1) * cs)
        for p in range(A_GROUPS // 2):
            ls = slice(LANES * p, LANES * (p + 1))
            vp = va_ref[0, rs, ls].astype(MXU_DTYPE)
            gate = jnp.where(lo_half, _dot(w_tril[2 * p], vp), _dot(w_tril[2 * p + 1], vp)) + bias_ref[:, ls]
            o_ref[0, rs, ls] = (u_ref[0, rs, ls] * gate).astype(o_ref.dtype)

    kcat = jnp.concatenate([hk_ref[0], k_ref[0]], axis=0).astype(MXU_DTYPE)
    vcat = jnp.concatenate([hv_ref[0], v_ref[0]], axis=0).astype(MXU_DTYPE)
    nwin = B_WINDOW + CHUNK
    head_of_row = lax.broadcasted_iota(I32, (B_GROUP * CHUNK, 1), 0) // CHUNK
    col = lax.broadcasted_iota(I32, (1, nwin), 1)
    for j in range(rows // CHUNK):
        rs = slice(CHUNK * j, CHUNK * (j + 1))
        kwin = kcat[CHUNK * j:CHUNK * j + nwin]
        vwin = vcat[CHUNK * j:CHUNK * j + nwin]
        outs = []
        for h in range(B_KV_HEADS):
            half = lo_half if h == 0 else jnp.logical_not(lo_half)
            q4 = jnp.concatenate(
                [jnp.where(half, q_ref[0, rs, LANES * g:LANES * (g + 1)], 0).astype(MXU_DTYPE)
                 for g in range(B_GROUP)], axis=0)
            s = _dot_nt(q4, kwin)
            if mask_first:
                s = jnp.where(t * rows + CHUNK * j - B_WINDOW + col >= 0, s, -jnp.inf)
            sink = jnp.zeros((B_GROUP * CHUNK, 1), F32)
            for g in range(B_GROUP):
                sink = jnp.where(head_of_row == g, sink_ref[B_GROUP * h + g], sink)
            m = jnp.maximum(jnp.max(s, axis=-1, keepdims=True), sink)
            e = jnp.exp(s - m)
            p = e / (jnp.sum(e, axis=-1, keepdims=True) + jnp.exp(sink - m))
            outs.append(_dot(p.astype(MXU_DTYPE), vwin))
        for g in range(B_GROUP):
            gs = slice(CHUNK * g, CHUNK * (g + 1))
            o_ref[0, rs, 512 + LANES * g:512 + LANES * (g + 1)] = jnp.where(
                lo_half, outs[0][gs], outs[1][gs]).astype(o_ref.dtype)


def _ab_mix(sinks, u, va, q, k, v, hist_k, hist_v, ws, bias, *, rows, cs, hist_from_self):
    bsz, t, _ = u.shape
    full = lambda a: pl.BlockSpec(a.shape, lambda bi, ti: (0,) * a.ndim)
    blk = lambda w_: pl.BlockSpec((1, rows, w_), lambda bi, ti: (bi, ti, 0))
    if hist_from_self:
        per = rows // B_WINDOW
        hist = pl.BlockSpec((1, B_WINDOW, 128), lambda bi, ti: (bi, jnp.maximum(ti * per - 1, 0), 0))
    else:
        hist = pl.BlockSpec((1, B_WINDOW, 128), lambda bi, ti: (bi, 0, 0))
    kern = functools.partial(_ab_mix_kernel, rows=rows, cs=cs, mask_first=hist_from_self)
    return pl.pallas_call(
        kern,
        grid=(bsz, t // rows),
        in_specs=[pl.BlockSpec(memory_space=pltpu.SMEM),
                  blk(512), blk(512), blk(512), blk(128), blk(128), hist, hist,
                  full(ws), full(bias)],
        out_specs=blk(1024),
        out_shape=jax.ShapeDtypeStruct((bsz, t, 1024), MXU_DTYPE),
        compiler_params=pltpu.CompilerParams(
            dimension_semantics=("arbitrary", "arbitrary"), vmem_limit_bytes=VMEM_LIMIT),
        name="ab_mix",
    )(sinks, u, va, q, k, v, hist_k, hist_v, ws, bias)


def _post_kernel(x_ref, a_ref, wo_ref, g1_ref, b1_ref, w1_ref, w2_ref, g2_ref, b2_ref, o_ref, *, ff_tile):
    x = _ln(ALPHA * x_ref[...] + _dot(a_ref[...], wo_ref[...]), g1_ref[...], b1_ref[...])
    xb = x.astype(MXU_DTYPE)
    acc = jnp.zeros(x.shape, F32)
    for c in range(w1_ref.shape[1] // ff_tile):
        h = _dot(xb, w1_ref[:, c * ff_tile:(c + 1) * ff_tile])
        h = jnp.square(jnp.maximum(h, 0.0)).astype(MXU_DTYPE)
        acc = acc + _dot(h, w2_ref[c * ff_tile:(c + 1) * ff_tile, :])
    o_ref[...] = _ln(ALPHA * x + acc, g2_ref[...], b2_ref[...])


def _post(x2d, a2d, wo, g1, b1, w1, w2, g2, b2, *, tm):
    rows, d = x2d.shape
    full = lambda a: pl.BlockSpec(a.shape, lambda i: (0,) * a.ndim, pipeline_mode=pl.Buffered(1))
    row = pl.BlockSpec((tm, d), lambda i: (i, 0))
    return pl.pallas_call(
        functools.partial(_post_kernel, ff_tile=1024),
        grid=(rows // tm,),
        in_specs=[row, row, full(wo), full(g1), full(b1), full(w1), full(w2), full(g2), full(b2)],
        out_specs=row,
        out_shape=jax.ShapeDtypeStruct((rows, d), F32),
        compiler_params=pltpu.CompilerParams(
            dimension_semantics=("arbitrary",), vmem_limit_bytes=VMEM_LIMIT),
        name="post",
    )(x2d, a2d, wo, g1, b1, w1, w2, g2, b2)


def _c_in_kernel(x_ref, w_ref, cos_ref, sin_ref, q_ref, k_ref, kb_ref, v_ref, qi_ref, kk_ref, kkb_ref, wi_ref):
    y = _dot(x_ref[...].astype(MXU_DTYPE), w_ref[...])
    cos = cos_ref[...]
    sin = sin_ref[...]
    for c in range(8):
        q_ref[:, LANES * c:LANES * (c + 1)] = (
            _rope2(y[:, LANES * c:LANES * (c + 1)], cos, sin) * (LOG2E * HEAD_DIM ** -0.5)).astype(q_ref.dtype)
    for c in range(2):
        kr = _rope2(y[:, 1024 + LANES * c:1024 + LANES * (c + 1)], cos, sin)
        k_ref[:, LANES * c:LANES * (c + 1)] = kr
        kb_ref[:, LANES * c:LANES * (c + 1)] = kr.astype(kb_ref.dtype)
    v_ref[...] = y[:, 1280:1536]
    for c in range(2):
        qi_ref[:, LANES * c:LANES * (c + 1)] = _rope2(
            y[:, 1536 + LANES * c:1536 + LANES * (c + 1)], cos, sin).astype(qi_ref.dtype)
    kk = _rope2(y[:, 1792:1920], cos, sin)
    kk_ref[...] = kk
    kkb_ref[...] = kk.astype(kkb_ref.dtype)
    wi_ref[...] = y[:, 1920:2048]


def _c_in(x2d, w, cos, sin, *, t, tm):
    rows = x2d.shape[0]
    nt = t // tm
    row_spec = lambda w_: pl.BlockSpec((tm, w_), lambda i: (i, 0))
    full = lambda a: pl.BlockSpec(a.shape, lambda i: (0,) * a.ndim)
    tab = pl.BlockSpec((tm, LANES), lambda i: (i % nt, 0))
    widths = [(1024, MXU_DTYPE), (256, F32), (256, MXU_DTYPE), (256, F32), (256, MXU_DTYPE),
              (128, F32), (128, MXU_DTYPE), (128, F32)]
    return pl.pallas_call(
        _c_in_kernel,
        grid=(rows // tm,),
        in_specs=[row_spec(1024), full(w), tab, tab],
        out_specs=[row_spec(w_) for w_, _ in widths],
        out_shape=[jax.ShapeDtypeStruct((rows, w_), dt) for w_, dt in widths],
        compiler_params=pltpu.CompilerParams(
            dimension_semantics=("arbitrary",), vmem_limit_bytes=VMEM_LIMIT),
        name="c_in",
    )(x2d, w, cos, sin)


def _bit_transpose32(words):
    a = list(words)
    j, m = 16, 0x0000FFFF
    while j:
        k = 0
        while k < 32:
            t = (a[k] ^ lax.shift_right_logical(a[k + j], jnp.int32(j))) & m
            a[k] = a[k] ^ t
            a[k + j] = a[k + j] ^ lax.shift_left(t, jnp.int32(j))
            k = (k + j + 1) & ~j
        j >>= 1
        m = (m ^ (m << j)) & 0xFFFFFFFF
    return a


def _dsa_kernel(q_ref, qi_ref, w_ref, qi_next_ref, w_next_ref, k_ref, vt_ref, kk_ref, o_ref,
                key_ref, planes_ref, rhs_ref, m_ref, acc_ref, ot_ref,
                *, causal, n_keys, ksel):
    tk, st_ = KEY_TILE, SUB_TILE
    i = pl.program_id(1)
    lane = lax.broadcasted_iota(I32, (1, LANES), 1)
    lo_half = lane < HEAD_DIM
    hi_half = jnp.logical_not(lo_half)
    nk = key_ref.shape[0] // 2

    def admissible(blk):
        if causal:
            lim = CHUNK * (2 * blk + jnp.where(lo_half, jnp.int32(0), jnp.int32(1)) + 1)
            hi = QUERY_BLOCK * (blk + 1)
        else:
            lim = jnp.full((1, LANES), n_keys, I32)
            hi = n_keys
        return lim, (hi + st_ - 1) // st_

    i_next = jnp.minimum(i + 1, pl.num_programs(1) - 1)
    limit, n_sub_tiles = admissible(i)
    limit_next, n_sub_next = admissible(i_next)
    key_base = (i % 2) * nk if causal else 0
    key_base_next = nk - key_base
    sub = lax.broadcasted_iota(I32, (st_, LANES), 0)
    per_tile = tk // st_
    n_full = n_sub_tiles // per_tile
    n_rem = n_sub_tiles - n_full * per_tile

    def over_keys(body, carry):
        carry = lax.fori_loop(0, n_full, lambda t, c: body(t * per_tile, per_tile, c), carry)
        return lax.fori_loop(0, n_rem, lambda r, c: body(n_full * per_tile + r, 1, c), carry)

    def key_off(j):
        return pl.multiple_of(j * st_, st_)

    def plane_row(j):
        return pl.multiple_of(j * (st_ // 32), 8)

    def scorer(qi_blk_ref, w_blk_ref, lim, base):
        qi_blk = qi_blk_ref[0]
        qi4 = jnp.concatenate(
            [jnp.where(lo_half if h % 2 == 0 else hi_half, qi_blk[:, LANES * (h // 2):LANES * (h // 2 + 1)], 0)
             for h in range(IDX_HEADS)], axis=0)
        w = w_blk_ref[0]

        def score_tile(j0, n_sub):
            for s in range(n_sub):
                off = key_off(j0 + s)
                sc4 = _dot_nt(kk_ref[0, pl.ds(off, st_), :], qi4)
                sc = jnp.maximum(sc4[:, 0:LANES], 0.0) * w[0:1]
                for h in range(1, IDX_HEADS):
                    sc = sc + jnp.maximum(sc4[:, LANES * h:LANES * (h + 1)], 0.0) * w[h:h + 1]
                bits = pltpu.bitcast(sc, I32)
                key = jnp.where(bits < 0, bits ^ 0x7FFFFFFF, bits)
                key = jnp.where(sub + off < lim, key, INT_MIN)
                key_ref[pl.ds(pl.multiple_of(base + off, st_), st_), :] = key
                ukey = key ^ INT_MIN
                planes = _bit_transpose32([ukey[8 * r:8 * (r + 1)] for r in range(32)])
                row = plane_row(j0 + s)
                for b in range(32):
                    planes_ref[b, pl.ds(row, 8), :] = planes[b]

        return score_tile

    @pl.when(jnp.logical_and(pl.program_id(0) == 0, i == 0))
    def _():
        planes_ref[...] = jnp.zeros(planes_ref.shape, I32)

    def score_own_block():
        score_own = scorer(qi_ref, w_ref, limit, key_base)
        over_keys(lambda j0, n_sub, c: (score_own(j0, n_sub), c)[1], 0)

    if causal:
        pl.when(i == 0)(score_own_block)
    else:
        score_own_block()

    plane_rows = planes_ref.shape[1]
    group = lax.broadcasted_iota(I32, (plane_rows, LANES), 0) // (st_ // 32)
    alive0 = jnp.where(group < n_sub_tiles, jnp.int32(-1), jnp.int32(0))

    def bit_step(b, state):
        alive, above, ubits = state
        plane = planes_ref[b]
        hits = lax.population_count(alive & plane)
        cnt = jnp.sum(jnp.sum(hits.reshape(plane_rows // 8, 8, LANES), axis=0), axis=0, keepdims=True)
        take = above + cnt >= ksel
        ubits = ubits | jnp.where(take, lax.shift_left(jnp.int32(1), jnp.int32(31) - b), jnp.int32(0))
        above = jnp.where(take, above, above + cnt)
        alive = alive & (plane ^ jnp.where(take, jnp.int32(0), jnp.int32(-1)))
        return alive, above, ubits

    zero_row = jnp.zeros((1, LANES), I32)
    _, above, ubits = lax.fori_loop(0, 32, bit_step, (alive0, zero_row, zero_row))
    thr = ubits ^ INT_MIN

    need = ksel - above
    need = jnp.where(thr == INT_MIN, jnp.int32(0), need).astype(F32)
    tril = (lax.broadcasted_iota(I32, (st_, st_), 0) >= lax.broadcasted_iota(I32, (st_, st_), 1)
            ).astype(F32).astype(MXU_DTYPE)
    keep = jnp.zeros((st_, LANES), F32)
    drop = jnp.full((st_, LANES), NEG_BIG, F32)
    f_one = jnp.ones((st_, LANES), F32)

    def tile_bias(j0, n_sub, seen):
        blks = [key_ref[pl.ds(pl.multiple_of(key_base + key_off(j0 + s), st_), st_), :] for s in range(n_sub)]
        ranks = [_dot(tril, jnp.where(blk == thr, f_one, keep).astype(MXU_DTYPE)) for blk in blks]
        out = []
        for blk, rank in zip(blks, ranks):
            rank = rank + seen
            tie_bias = jnp.where(blk == thr, jnp.where(rank <= need, keep, drop), drop)
            out.append(jnp.where(blk > thr, keep, tie_bias).astype(MXU_DTYPE))
            seen = rank[st_ - 1:st_, :]
        return out, seen

    eye4 = (lax.broadcasted_iota(I32, (LANES, C_GROUP * LANES), 1) % LANES
            == lax.broadcasted_iota(I32, (LANES, C_GROUP * LANES), 0)).astype(F32).astype(MXU_DTYPE)
    for h in range(C_KV_HEADS):
        half = lo_half if h % 2 == 0 else hi_half
        q4t = jnp.concatenate(
            [jnp.transpose(jnp.where(half, q_ref[0, :, LANES * (4 * (h // 2) + g):LANES * (4 * (h // 2) + g + 1)],
                                     0).astype(F32)) for g in range(C_GROUP)], axis=1)
        rhs_ref[h] = jnp.concatenate([q4t.astype(MXU_DTYPE), eye4], axis=0)
    m_ref[...] = jnp.full(m_ref.shape, NEG_BIG, F32)
    acc_ref[...] = jnp.zeros(acc_ref.shape, F32)

    score_next = scorer(qi_next_ref, w_next_ref, limit_next, key_base_next) if causal else None

    def attn_tile(j0, n_sub, seen):
        chains = [(h, s) for s in range(n_sub) for h in range(C_KV_HEADS)]
        bias, seen = tile_bias(j0, n_sub, seen)
        if causal:
            score_next(j0, n_sub)
        scores = []
        for h, s in chains:
            kcol = slice(LANES * (h // 2), LANES * (h // 2 + 1))
            lhs = jnp.concatenate([k_ref[0, pl.ds(key_off(j0 + s), st_), kcol], bias[s]], axis=1)
            scores.append(_dot(lhs, rhs_ref[h]))
        for (h, s), sc in zip(chains, scores):
            sc = sc.astype(MXU_DTYPE)
            m_old = m_ref[h]
            m_new = jnp.maximum(m_old, jnp.max(sc, axis=0, keepdims=True).astype(F32))
            p = jnp.exp2(sc - m_new.astype(MXU_DTYPE))
            vt = vt_ref[0, V_ROWS * h:V_ROWS * (h + 1), pl.ds(key_off(j0 + s), st_)]
            acc_ref[h] = jnp.exp2(m_old - m_new) * acc_ref[h] + _dot(vt, p)
            m_ref[h] = m_new
        return seen

    over_keys(attn_tile, jnp.zeros((1, LANES), F32))
    if causal:
        lax.fori_loop(n_sub_tiles, n_sub_next, lambda j, c: (score_next(j, 1), c)[1], 0)

    for h in range(C_KV_HEADS):
        acc = acc_ref[h]
        o = acc[0:HEAD_DIM] / acc[HEAD_DIM:HEAD_DIM + 1]
        for g in range(C_GROUP):
            slot = 8 * (h // 2) + 2 * g + (h % 2)
            ot_ref[HEAD_DIM * slot:HEAD_DIM * (slot + 1), :] = o[:, LANES * g:LANES * (g + 1)]

    o_ref[0] = jnp.transpose(ot_ref[...]).astype(o_ref.dtype)


def _dsa(q, qi, w_t, k, v_t, kk, *, causal, n_keys, ksel):
    bsz, t, _ = q.shape
    nk = k.shape[1]
    qb = QUERY_BLOCK
    nqb = t // qb
    nxt = lambda qi_: jnp.minimum(qi_ + 1, nqb - 1)
    blk = lambda w_: pl.BlockSpec((1, qb, w_), lambda bi, qi_: (bi, qi_, 0))
    per_b = lambda a: pl.BlockSpec((1,) + a.shape[1:], lambda bi, qi_: (bi, 0, 0))
    kern = functools.partial(_dsa_kernel, causal=causal, n_keys=n_keys, ksel=ksel)
    return pl.pallas_call(
        kern,
        grid=(bsz, nqb),
        in_specs=[blk(1024), blk(256),
                  pl.BlockSpec((1, 8, qb), lambda bi, qi_: (bi, 0, qi_)),
                  pl.BlockSpec((1, qb, 256), lambda bi, qi_: (bi, nxt(qi_), 0)),
                  pl.BlockSpec((1, 8, qb), lambda bi, qi_: (bi, 0, nxt(qi_))),
                  per_b(k), per_b(v_t), per_b(kk)],
        out_specs=blk(1024),
        out_shape=jax.ShapeDtypeStruct((bsz, t, 1024), MXU_DTYPE),
        scratch_shapes=[pltpu.VMEM((2 * nk, LANES), I32),
                        pltpu.VMEM((32, nk // 32, LANES), I32),
                        pltpu.VMEM((C_KV_HEADS, 2 * LANES, C_GROUP * LANES), MXU_DTYPE),
                        pltpu.VMEM((C_KV_HEADS, 1, C_GROUP * LANES), F32),
                        pltpu.VMEM((C_KV_HEADS, V_ROWS, C_GROUP * LANES), F32),
                        pltpu.VMEM((1024, LANES), F32)],
        compiler_params=pltpu.CompilerParams(
            dimension_semantics=("arbitrary", "arbitrary"), vmem_limit_bytes=VMEM_LIMIT),
        name="dsa",
    )(q, qi, w_t, qi, w_t, k, v_t, kk)


def _slot_perm(head_of_slot):
    return np.concatenate([np.arange(HEAD_DIM * j, HEAD_DIM * (j + 1)) for j in head_of_slot])


_AB_Q_PERM = _slot_perm([B_GROUP * (s % 2) + s // 2 for s in range(8)])
_C_Q_PERM = _slot_perm([4 * (2 * (s // 8) + (s % 8) % 2) + (s % 8) // 2 for s in range(16)])


def _rope_tables(pos):
    half = HEAD_DIM // 2
    inv_freq = jnp.exp(-math.log(ROPE_THETA) * jnp.arange(half, dtype=F32) / half)
    ang = pos.astype(F32)[:, None] * inv_freq[None, :]
    cos, sin = jnp.cos(ang), jnp.sin(ang)
    return jnp.concatenate([cos] * 4, axis=1), jnp.concatenate([-sin, sin, -sin, sin], axis=1)


def _value_operand(v):
    bsz, n, _ = v.shape
    vt = jnp.swapaxes(v, 1, 2).reshape(bsz, C_KV_HEADS, HEAD_DIM, n)
    ones = jnp.ones((bsz, C_KV_HEADS, 1, n), v.dtype)
    zeros = jnp.zeros((bsz, C_KV_HEADS, V_ROWS - HEAD_DIM - 1, n), v.dtype)
    return jnp.concatenate([vt, ones, zeros], axis=2).reshape(bsz, C_KV_HEADS * V_ROWS, n)


def _round_up(n, m):
    return (n + m - 1) // m * m


def _trunk(x, pos, caches, wts):
    (ab_w_in, a_ln_g, a_ln_b, a_ws, a_bs, b_sinks, ab_w_out, c_w_in, c_w_out,
     ln1_g, ln1_b, ln2_g, ln2_b, ff_w1, ff_w2) = wts
    bsz, t, d = x.shape
    sample = caches is not None
    cos, sin = _rope_tables(pos)
    tm = min(512, t)
    row = lambda a: a.reshape(1, -1)

    w_in = jnp.concatenate([ab_w_in[0][:, :1024], ab_w_in[0][:, 1024:1536][:, _AB_Q_PERM],
                            ab_w_in[0][:, 1536:]], axis=1).astype(MXU_DTYPE)
    w_out = jnp.concatenate([ab_w_out[0][:512], ab_w_out[0][512:][_AB_Q_PERM]], axis=0).astype(MXU_DTYPE)
    u, va, q, k, v = _ab_in(x.reshape(bsz * t, d), w_in, cos, sin, row(a_ln_g[0]), row(a_ln_b[0]), t=t, tm=tm)
    r3 = lambda a: a.reshape(bsz, t, a.shape[-1])
    u, va, q, k, v = r3(u), r3(va), r3(q), r3(k), r3(v)
    cs = min(A_CHUNK, t)
    ws = a_ws[0][:, :cs, :cs]
    bias = jnp.repeat(a_bs[0][:, :cs].T, HEAD_DIM, axis=1)
    if sample:
        hist_k = caches[0][0].reshape(bsz, B_WINDOW, 128)
        hist_v = caches[1][0].reshape(bsz, B_WINDOW, 128)
        rows = t
    else:
        hist_k, hist_v = k, v
        rows = min(256, t)
    mixed = _ab_mix(b_sinks[0], u, va, q, k, v, hist_k, hist_v, ws, bias,
                    rows=rows, cs=cs, hist_from_self=not sample)
    x = _post(x.reshape(bsz * t, d), mixed.reshape(bsz * t, d), w_out, row(ln1_g[0]), row(ln1_b[0]),
              ff_w1[0].astype(MXU_DTYPE), ff_w2[0].astype(MXU_DTYPE),
              row(ln2_g[0]), row(ln2_b[0]), tm=tm).reshape(bsz, t, d)
    b_k = k.reshape(bsz, t, B_KV_HEADS, HEAD_DIM)
    b_v = v.reshape(bsz, t, B_KV_HEADS, HEAD_DIM)

    cw = c_w_in[0]
    w_in = jnp.concatenate(
        [cw[:, :1024][:, _C_Q_PERM], cw[:, 1024:1792], cw[:, 1792:1856], cw[:, 1792:1856], cw[:, 1856:1860],
         jnp.zeros((d, LANES - IDX_HEADS), F32)], axis=1).astype(MXU_DTYPE)
    w_out = c_w_out[0][_C_Q_PERM].astype(MXU_DTYPE)
    q, k, kb, v, qi, kk, kkb, wi = _c_in(x.reshape(bsz * t, d), w_in, cos, sin, t=t, tm=tm)
    q, k, kb, v, qi, kk, kkb, wi = (r3(a) for a in (q, k, kb, v, qi, kk, kkb, wi))
    ki = kk[:, :, :IDX_DIM]
    w_t = jnp.swapaxes(wi[:, :, :8], 1, 2) * (IDX_DIM ** -0.5 * IDX_HEADS ** -0.5)
    if sample:
        keys_k = jnp.concatenate([caches[2][0].reshape(bsz, -1, 256).astype(MXU_DTYPE), kb], axis=1)
        keys_v = jnp.concatenate([caches[3][0].reshape(bsz, -1, 256), v], axis=1).astype(MXU_DTYPE)
        ci = caches[4][0].astype(MXU_DTYPE)
        keys_i = jnp.concatenate([jnp.concatenate([ci, ci], axis=-1), kkb], axis=1)
        n_keys = keys_k.shape[1]
        pad_k = _round_up(n_keys, SUB_TILE) - n_keys
        keys_k, keys_v, keys_i = (jnp.pad(a, ((0, 0), (0, pad_k), (0, 0))) for a in (keys_k, keys_v, keys_i))
        pad_q = QUERY_BLOCK - t
        padq = lambda a: jnp.pad(a, ((0, 0), (0, pad_q), (0, 0)))
        mixed = _dsa(padq(q), padq(qi), jnp.pad(w_t, ((0, 0), (0, 0), (0, pad_q))), keys_k,
                     _value_operand(keys_v), keys_i,
                     causal=False, n_keys=n_keys, ksel=min(TOPK_MAX, n_keys // 4))[:, :t]
    else:
        mixed = _dsa(q, qi, w_t, kb, _value_operand(v.astype(MXU_DTYPE)), kkb,
                     causal=True, n_keys=t, ksel=min(TOPK_MAX, t // 4))
    x = _post(x.reshape(bsz * t, d), mixed.reshape(bsz * t, d), w_out, row(ln1_g[1]), row(ln1_b[1]),
              ff_w1[1].astype(MXU_DTYPE), ff_w2[1].astype(MXU_DTYPE),
              row(ln2_g[1]), row(ln2_b[1]), tm=tm).reshape(bsz, t, d)
    c_k = k.reshape(bsz, t, C_KV_HEADS, HEAD_DIM)
    c_v = v.reshape(bsz, t, C_KV_HEADS, HEAD_DIM)
    return x, va[None], b_k[None], b_v[None], c_k[None], c_v[None], ki[None]


def kernel(x_prompt, x_sample, cache_b_k, cache_b_v, cache_c_k, cache_c_v, cache_c_idx, ab_w_in, a_ln_g, a_ln_b, a_ws, a_bs, b_sinks, ab_w_out, c_w_in, c_w_out, ln1_g, ln1_b, ln2_g, ln2_b, ff_w1, ff_w2):
    wts = (ab_w_in, a_ln_g, a_ln_b, a_ws, a_bs, b_sinks, ab_w_out, c_w_in, c_w_out,
           ln1_g, ln1_b, ln2_g, ln2_b, ff_w1, ff_w2)
    past_len = cache_c_k.shape[2]
    pos_p = jnp.arange(x_prompt.shape[1], dtype=jnp.int32)
    pos_s = past_len + jnp.arange(x_sample.shape[1], dtype=jnp.int32)
    y_p, _, p_b_k, p_b_v, p_c_k, p_c_v, p_c_idx = _trunk(x_prompt, pos_p, None, wts)
    y_s, s_a_v, s_b_k, s_b_v, s_c_k, s_c_v, s_c_idx = _trunk(
        x_sample, pos_s, (cache_b_k, cache_b_v, cache_c_k, cache_c_v, cache_c_idx), wts)
    return (y_p, y_s, p_b_k[:, :, -B_WINDOW:], p_b_v[:, :, -B_WINDOW:], p_c_k, p_c_v, p_c_idx,
            s_a_v, s_b_k, s_b_v, s_c_k, s_c_v, s_c_idx)
```

```python
import functools
import math

import numpy as np
import jax
import jax.numpy as jnp
from jax import lax
from jax.experimental import pallas as pl
from jax.experimental.pallas import tpu as pltpu

F32 = jnp.float32
I32 = jnp.int32
MXU_DTYPE = jnp.bfloat16

HEAD_DIM = 64
CHUNK = 64
ROPE_THETA = 10000.0
LN_EPS = 1e-5
DEPTH = 2
ALPHA = (2.0 * DEPTH) ** 0.25
A_GROUPS = 8
A_CHUNK = 128
B_KV_HEADS = 2
B_GROUP = 4
B_WINDOW = 128
C_KV_HEADS = 4
C_GROUP = 4
IDX_HEADS = 4
IDX_DIM = 64
TOPK_MAX = 256

LANES = 128
KEY_TILE = 1024
SUB_TILE = 256
QUERY_BLOCK = 128
INT_MIN = -(2 ** 31)
NEG_BIG = -(2.0 ** 100)
LOG2E = math.log2(math.e)
V_ROWS = 80
VMEM_LIMIT = 56 * 1024 * 1024


def _gelu(x):
    c = math.sqrt(2.0 / math.pi)
    return 0.5 * x * (1.0 + jnp.tanh(c * (x + 0.044715 * (x * x * x))))


def _ln(z, g, b):
    mu = jnp.mean(z, axis=-1, keepdims=True)
    d = z - mu
    var = jnp.mean(d * d, axis=-1, keepdims=True)
    return d * lax.rsqrt(var + LN_EPS) * g + b


def _rope2(x, cos, sin):
    lane = lax.broadcasted_iota(I32, (1, LANES), 1)
    first = (lane % HEAD_DIM) < (HEAD_DIM // 2)
    swapped = jnp.where(first, pltpu.roll(x, LANES - HEAD_DIM // 2, 1), pltpu.roll(x, HEAD_DIM // 2, 1))
    return x * cos + swapped * sin


def _dot(a, b):
    return jnp.dot(a, b, preferred_element_type=F32)


def _dot_nt(a, b):
    return lax.dot_general(a, b, (((1,), (1,)), ((), ())), preferred_element_type=F32)


def _ab_in_kernel(x_ref, w_ref, cos_ref, sin_ref, g_ref, b_ref, u_ref, va_ref, q_ref, k_ref, v_ref):
    y = _dot(x_ref[...].astype(MXU_DTYPE), w_ref[...])
    u_ref[...] = _gelu(y[:, 0:512])
    va_ref[...] = _ln(_gelu(y[:, 512:1024]), g_ref[...], b_ref[...])
    cos = cos_ref[...]
    sin = sin_ref[...]
    for c in range(4):
        lo = 1024 + LANES * c
        q_ref[:, LANES * c:LANES * (c + 1)] = (
            _rope2(y[:, lo:lo + LANES], cos, sin) * (HEAD_DIM ** -0.5)).astype(q_ref.dtype)
    k_ref[...] = _rope2(y[:, 1536:1664], cos, sin)
    v_ref[...] = y[:, 1664:1792]


def _ab_in(x2d, w, cos, sin, g, b, *, t, tm):
    rows = x2d.shape[0]
    nt = t // tm
    row_spec = lambda w_: pl.BlockSpec((tm, w_), lambda i: (i, 0))
    full = lambda a: pl.BlockSpec(a.shape, lambda i: (0,) * a.ndim)
    tab = pl.BlockSpec((tm, LANES), lambda i: (i % nt, 0))
    return pl.pallas_call(
        _ab_in_kernel,
        grid=(rows // tm,),
        in_specs=[row_spec(1024), full(w), tab, tab, full(g), full(b)],
        out_specs=[row_spec(512), row_spec(512), row_spec(512), row_spec(128), row_spec(128)],
        out_shape=[
            jax.ShapeDtypeStruct((rows, 512), F32),
            jax.ShapeDtypeStruct((rows, 512), F32),
            jax.ShapeDtypeStruct((rows, 512), MXU_DTYPE),
            jax.ShapeDtypeStruct((rows, 128), F32),
            jax.ShapeDtypeStruct((rows, 128), F32),
        ],
        compiler_params=pltpu.CompilerParams(
            dimension_semantics=("arbitrary",), vmem_limit_bytes=VMEM_LIMIT),
        name="ab_in",
    )(x2d, w, cos, sin, g, b)


def _ab_mix_kernel(sink_ref, u_ref, va_ref, q_ref, k_ref, v_ref, hk_ref, hv_ref,
                   ws_ref, bias_ref, o_ref, *, rows, cs, mask_first):
    t = pl.program_id(1)
    lane = lax.broadcasted_iota(I32, (1, LANES), 1)
    lo_half = lane < HEAD_DIM

    r_i = lax.broadcasted_iota(I32, (cs, cs), 0)
    c_i = lax.broadcasted_iota(I32, (cs, cs), 1)
    tril = r_i >= c_i
    w_tril = [jnp.where(tril, ws_ref[g], 0.0).astype(MXU_DTYPE) for g in range(A_GROUPS)]
    for c in range(rows // cs):
        rs = slice(c * cs, (c + 1) * cs)
        for p in range(A_GROUPS // 2):
            ls = slice(LANES * p, LANES * (p + 1))
            vp = va_ref[0, rs, ls].astype(MXU_DTYPE)
            gate = jnp.where(lo_half, _dot(w_tril[2 * p], vp), _dot(w_tril[2 * p + 1], vp)) + bias_ref[:, ls]
            o_ref[0, rs, ls] = (u_ref[0, rs, ls] * gate).astype(o_ref.dtype)

    kcat = jnp.concatenate([hk_ref[0], k_ref[0]], axis=0).astype(MXU_DTYPE)
    vcat = jnp.concatenate([hv_ref[0], v_ref[0]], axis=0).astype(MXU_DTYPE)
    nwin = B_WINDOW + CHUNK
    head_of_row = lax.broadcasted_iota(I32, (B_GROUP * CHUNK, 1), 0) // CHUNK
    col = lax.broadcasted_iota(I32, (1, nwin), 1)
    for j in range(rows // CHUNK):
        rs = slice(CHUNK * j, CHUNK * (j + 1))
        kwin = kcat[CHUNK * j:CHUNK * j + nwin]
        vwin = vcat[CHUNK * j:CHUNK * j + nwin]
        outs = []
        for h in range(B_KV_HEADS):
            half = lo_half if h == 0 else jnp.logical_not(lo_half)
            q4 = jnp.concatenate(
                [jnp.where(half, q_ref[0, rs, LANES * g:LANES * (g + 1)], 0).astype(MXU_DTYPE)
                 for g in range(B_GROUP)], axis=0)
            s = _dot_nt(q4, kwin)
            if mask_first:
                s = jnp.where(t * rows + CHUNK * j - B_WINDOW + col >= 0, s, -jnp.inf)
            sink = jnp.zeros((B_GROUP * CHUNK, 1), F32)
            for g in range(B_GROUP):
                sink = jnp.where(head_of_row == g, sink_ref[B_GROUP * h + g], sink)
            m = jnp.maximum(jnp.max(s, axis=-1, keepdims=True), sink)
            e = jnp.exp(s - m)
            p = e / (jnp.sum(e, axis=-1, keepdims=True) + jnp.exp(sink - m))
            outs.append(_dot(p.astype(MXU_DTYPE), vwin))
        for g in range(B_GROUP):
            gs = slice(CHUNK * g, CHUNK * (g + 1))
            o_ref[0, rs, 512 + LANES * g:512 + LANES * (g + 1)] = jnp.where(
                lo_half, outs[0][gs], outs[1][gs]).astype(o_ref.dtype)


def _ab_mix(sinks, u, va, q, k, v, hist_k, hist_v, ws, bias, *, rows, cs, hist_from_self):
    bsz, t, _ = u.shape
    full = lambda a: pl.BlockSpec(a.shape, lambda bi, ti: (0,) * a.ndim)
    blk = lambda w_: pl.BlockSpec((1, rows, w_), lambda bi, ti: (bi, ti, 0))
    if hist_from_self:
        per = rows // B_WINDOW
        hist = pl.BlockSpec((1, B_WINDOW, 128), lambda bi, ti: (bi, jnp.maximum(ti * per - 1, 0), 0))
    else:
        hist = pl.BlockSpec((1, B_WINDOW, 128), lambda bi, ti: (bi, 0, 0))
    kern = functools.partial(_ab_mix_kernel, rows=rows, cs=cs, mask_first=hist_from_self)
    return pl.pallas_call(
        kern,
        grid=(bsz, t // rows),
        in_specs=[pl.BlockSpec(memory_space=pltpu.SMEM),
                  blk(512), blk(512), blk(512), blk(128), blk(128), hist, hist,
                  full(ws), full(bias)],
        out_specs=blk(1024),
        out_shape=jax.ShapeDtypeStruct((bsz, t, 1024), MXU_DTYPE),
        compiler_params=pltpu.CompilerParams(
            dimension_semantics=("arbitrary", "arbitrary"), vmem_limit_bytes=VMEM_LIMIT),
        name="ab_mix",
    )(sinks, u, va, q, k, v, hist_k, hist_v, ws, bias)


def _post_kernel(x_ref, a_ref, wo_ref, g1_ref, b1_ref, w1_ref, w2_ref, g2_ref, b2_ref, o_ref, *, ff_tile):
    x = _ln(ALPHA * x_ref[...] + _dot(a_ref[...], wo_ref[...]), g1_ref[...], b1_ref[...])
    xb = x.astype(MXU_DTYPE)
    acc = jnp.zeros(x.shape, F32)
    for c in range(w1_ref.shape[1] // ff_tile):
        h = _dot(xb, w1_ref[:, c * ff_tile:(c + 1) * ff_tile])
        h = jnp.square(jnp.maximum(h, 0.0)).astype(MXU_DTYPE)
        acc = acc + _dot(h, w2_ref[c * ff_tile:(c + 1) * ff_tile, :])
    o_ref[...] = _ln(ALPHA * x + acc, g2_ref[...], b2_ref[...])


def _post(x2d, a2d, wo, g1, b1, w1, w2, g2, b2, *, tm):
    rows, d = x2d.shape
    full = lambda a: pl.BlockSpec(a.shape, lambda i: (0,) * a.ndim, pipeline_mode=pl.Buffered(1))
    row = pl.BlockSpec((tm, d), lambda i: (i, 0))
    return pl.pallas_call(
        functools.partial(_post_kernel, ff_tile=1024),
        grid=(rows // tm,),
        in_specs=[row, row, full(wo), full(g1), full(b1), full(w1), full(w2), full(g2), full(b2)],
        out_specs=row,
        out_shape=jax.ShapeDtypeStruct((rows, d), F32),
        compiler_params=pltpu.CompilerParams(
            dimension_semantics=("arbitrary",), vmem_limit_bytes=VMEM_LIMIT),
        name="post",
    )(x2d, a2d, wo, g1, b1, w1, w2, g2, b2)


def _c_in_kernel(x_ref, w_ref, cos_ref, sin_ref, q_ref, k_ref, kb_ref, v_ref, qi_ref, kk_ref, kkb_ref, wi_ref):
    y = _dot(x_ref[...].astype(MXU_DTYPE), w_ref[...])
    cos = cos_ref[...]
    sin = sin_ref[...]
    for c in range(8):
        q_ref[:, LANES * c:LANES * (c + 1)] = (
            _rope2(y[:, LANES * c:LANES * (c + 1)], cos, sin) * (LOG2E * HEAD_DIM ** -0.5)).astype(q_ref.dtype)
    for c in range(2):
        kr = _rope2(y[:, 1024 + LANES * c:1024 + LANES * (c + 1)], cos, sin)
        k_ref[:, LANES * c:LANES * (c + 1)] = kr
        kb_ref[:, LANES * c:LANES * (c + 1)] = kr.astype(kb_ref.dtype)
    v_ref[...] = y[:, 1280:1536]
    for c in range(2):
        qi_ref[:, LANES * c:LANES * (c + 1)] = _rope2(
            y[:, 1536 + LANES * c:1536 + LANES * (c + 1)], cos, sin).astype(qi_ref.dtype)
    kk = _rope2(y[:, 1792:1920], cos, sin)
    kk_ref[...] = kk
    kkb_ref[...] = kk.astype(kkb_ref.dtype)
    wi_ref[...] = y[:, 1920:2048]


def _c_in(x2d, w, cos, sin, *, t, tm):
    rows = x2d.shape[0]
    nt = t // tm
    row_spec = lambda w_: pl.BlockSpec((tm, w_), lambda i: (i, 0))
    full = lambda a: pl.BlockSpec(a.shape, lambda i: (0,) * a.ndim)
    tab = pl.BlockSpec((tm, LANES), lambda i: (i % nt, 0))
    widths = [(1024, MXU_DTYPE), (256, F32), (256, MXU_DTYPE), (256, F32), (256, MXU_DTYPE),
              (128, F32), (128, MXU_DTYPE), (128, F32)]
    return pl.pallas_call(
        _c_in_kernel,
        grid=(rows // tm,),
        in_specs=[row_spec(1024), full(w), tab, tab],
        out_specs=[row_spec(w_) for w_, _ in widths],
        out_shape=[jax.ShapeDtypeStruct((rows, w_), dt) for w_, dt in widths],
        compiler_params=pltpu.CompilerParams(
            dimension_semantics=("arbitrary",), vmem_limit_bytes=VMEM_LIMIT),
        name="c_in",
    )(x2d, w, cos, sin)


def _bit_transpose32(words):
    a = list(words)
    j, m = 16, 0x0000FFFF
    while j:
        k = 0
        while k < 32:
            t = (a[k] ^ lax.shift_right_logical(a[k + j], jnp.int32(j))) & m
            a[k] = a[k] ^ t
            a[k + j] = a[k + j] ^ lax.shift_left(t, jnp.int32(j))
            k = (k + j + 1) & ~j
        j >>= 1
        m = (m ^ (m << j)) & 0xFFFFFFFF
    return a


def _dsa_kernel(q_ref, qi_ref, w_ref, qi_next_ref, w_next_ref, k_ref, vt_ref, kk_ref, o_ref,
                key_ref, planes_ref, rhs_ref, m_ref, acc_ref, ot_ref,
                *, causal, n_keys, ksel):
    tk, st_ = KEY_TILE, SUB_TILE
    i = pl.program_id(1)
    lane = lax.broadcasted_iota(I32, (1, LANES), 1)
    lo_half = lane < HEAD_DIM
    hi_half = jnp.logical_not(lo_half)
    nk = key_ref.shape[0] // 2

    def admissible(blk):
        if causal:
            lim = CHUNK * (2 * blk + jnp.where(lo_half, jnp.int32(0), jnp.int32(1)) + 1)
            hi = QUERY_BLOCK * (blk + 1)
        else:
            lim = jnp.full((1, LANES), n_keys, I32)
            hi = n_keys
        return lim, (hi + st_ - 1) // st_

    i_next = jnp.minimum(i + 1, pl.num_programs(1) - 1)
    limit, n_sub_tiles = admissible(i)
    limit_next, n_sub_next = admissible(i_next)
    key_base = (i % 2) * nk if causal else 0
    key_base_next = nk - key_base
    sub = lax.broadcasted_iota(I32, (st_, LANES), 0)
    per_tile = tk // st_
    n_full = n_sub_tiles // per_tile
    n_rem = n_sub_tiles - n_full * per_tile

    def over_keys(body, carry):
        carry = lax.fori_loop(0, n_full, lambda t, c: body(t * per_tile, per_tile, c), carry)
        return lax.fori_loop(0, n_rem, lambda r, c: body(n_full * per_tile + r, 1, c), carry)

    def key_off(j):
        return pl.multiple_of(j * st_, st_)

    def plane_row(j):
        return pl.multiple_of(j * (st_ // 32), 8)

    def scorer(qi_blk_ref, w_blk_ref, lim, base):
        qi_blk = qi_blk_ref[0]
        qi4 = jnp.concatenate(
            [jnp.where(lo_half if h % 2 == 0 else hi_half, qi_blk[:, LANES * (h // 2):LANES * (h // 2 + 1)], 0)
             for h in range(IDX_HEADS)], axis=0)
        w = w_blk_ref[0]

        def score_tile(j0, n_sub):
            for s in range(n_sub):
                off = key_off(j0 + s)
                sc4 = _dot_nt(kk_ref[0, pl.ds(off, st_), :], qi4)
                sc = jnp.maximum(sc4[:, 0:LANES], 0.0) * w[0:1]
                for h in range(1, IDX_HEADS):
                    sc = sc + jnp.maximum(sc4[:, LANES * h:LANES * (h + 1)], 0.0) * w[h:h + 1]
                bits = pltpu.bitcast(sc, I32)
                key = jnp.where(bits < 0, bits ^ 0x7FFFFFFF, bits)
                key = jnp.where(sub + off < lim, key, INT_MIN)
                key_ref[pl.ds(pl.multiple_of(base + off, st_), st_), :] = key
                ukey = key ^ INT_MIN
                planes = _bit_transpose32([ukey[8 * r:8 * (r + 1)] for r in range(32)])
                row = plane_row(j0 + s)
                for b in range(32):
                    planes_ref[b, pl.ds(row, 8), :] = planes[b]

        return score_tile

    @pl.when(jnp.logical_and(pl.program_id(0) == 0, i == 0))
    def _():
        planes_ref[...] = jnp.zeros(planes_ref.shape, I32)

    def score_own_block():
        score_own = scorer(qi_ref, w_ref, limit, key_base)
        over_keys(lambda j0, n_sub, c: (score_own(j0, n_sub), c)[1], 0)

    if causal:
        pl.when(i == 0)(score_own_block)
    else:
        score_own_block()

    plane_rows = planes_ref.shape[1]
    group = lax.broadcasted_iota(I32, (plane_rows, LANES), 0) // (st_ // 32)
    alive0 = jnp.where(group < n_sub_tiles, jnp.int32(-1), jnp.int32(0))

    def bit_step(b, state):
        alive, above, ubits = state
        plane = planes_ref[b]
        hits = lax.population_count(alive & plane)
        cnt = jnp.sum(jnp.sum(hits.reshape(plane_rows // 8, 8, LANES), axis=0), axis=0, keepdims=True)
        take = above + cnt >= ksel
        ubits = ubits | jnp.where(take, lax.shift_left(jnp.int32(1), jnp.int32(31) - b), jnp.int32(0))
        above = jnp.where(take, above, above + cnt)
        alive = alive & (plane ^ jnp.where(take, jnp.int32(0), jnp.int32(-1)))
        return alive, above, ubits

    zero_row = jnp.zeros((1, LANES), I32)
    _, above, ubits = lax.fori_loop(0, 32, bit_step, (alive0, zero_row, zero_row))
    thr = ubits ^ INT_MIN

    need = ksel - above
    need = jnp.where(thr == INT_MIN, jnp.int32(0), need).astype(F32)
    tril = (lax.broadcasted_iota(I32, (st_, st_), 0) >= lax.broadcasted_iota(I32, (st_, st_), 1)
            ).astype(F32).astype(MXU_DTYPE)
    keep = jnp.zeros((st_, LANES), F32)
    drop = jnp.full((st_, LANES), NEG_BIG, F32)
    f_one = jnp.ones((st_, LANES), F32)

    def tile_bias(j0, n_sub, seen):
        blks = [key_ref[pl.ds(pl.multiple_of(key_base + key_off(j0 + s), st_), st_), :] for s in range(n_sub)]
        ranks = [_dot(tril, jnp.where(blk == thr, f_one, keep).astype(MXU_DTYPE)) for blk in blks]
        out = []
        for blk, rank in zip(blks, ranks):
            rank = rank + seen
            tie_bias = jnp.where(blk == thr, jnp.where(rank <= need, keep, drop), drop)
            out.append(jnp.where(blk > thr, keep, tie_bias).astype(MXU_DTYPE))
            seen = rank[st_ - 1:st_, :]
        return out, seen

    eye4 = (lax.broadcasted_iota(I32, (LANES, C_GROUP * LANES), 1) % LANES
            == lax.broadcasted_iota(I32, (LANES, C_GROUP * LANES), 0)).astype(F32).astype(MXU_DTYPE)
    for h in range(C_KV_HEADS):
        half = lo_half if h % 2 == 0 else hi_half
        q4t = jnp.concatenate(
            [jnp.transpose(jnp.where(half, q_ref[0, :, LANES * (4 * (h // 2) + g):LANES * (4 * (h // 2) + g + 1)],
                                     0).astype(F32)) for g in range(C_GROUP)], axis=1)
        rhs_ref[h] = jnp.concatenate([q4t.astype(MXU_DTYPE), eye4], axis=0)
    m_ref[...] = jnp.full(m_ref.shape, NEG_BIG, F32)
    acc_ref[...] = jnp.zeros(acc_ref.shape, F32)

    score_next = scorer(qi_next_ref, w_next_ref, limit_next, key_base_next) if causal else None

    def attn_tile(j0, n_sub, seen):
        chains = [(h, s) for s in range(n_sub) for h in range(C_KV_HEADS)]
        bias, seen = tile_bias(j0, n_sub, seen)
        if causal:
            score_next(j0, n_sub)
        scores = []
        for h, s in chains:
            kcol = slice(LANES * (h // 2), LANES * (h // 2 + 1))
            lhs = jnp.concatenate([k_ref[0, pl.ds(key_off(j0 + s), st_), kcol], bias[s]], axis=1)
            scores.append(_dot(lhs, rhs_ref[h]))
        for (h, s), sc in zip(chains, scores):
            sc = sc.astype(MXU_DTYPE)
            m_old = m_ref[h]
            m_new = jnp.maximum(m_old, jnp.max(sc, axis=0, keepdims=True).astype(F32))
            p = jnp.exp2(sc - m_new.astype(MXU_DTYPE))
            vt = vt_ref[0, V_ROWS * h:V_ROWS * (h + 1), pl.ds(key_off(j0 + s), st_)]
            acc_ref[h] = jnp.exp2(m_old - m_new) * acc_ref[h] + _dot(vt, p)
            m_ref[h] = m_new
        return seen

    over_keys(attn_tile, jnp.zeros((1, LANES), F32))
    if causal:
        lax.fori_loop(n_sub_tiles, n_sub_next, lambda j, c: (score_next(j, 1), c)[1], 0)

    for h in range(C_KV_HEADS):
        acc = acc_ref[h]
        o = acc[0:HEAD_DIM] / acc[HEAD_DIM:HEAD_DIM + 1]
        for g in range(C_GROUP):
            slot = 8 * (h // 2) + 2 * g + (h % 2)
            ot_ref[HEAD_DIM * slot:HEAD_DIM * (slot + 1), :] = o[:, LANES * g:LANES * (g + 1)]

    o_ref[0] = jnp.transpose(ot_ref[...]).astype(o_ref.dtype)


def _dsa(q, qi, w_t, k, v_t, kk, *, causal, n_keys, ksel):
    bsz, t, _ = q.shape
    nk = k.shape[1]
    qb = QUERY_BLOCK
    nqb = t // qb
    nxt = lambda qi_: jnp.minimum(qi_ + 1, nqb - 1)
    blk = lambda w_: pl.BlockSpec((1, qb, w_), lambda bi, qi_: (bi, qi_, 0))
    per_b = lambda a: pl.BlockSpec((1,) + a.shape[1:], lambda bi, qi_: (bi, 0, 0))
    kern = functools.partial(_dsa_kernel, causal=causal, n_keys=n_keys, ksel=ksel)
    return pl.pallas_call(
        kern,
        grid=(bsz, nqb),
        in_specs=[blk(1024), blk(256),
                  pl.BlockSpec((1, 8, qb), lambda bi, qi_: (bi, 0, qi_)),
                  pl.BlockSpec((1, qb, 256), lambda bi, qi_: (bi, nxt(qi_), 0)),
                  pl.BlockSpec((1, 8, qb), lambda bi, qi_: (bi, 0, nxt(qi_))),
                  per_b(k), per_b(v_t), per_b(kk)],
        out_specs=blk(1024),
        out_shape=jax.ShapeDtypeStruct((bsz, t, 1024), MXU_DTYPE),
        scratch_shapes=[pltpu.VMEM((2 * nk, LANES), I32),
                        pltpu.VMEM((32, nk // 32, LANES), I32),
                        pltpu.VMEM((C_KV_HEADS, 2 * LANES, C_GROUP * LANES), MXU_DTYPE),
                        pltpu.VMEM((C_KV_HEADS, 1, C_GROUP * LANES), F32),
                        pltpu.VMEM((C_KV_HEADS, V_ROWS, C_GROUP * LANES), F32),
                        pltpu.VMEM((1024, LANES), F32)],
        compiler_params=pltpu.CompilerParams(
            dimension_semantics=("arbitrary", "arbitrary"), vmem_limit_bytes=VMEM_LIMIT),
        name="dsa",
    )(q, qi, w_t, qi, w_t, k, v_t, kk)


def _slot_perm(head_of_slot):
    return np.concatenate([np.arange(HEAD_DIM * j, HEAD_DIM * (j + 1)) for j in head_of_slot])


_AB_Q_PERM = _slot_perm([B_GROUP * (s % 2) + s // 2 for s in range(8)])
_C_Q_PERM = _slot_perm([4 * (2 * (s // 8) + (s % 8) % 2) + (s % 8) // 2 for s in range(16)])


def _rope_tables(pos):
    half = HEAD_DIM // 2
    inv_freq = jnp.exp(-math.log(ROPE_THETA) * jnp.arange(half, dtype=F32) / half)
    ang = pos.astype(F32)[:, None] * inv_freq[None, :]
    cos, sin = jnp.cos(ang), jnp.sin(ang)
    return jnp.concatenate([cos] * 4, axis=1), jnp.concatenate([-sin, sin, -sin, sin], axis=1)


def _value_operand(v):
    bsz, n, _ = v.shape
    vt = jnp.swapaxes(v, 1, 2).reshape(bsz, C_KV_HEADS, HEAD_DIM, n)
    ones = jnp.ones((bsz, C_KV_HEADS, 1, n), v.dtype)
    zeros = jnp.zeros((bsz, C_KV_HEADS, V_ROWS - HEAD_DIM - 1, n), v.dtype)
    return jnp.concatenate([vt, ones, zeros], axis=2).reshape(bsz, C_KV_HEADS * V_ROWS, n)


def _round_up(n, m):
    return (n + m - 1) // m * m


def _trunk(x, pos, caches, wts):
    (ab_w_in, a_ln_g, a_ln_b, a_ws, a_bs, b_sinks, ab_w_out, c_w_in, c_w_out,
     ln1_g, ln1_b, ln2_g, ln2_b, ff_w1, ff_w2) = wts
    bsz, t, d = x.shape
    sample = caches is not None
    cos, sin = _rope_tables(pos)
    tm = min(512, t)
    tm_in = min(1024, t)
    row = lambda a: a.reshape(1, -1)

    w_in = jnp.concatenate([ab_w_in[0][:, :1024], ab_w_in[0][:, 1024:1536][:, _AB_Q_PERM],
                            ab_w_in[0][:, 1536:]], axis=1).astype(MXU_DTYPE)
    w_out = jnp.concatenate([ab_w_out[0][:512], ab_w_out[0][512:][_AB_Q_PERM]], axis=0).astype(MXU_DTYPE)
    u, va, q, k, v = _ab_in(x.reshape(bsz * t, d), w_in, cos, sin, row(a_ln_g[0]), row(a_ln_b[0]), t=t, tm=tm_in)
    r3 = lambda a: a.reshape(bsz, t, a.shape[-1])
    u, va, q, k, v = r3(u), r3(va), r3(q), r3(k), r3(v)
    cs = min(A_CHUNK, t)
    ws = a_ws[0][:, :cs, :cs]
    bias = jnp.repeat(a_bs[0][:, :cs].T, HEAD_DIM, axis=1)
    if sample:
        hist_k = caches[0][0].reshape(bsz, B_WINDOW, 128)
        hist_v = caches[1][0].reshape(bsz, B_WINDOW, 128)
        rows = t
    else:
        hist_k, hist_v = k, v
        rows = tm_in
    mixed = _ab_mix(b_sinks[0], u, va, q, k, v, hist_k, hist_v, ws, bias,
                    rows=rows, cs=cs, hist_from_self=not sample)
    x = _post(x.reshape(bsz * t, d), mixed.reshape(bsz * t, d), w_out, row(ln1_g[0]), row(ln1_b[0]),
              ff_w1[0].astype(MXU_DTYPE), ff_w2[0].astype(MXU_DTYPE),
              row(ln2_g[0]), row(ln2_b[0]), tm=tm).reshape(bsz, t, d)
    b_k = k.reshape(bsz, t, B_KV_HEADS, HEAD_DIM)
    b_v = v.reshape(bsz, t, B_KV_HEADS, HEAD_DIM)

    cw = c_w_in[0]
    w_in = jnp.concatenate(
        [cw[:, :1024][:, _C_Q_PERM], cw[:, 1024:1792], cw[:, 1792:1856], cw[:, 1792:1856], cw[:, 1856:1860],
         jnp.zeros((d, LANES - IDX_HEADS), F32)], axis=1).astype(MXU_DTYPE)
    w_out = c_w_out[0][_C_Q_PERM].astype(MXU_DTYPE)
    q, k, kb, v, qi, kk, kkb, wi = _c_in(x.reshape(bsz * t, d), w_in, cos, sin, t=t, tm=tm_in)
    q, k, kb, v, qi, kk, kkb, wi = (r3(a) for a in (q, k, kb, v, qi, kk, kkb, wi))
    ki = kk[:, :, :IDX_DIM]
    w_t = jnp.swapaxes(wi[:, :, :8], 1, 2) * (IDX_DIM ** -0.5 * IDX_HEADS ** -0.5)
    if sample:
        keys_k = jnp.concatenate([caches[2][0].reshape(bsz, -1, 256).astype(MXU_DTYPE), kb], axis=1)
        keys_v = jnp.concatenate([caches[3][0].reshape(bsz, -1, 256), v], axis=1).astype(MXU_DTYPE)
        ci = caches[4][0].astype(MXU_DTYPE)
        keys_i = jnp.concatenate([jnp.concatenate([ci, ci], axis=-1), kkb], axis=1)
        n_keys = keys_k.shape[1]
        pad_k = _round_up(n_keys, SUB_TILE) - n_keys
        keys_k, keys_v, keys_i = (jnp.pad(a, ((0, 0), (0, pad_k), (0, 0))) for a in (keys_k, keys_v, keys_i))
        pad_q = QUERY_BLOCK - t
        padq = lambda a: jnp.pad(a, ((0, 0), (0, pad_q), (0, 0)))
        mixed = _dsa(padq(q), padq(qi), jnp.pad(w_t, ((0, 0), (0, 0), (0, pad_q))), keys_k,
                     _value_operand(keys_v), keys_i,
                     causal=False, n_keys=n_keys, ksel=min(TOPK_MAX, n_keys // 4))[:, :t]
    else:
        mixed = _dsa(q, qi, w_t, kb, _value_operand(v.astype(MXU_DTYPE)), kkb,
                     causal=True, n_keys=t, ksel=min(TOPK_MAX, t // 4))
    x = _post(x.reshape(bsz * t, d), mixed.reshape(bsz * t, d), w_out, row(ln1_g[1]), row(ln1_b[1]),
              ff_w1[1].astype(MXU_DTYPE), ff_w2[1].astype(MXU_DTYPE),
              row(ln2_g[1]), row(ln2_b[1]), tm=tm).reshape(bsz, t, d)
    c_k = k.reshape(bsz, t, C_KV_HEADS, HEAD_DIM)
    c_v = v.reshape(bsz, t, C_KV_HEADS, HEAD_DIM)
    return x, va[None], b_k[None], b_v[None], c_k[None], c_v[None], ki[None]


def kernel(x_prompt, x_sample, cache_b_k, cache_b_v, cache_c_k, cache_c_v, cache_c_idx, ab_w_in, a_ln_g, a_ln_b, a_ws, a_bs, b_sinks, ab_w_out, c_w_in, c_w_out, ln1_g, ln1_b, ln2_g, ln2_b, ff_w1, ff_w2):
    wts = (ab_w_in, a_ln_g, a_ln_b, a_ws, a_bs, b_sinks, ab_w_out, c_w_in, c_w_out,
           ln1_g, ln1_b, ln2_g, ln2_b, ff_w1, ff_w2)
    past_len = cache_c_k.shape[2]
    pos_p = jnp.arange(x_prompt.shape[1], dtype=jnp.int32)
    pos_s = past_len + jnp.arange(x_sample.shape[1], dtype=jnp.int32)
    y_p, _, p_b_k, p_b_v, p_c_k, p_c_v, p_c_idx = _trunk(x_prompt, pos_p, None, wts)
    y_s, s_a_v, s_b_k, s_b_v, s_c_k, s_c_v, s_c_idx = _trunk(
        x_sample, pos_s, (cache_b_k, cache_b_v, cache_c_k, cache_c_v, cache_c_idx), wts)
    return (y_p, y_s, p_b_k[:, :, -B_WINDOW:], p_b_v[:, :, -B_WINDOW:], p_c_k, p_c_v, p_c_idx,
            s_a_v, s_b_k, s_b_v, s_c_k, s_c_v, s_c_idx)
```

```python
import functools
import math

import numpy as np
import jax
import jax.numpy as jnp
from jax import lax
from jax.experimental import pallas as pl
from jax.experimental.pallas import tpu as pltpu

F32 = jnp.float32
I32 = jnp.int32
MXU_DTYPE = jnp.bfloat16

HEAD_DIM = 64
CHUNK = 64
ROPE_THETA = 10000.0
LN_EPS = 1e-5
DEPTH = 2
ALPHA = (2.0 * DEPTH) ** 0.25
A_GROUPS = 8
A_CHUNK = 128
B_KV_HEADS = 2
B_GROUP = 4
B_WINDOW = 128
C_KV_HEADS = 4
C_GROUP = 4
IDX_HEADS = 4
IDX_DIM = 64
TOPK_MAX = 256

LANES = 128
KEY_TILE = 1024
SUB_TILE = 256
QUERY_BLOCK = 128
INT_MIN = -(2 ** 31)
NEG_BIG = -(2.0 ** 100)
LOG2E = math.log2(math.e)
V_ROWS = 80
VMEM_LIMIT = 56 * 1024 * 1024


def _gelu(x):
    c = math.sqrt(2.0 / math.pi)
    return 0.5 * x * (1.0 + jnp.tanh(c * (x + 0.044715 * (x * x * x))))


def _ln(z, g, b):
    mu = jnp.mean(z, axis=-1, keepdims=True)
    d = z - mu
    var = jnp.mean(d * d, axis=-1, keepdims=True)
    return d * lax.rsqrt(var + LN_EPS) * g + b


def _rope2(x, cos, sin):
    lane = lax.broadcasted_iota(I32, (1, LANES), 1)
    first = (lane % HEAD_DIM) < (HEAD_DIM // 2)
    swapped = jnp.where(first, pltpu.roll(x, LANES - HEAD_DIM // 2, 1), pltpu.roll(x, HEAD_DIM // 2, 1))
    return x * cos + swapped * sin


def _dot(a, b):
    return jnp.dot(a, b, preferred_element_type=F32)


def _dot_nt(a, b):
    return lax.dot_general(a, b, (((1,), (1,)), ((), ())), preferred_element_type=F32)


def _ab_in_kernel(x_ref, w_ref, cos_ref, sin_ref, g_ref, b_ref, u_ref, va_ref, q_ref, k_ref, v_ref):
    y = _dot(x_ref[...].astype(MXU_DTYPE), w_ref[...])
    u_ref[...] = _gelu(y[:, 0:512]).astype(u_ref.dtype)
    va_ref[...] = _ln(_gelu(y[:, 512:1024]), g_ref[...], b_ref[...]).astype(va_ref.dtype)
    cos = cos_ref[...]
    sin = sin_ref[...]
    for c in range(4):
        lo = 1024 + LANES * c
        q_ref[:, LANES * c:LANES * (c + 1)] = (
            _rope2(y[:, lo:lo + LANES], cos, sin) * (HEAD_DIM ** -0.5)).astype(q_ref.dtype)
    k_ref[...] = _rope2(y[:, 1536:1664], cos, sin)
    v_ref[...] = y[:, 1664:1792]


def _ab_in(x2d, w, cos, sin, g, b, *, t, tm, act_dtype):
    rows = x2d.shape[0]
    nt = t // tm
    row_spec = lambda w_: pl.BlockSpec((tm, w_), lambda i: (i, 0))
    full = lambda a: pl.BlockSpec(a.shape, lambda i: (0,) * a.ndim)
    tab = pl.BlockSpec((tm, LANES), lambda i: (i % nt, 0))
    return pl.pallas_call(
        _ab_in_kernel,
        grid=(rows // tm,),
        in_specs=[row_spec(1024), full(w), tab, tab, full(g), full(b)],
        out_specs=[row_spec(512), row_spec(512), row_spec(512), row_spec(128), row_spec(128)],
        out_shape=[
            jax.ShapeDtypeStruct((rows, 512), act_dtype),
            jax.ShapeDtypeStruct((rows, 512), act_dtype),
            jax.ShapeDtypeStruct((rows, 512), MXU_DTYPE),
            jax.ShapeDtypeStruct((rows, 128), F32),
            jax.ShapeDtypeStruct((rows, 128), F32),
        ],
        compiler_params=pltpu.CompilerParams(
            dimension_semantics=("arbitrary",), vmem_limit_bytes=VMEM_LIMIT),
        name="ab_in",
    )(x2d, w, cos, sin, g, b)


def _ab_mix_kernel(sink_ref, u_ref, va_ref, q_ref, k_ref, v_ref, hk_ref, hv_ref,
                   ws_ref, bias_ref, o_ref, *, rows, cs, mask_first):
    t = pl.program_id(1)
    lane = lax.broadcasted_iota(I32, (1, LANES), 1)
    lo_half = lane < HEAD_DIM

    r_i = lax.broadcasted_iota(I32, (cs, cs), 0)
    c_i = lax.broadcasted_iota(I32, (cs, cs), 1)
    tril = r_i >= c_i
    w_tril = [jnp.where(tril, ws_ref[g], 0.0).astype(MXU_DTYPE) for g in range(A_GROUPS)]
    for c in range(rows // cs):
        rs = slice(c * cs, (c + 1) * cs)
        for p in range(A_GROUPS // 2):
            ls = slice(LANES * p, LANES * (p + 1))
            vp = va_ref[0, rs, ls].astype(MXU_DTYPE)
            gate = jnp.where(lo_half, _dot(w_tril[2 * p], vp), _dot(w_tril[2 * p + 1], vp)) + bias_ref[:, ls]
            o_ref[0, rs, ls] = (u_ref[0, rs, ls] * gate).astype(o_ref.dtype)

    kcat = jnp.concatenate([hk_ref[0], k_ref[0]], axis=0).astype(MXU_DTYPE)
    vcat = jnp.concatenate([hv_ref[0], v_ref[0]], axis=0).astype(MXU_DTYPE)
    nwin = B_WINDOW + CHUNK
    head_of_row = lax.broadcasted_iota(I32, (B_GROUP * CHUNK, 1), 0) // CHUNK
    col = lax.broadcasted_iota(I32, (1, nwin), 1)
    for j in range(rows // CHUNK):
        rs = slice(CHUNK * j, CHUNK * (j + 1))
        kwin = kcat[CHUNK * j:CHUNK * j + nwin]
        vwin = vcat[CHUNK * j:CHUNK * j + nwin]
        outs = []
        for h in range(B_KV_HEADS):
            half = lo_half if h == 0 else jnp.logical_not(lo_half)
            q4 = jnp.concatenate(
                [jnp.where(half, q_ref[0, rs, LANES * g:LANES * (g + 1)], 0).astype(MXU_DTYPE)
                 for g in range(B_GROUP)], axis=0)
            s = _dot_nt(q4, kwin)
            if mask_first:
                s = jnp.where(t * rows + CHUNK * j - B_WINDOW + col >= 0, s, -jnp.inf)
            sink = jnp.zeros((B_GROUP * CHUNK, 1), F32)
            for g in range(B_GROUP):
                sink = jnp.where(head_of_row == g, sink_ref[B_GROUP * h + g], sink)
            m = jnp.maximum(jnp.max(s, axis=-1, keepdims=True), sink)
            e = jnp.exp(s - m)
            p = e / (jnp.sum(e, axis=-1, keepdims=True) + jnp.exp(sink - m))
            outs.append(_dot(p.astype(MXU_DTYPE), vwin))
        for g in range(B_GROUP):
            gs = slice(CHUNK * g, CHUNK * (g + 1))
            o_ref[0, rs, 512 + LANES * g:512 + LANES * (g + 1)] = jnp.where(
                lo_half, outs[0][gs], outs[1][gs]).astype(o_ref.dtype)


def _ab_mix(sinks, u, va, q, k, v, hist_k, hist_v, ws, bias, *, rows, cs, hist_from_self):
    bsz, t, _ = u.shape
    full = lambda a: pl.BlockSpec(a.shape, lambda bi, ti: (0,) * a.ndim)
    blk = lambda w_: pl.BlockSpec((1, rows, w_), lambda bi, ti: (bi, ti, 0))
    if hist_from_self:
        per = rows // B_WINDOW
        hist = pl.BlockSpec((1, B_WINDOW, 128), lambda bi, ti: (bi, jnp.maximum(ti * per - 1, 0), 0))
    else:
        hist = pl.BlockSpec((1, B_WINDOW, 128), lambda bi, ti: (bi, 0, 0))
    kern = functools.partial(_ab_mix_kernel, rows=rows, cs=cs, mask_first=hist_from_self)
    return pl.pallas_call(
        kern,
        grid=(bsz, t // rows),
        in_specs=[pl.BlockSpec(memory_space=pltpu.SMEM),
                  blk(512), blk(512), blk(512), blk(128), blk(128), hist, hist,
                  full(ws), full(bias)],
        out_specs=blk(1024),
        out_shape=jax.ShapeDtypeStruct((bsz, t, 1024), MXU_DTYPE),
        compiler_params=pltpu.CompilerParams(
            dimension_semantics=("arbitrary", "arbitrary"), vmem_limit_bytes=VMEM_LIMIT),
        name="ab_mix",
    )(sinks, u, va, q, k, v, hist_k, hist_v, ws, bias)


def _post_kernel(x_ref, a_ref, wo_ref, g1_ref, b1_ref, w1_ref, w2_ref, g2_ref, b2_ref, o_ref, *, ff_tile):
    x = _ln(ALPHA * x_ref[...] + _dot(a_ref[...], wo_ref[...]), g1_ref[...], b1_ref[...])
    xb = x.astype(MXU_DTYPE)
    acc = jnp.zeros(x.shape, F32)
    for c in range(w1_ref.shape[1] // ff_tile):
        h = _dot(xb, w1_ref[:, c * ff_tile:(c + 1) * ff_tile])
        h = jnp.square(jnp.maximum(h, 0.0)).astype(MXU_DTYPE)
        acc = acc + _dot(h, w2_ref[c * ff_tile:(c + 1) * ff_tile, :])
    o_ref[...] = _ln(ALPHA * x + acc, g2_ref[...], b2_ref[...])


def _post(x2d, a2d, wo, g1, b1, w1, w2, g2, b2, *, tm):
    rows, d = x2d.shape
    full = lambda a: pl.BlockSpec(a.shape, lambda i: (0,) * a.ndim, pipeline_mode=pl.Buffered(1))
    row = pl.BlockSpec((tm, d), lambda i: (i, 0))
    return pl.pallas_call(
        functools.partial(_post_kernel, ff_tile=1024),
        grid=(rows // tm,),
        in_specs=[row, row, full(wo), full(g1), full(b1), full(w1), full(w2), full(g2), full(b2)],
        out_specs=row,
        out_shape=jax.ShapeDtypeStruct((rows, d), F32),
        compiler_params=pltpu.CompilerParams(
            dimension_semantics=("arbitrary",), vmem_limit_bytes=VMEM_LIMIT),
        name="post",
    )(x2d, a2d, wo, g1, b1, w1, w2, g2, b2)


def _c_in_kernel(x_ref, w_ref, cos_ref, sin_ref, q_ref, k_ref, kb_ref, v_ref, vb_ref, qi_ref, kk_ref, kkb_ref,
                 wi_ref):
    y = _dot(x_ref[...].astype(MXU_DTYPE), w_ref[...])
    cos = cos_ref[...]
    sin = sin_ref[...]
    for c in range(8):
        q_ref[:, LANES * c:LANES * (c + 1)] = (
            _rope2(y[:, LANES * c:LANES * (c + 1)], cos, sin) * (LOG2E * HEAD_DIM ** -0.5)).astype(q_ref.dtype)
    for c in range(2):
        kr = _rope2(y[:, 1024 + LANES * c:1024 + LANES * (c + 1)], cos, sin)
        k_ref[:, LANES * c:LANES * (c + 1)] = kr
        kb_ref[:, LANES * c:LANES * (c + 1)] = kr.astype(kb_ref.dtype)
    v_ref[...] = y[:, 1280:1536]
    vb_ref[...] = y[:, 1280:1536].astype(vb_ref.dtype)
    for c in range(2):
        qi_ref[:, LANES * c:LANES * (c + 1)] = _rope2(
            y[:, 1536 + LANES * c:1536 + LANES * (c + 1)], cos, sin).astype(qi_ref.dtype)
    kk = _rope2(y[:, 1792:1920], cos, sin)
    kk_ref[...] = kk
    kkb_ref[...] = kk.astype(kkb_ref.dtype)
    wi_ref[...] = y[:, 1920:2048]


def _c_in(x2d, w, cos, sin, *, t, tm):
    rows = x2d.shape[0]
    nt = t // tm
    row_spec = lambda w_: pl.BlockSpec((tm, w_), lambda i: (i, 0))
    full = lambda a: pl.BlockSpec(a.shape, lambda i: (0,) * a.ndim)
    tab = pl.BlockSpec((tm, LANES), lambda i: (i % nt, 0))
    widths = [(1024, MXU_DTYPE), (256, F32), (256, MXU_DTYPE), (256, F32), (256, MXU_DTYPE), (256, MXU_DTYPE),
              (128, F32), (128, MXU_DTYPE), (128, F32)]
    return pl.pallas_call(
        _c_in_kernel,
        grid=(rows // tm,),
        in_specs=[row_spec(1024), full(w), tab, tab],
        out_specs=[row_spec(w_) for w_, _ in widths],
        out_shape=[jax.ShapeDtypeStruct((rows, w_), dt) for w_, dt in widths],
        compiler_params=pltpu.CompilerParams(
            dimension_semantics=("arbitrary",), vmem_limit_bytes=VMEM_LIMIT),
        name="c_in",
    )(x2d, w, cos, sin)


def _bit_transpose32(words):
    a = list(words)
    j, m = 16, 0x0000FFFF
    while j:
        k = 0
        while k < 32:
            t = (a[k] ^ lax.shift_right_logical(a[k + j], jnp.int32(j))) & m
            a[k] = a[k] ^ t
            a[k + j] = a[k + j] ^ lax.shift_left(t, jnp.int32(j))
            k = (k + j + 1) & ~j
        j >>= 1
        m = (m ^ (m << j)) & 0xFFFFFFFF
    return a


def _dsa_kernel(q_ref, qi_ref, w_ref, qi_next_ref, w_next_ref, k_ref, vt_ref, kk_ref, tril_ref, eye_ref, o_ref,
                key_ref, planes_ref, rhs_ref, m_ref, acc_ref, ot_ref,
                *, causal, n_keys, ksel):
    tk, st_ = KEY_TILE, SUB_TILE
    i = pl.program_id(1)
    lane = lax.broadcasted_iota(I32, (1, LANES), 1)
    lo_half = lane < HEAD_DIM
    hi_half = jnp.logical_not(lo_half)
    nk = key_ref.shape[0] // 2

    def admissible(blk):
        if causal:
            lim = CHUNK * (2 * blk + jnp.where(lo_half, jnp.int32(0), jnp.int32(1)) + 1)
            hi = QUERY_BLOCK * (blk + 1)
        else:
            lim = jnp.full((1, LANES), n_keys, I32)
            hi = n_keys
        return lim, (hi + st_ - 1) // st_

    i_next = jnp.minimum(i + 1, pl.num_programs(1) - 1)
    limit, n_sub_tiles = admissible(i)
    limit_next, n_sub_next = admissible(i_next)
    key_base = (i % 2) * nk if causal else 0
    key_base_next = nk - key_base
    sub = lax.broadcasted_iota(I32, (st_, LANES), 0)
    per_tile = tk // st_
    n_full = n_sub_tiles // per_tile
    n_rem = n_sub_tiles - n_full * per_tile

    def over_keys(body, carry):
        carry = lax.fori_loop(0, n_full, lambda t, c: body(t * per_tile, per_tile, c), carry)
        return lax.fori_loop(0, n_rem, lambda r, c: body(n_full * per_tile + r, 1, c), carry)

    def key_off(j):
        return pl.multiple_of(j * st_, st_)

    def plane_row(j):
        return pl.multiple_of(j * (st_ // 32), 8)

    def scorer(qi_blk_ref, w_blk_ref, lim, base):
        qi_blk = qi_blk_ref[0]
        qi4 = jnp.concatenate(
            [jnp.where(lo_half if h % 2 == 0 else hi_half, qi_blk[:, LANES * (h // 2):LANES * (h // 2 + 1)], 0)
             for h in range(IDX_HEADS)], axis=0)
        w = w_blk_ref[0]

        def score_tile(j0, n_sub):
            for s in range(n_sub):
                off = key_off(j0 + s)
                sc4 = _dot_nt(kk_ref[0, pl.ds(off, st_), :], qi4)
                sc = jnp.maximum(sc4[:, 0:LANES], 0.0) * w[0:1]
                for h in range(1, IDX_HEADS):
                    sc = sc + jnp.maximum(sc4[:, LANES * h:LANES * (h + 1)], 0.0) * w[h:h + 1]
                bits = pltpu.bitcast(sc, I32)
                key = jnp.where(bits < 0, bits ^ 0x7FFFFFFF, bits)
                key = jnp.where(sub + off < lim, key, INT_MIN)
                key_ref[pl.ds(pl.multiple_of(base + off, st_), st_), :] = key
                ukey = key ^ INT_MIN
                planes = _bit_transpose32([ukey[8 * r:8 * (r + 1)] for r in range(32)])
                row = plane_row(j0 + s)
                for b in range(32):
                    planes_ref[b, pl.ds(row, 8), :] = planes[b]

        return score_tile

    @pl.when(jnp.logical_and(pl.program_id(0) == 0, i == 0))
    def _():
        planes_ref[...] = jnp.zeros(planes_ref.shape, I32)

    def score_own_block():
        score_own = scorer(qi_ref, w_ref, limit, key_base)
        over_keys(lambda j0, n_sub, c: (score_own(j0, n_sub), c)[1], 0)

    if causal:
        pl.when(i == 0)(score_own_block)
    else:
        score_own_block()

    plane_rows = planes_ref.shape[1]
    group = lax.broadcasted_iota(I32, (plane_rows, LANES), 0) // (st_ // 32)
    alive0 = jnp.where(group < n_sub_tiles, jnp.int32(-1), jnp.int32(0))

    def bit_step(b, state):
        alive, above, ubits = state
        plane = planes_ref[b]
        hits = lax.population_count(alive & plane)
        cnt = jnp.sum(jnp.sum(hits.reshape(plane_rows // 8, 8, LANES), axis=0), axis=0, keepdims=True)
        take = above + cnt >= ksel
        ubits = ubits | jnp.where(take, lax.shift_left(jnp.int32(1), jnp.int32(31) - b), jnp.int32(0))
        above = jnp.where(take, above, above + cnt)
        alive = alive & (plane ^ jnp.where(take, jnp.int32(0), jnp.int32(-1)))
        return alive, above, ubits

    zero_row = jnp.zeros((1, LANES), I32)
    _, above, ubits = lax.fori_loop(0, 32, bit_step, (alive0, zero_row, zero_row))
    thr = ubits ^ INT_MIN

    need = ksel - above
    need = jnp.where(thr == INT_MIN, jnp.int32(0), need).astype(F32)
    tril = tril_ref[...]
    keep = jnp.zeros((st_, LANES), F32)
    drop = jnp.full((st_, LANES), NEG_BIG, F32)
    f_one = jnp.ones((st_, LANES), F32)

    def tile_bias(j0, n_sub, seen):
        blks = [key_ref[pl.ds(pl.multiple_of(key_base + key_off(j0 + s), st_), st_), :] for s in range(n_sub)]
        ranks = [_dot(tril, jnp.where(blk == thr, f_one, keep).astype(MXU_DTYPE)) for blk in blks]
        out = []
        for blk, rank in zip(blks, ranks):
            rank = rank + seen
            tie_bias = jnp.where(blk == thr, jnp.where(rank <= need, keep, drop), drop)
            out.append(jnp.where(blk > thr, keep, tie_bias).astype(MXU_DTYPE))
            seen = rank[st_ - 1:st_, :]
        return out, seen

    eye4 = eye_ref[...]
    for h in range(C_KV_HEADS):
        half = lo_half if h % 2 == 0 else hi_half
        q4t = jnp.concatenate(
            [jnp.transpose(jnp.where(half, q_ref[0, :, LANES * (4 * (h // 2) + g):LANES * (4 * (h // 2) + g + 1)],
                                     0).astype(F32)) for g in range(C_GROUP)], axis=1)
        rhs_ref[h] = jnp.concatenate([q4t.astype(MXU_DTYPE), eye4], axis=0)
    m_ref[...] = jnp.full(m_ref.shape, NEG_BIG, F32)
    acc_ref[...] = jnp.zeros(acc_ref.shape, F32)

    score_next = scorer(qi_next_ref, w_next_ref, limit_next, key_base_next) if causal else None

    def attn_tile(j0, n_sub, seen):
        chains = [(h, s) for s in range(n_sub) for h in range(C_KV_HEADS)]
        bias, seen = tile_bias(j0, n_sub, seen)
        if causal:
            score_next(j0, n_sub)
        scores = []
        for h, s in chains:
            kcol = slice(LANES * (h // 2), LANES * (h // 2 + 1))
            lhs = jnp.concatenate([k_ref[0, pl.ds(key_off(j0 + s), st_), kcol], bias[s]], axis=1)
            scores.append(_dot(lhs, rhs_ref[h]))
        for (h, s), sc in zip(chains, scores):
            sc = sc.astype(MXU_DTYPE)
            m_old = m_ref[h]
            m_new = jnp.maximum(m_old, jnp.max(sc, axis=0, keepdims=True).astype(F32))
            p = jnp.exp2(sc - m_new.astype(MXU_DTYPE))
            vt = vt_ref[0, V_ROWS * h:V_ROWS * (h + 1), pl.ds(key_off(j0 + s), st_)]
            acc_ref[h] = jnp.exp2(m_old - m_new) * acc_ref[h] + _dot(vt, p)
            m_ref[h] = m_new
        return seen

    over_keys(attn_tile, jnp.zeros((1, LANES), F32))
    if causal:
        lax.fori_loop(n_sub_tiles, n_sub_next, lambda j, c: (score_next(j, 1), c)[1], 0)

    for h in range(C_KV_HEADS):
        acc = acc_ref[h]
        o = acc[0:HEAD_DIM] / acc[HEAD_DIM:HEAD_DIM + 1]
        for g in range(C_GROUP):
            slot = 8 * (h // 2) + 2 * g + (h % 2)
            ot_ref[HEAD_DIM * slot:HEAD_DIM * (slot + 1), :] = o[:, LANES * g:LANES * (g + 1)]

    o_ref[0] = jnp.transpose(ot_ref[...]).astype(o_ref.dtype)


def _dsa(q, qi, w_t, k, v_t, kk, *, causal, n_keys, ksel):
    bsz, t, _ = q.shape
    nk = k.shape[1]
    qb = QUERY_BLOCK
    nqb = t // qb
    nxt = lambda qi_: jnp.minimum(qi_ + 1, nqb - 1)
    full = lambda a: pl.BlockSpec(a.shape, lambda bi, qi_: (0,) * a.ndim)
    tril = jnp.tril(jnp.ones((SUB_TILE, SUB_TILE), MXU_DTYPE))
    eye4 = jnp.tile(jnp.eye(LANES, dtype=MXU_DTYPE), (1, C_GROUP))
    blk = lambda w_: pl.BlockSpec((1, qb, w_), lambda bi, qi_: (bi, qi_, 0))
    per_b = lambda a: pl.BlockSpec((1,) + a.shape[1:], lambda bi, qi_: (bi, 0, 0))
    kern = functools.partial(_dsa_kernel, causal=causal, n_keys=n_keys, ksel=ksel)
    return pl.pallas_call(
        kern,
        grid=(bsz, nqb),
        in_specs=[blk(1024), blk(256),
                  pl.BlockSpec((1, 8, qb), lambda bi, qi_: (bi, 0, qi_)),
                  pl.BlockSpec((1, qb, 256), lambda bi, qi_: (bi, nxt(qi_), 0)),
                  pl.BlockSpec((1, 8, qb), lambda bi, qi_: (bi, 0, nxt(qi_))),
                  per_b(k), per_b(v_t), per_b(kk), full(tril), full(eye4)],
        out_specs=blk(1024),
        out_shape=jax.ShapeDtypeStruct((bsz, t, 1024), MXU_DTYPE),
        scratch_shapes=[pltpu.VMEM((2 * nk, LANES), I32),
                        pltpu.VMEM((32, nk // 32, LANES), I32),
                        pltpu.VMEM((C_KV_HEADS, 2 * LANES, C_GROUP * LANES), MXU_DTYPE),
                        pltpu.VMEM((C_KV_HEADS, 1, C_GROUP * LANES), F32),
                        pltpu.VMEM((C_KV_HEADS, V_ROWS, C_GROUP * LANES), F32),
                        pltpu.VMEM((1024, LANES), F32)],
        compiler_params=pltpu.CompilerParams(
            dimension_semantics=("arbitrary", "arbitrary"), vmem_limit_bytes=VMEM_LIMIT),
        name="dsa",
    )(q, qi, w_t, qi, w_t, k, v_t, kk, tril, eye4)


def _slot_perm(head_of_slot):
    return np.concatenate([np.arange(HEAD_DIM * j, HEAD_DIM * (j + 1)) for j in head_of_slot])


_AB_Q_PERM = _slot_perm([B_GROUP * (s % 2) + s // 2 for s in range(8)])
_C_Q_PERM = _slot_perm([4 * (2 * (s // 8) + (s % 8) % 2) + (s % 8) // 2 for s in range(16)])


def _rope_tables(pos):
    half = HEAD_DIM // 2
    inv_freq = jnp.exp(-math.log(ROPE_THETA) * jnp.arange(half, dtype=F32) / half)
    ang = pos.astype(F32)[:, None] * inv_freq[None, :]
    cos, sin = jnp.cos(ang), jnp.sin(ang)
    return jnp.concatenate([cos] * 4, axis=1), jnp.concatenate([-sin, sin, -sin, sin], axis=1)


def _value_operand(v):
    bsz, n, _ = v.shape
    vt = jnp.swapaxes(v, 1, 2).reshape(bsz, C_KV_HEADS, HEAD_DIM, n)
    ones = jnp.ones((bsz, C_KV_HEADS, 1, n), v.dtype)
    zeros = jnp.zeros((bsz, C_KV_HEADS, V_ROWS - HEAD_DIM - 1, n), v.dtype)
    return jnp.concatenate([vt, ones, zeros], axis=2).reshape(bsz, C_KV_HEADS * V_ROWS, n)


def _round_up(n, m):
    return (n + m - 1) // m * m


def _trunk(x, pos, caches, wts):
    (ab_w_in, a_ln_g, a_ln_b, a_ws, a_bs, b_sinks, ab_w_out, c_w_in, c_w_out,
     ln1_g, ln1_b, ln2_g, ln2_b, ff_w1, ff_w2) = wts
    bsz, t, d = x.shape
    sample = caches is not None
    cos, sin = _rope_tables(pos)
    tm = min(512, t)
    tm_in = min(1024, t)
    row = lambda a: a.reshape(1, -1)

    w_in = jnp.concatenate([ab_w_in[0][:, :1024], ab_w_in[0][:, 1024:1536][:, _AB_Q_PERM],
                            ab_w_in[0][:, 1536:]], axis=1).astype(MXU_DTYPE)
    w_out = jnp.concatenate([ab_w_out[0][:512], ab_w_out[0][512:][_AB_Q_PERM]], axis=0).astype(MXU_DTYPE)
    u, va, q, k, v = _ab_in(x.reshape(bsz * t, d), w_in, cos, sin, row(a_ln_g[0]), row(a_ln_b[0]), t=t, tm=tm_in,
                            act_dtype=F32 if sample else MXU_DTYPE)
    r3 = lambda a: a.reshape(bsz, t, a.shape[-1])
    u, va, q, k, v = r3(u), r3(va), r3(q), r3(k), r3(v)
    cs = min(A_CHUNK, t)
    ws = a_ws[0][:, :cs, :cs]
    bias = jnp.repeat(a_bs[0][:, :cs].T, HEAD_DIM, axis=1)
    if sample:
        hist_k = caches[0][0].reshape(bsz, B_WINDOW, 128)
        hist_v = caches[1][0].reshape(bsz, B_WINDOW, 128)
        rows = t
    else:
        hist_k, hist_v = k, v
        rows = tm_in
    mixed = _ab_mix(b_sinks[0], u, va, q, k, v, hist_k, hist_v, ws, bias,
                    rows=rows, cs=cs, hist_from_self=not sample)
    x = _post(x.reshape(bsz * t, d), mixed.reshape(bsz * t, d), w_out, row(ln1_g[0]), row(ln1_b[0]),
              ff_w1[0].astype(MXU_DTYPE), ff_w2[0].astype(MXU_DTYPE),
              row(ln2_g[0]), row(ln2_b[0]), tm=tm).reshape(bsz, t, d)
    b_k = k.reshape(bsz, t, B_KV_HEADS, HEAD_DIM)
    b_v = v.reshape(bsz, t, B_KV_HEADS, HEAD_DIM)

    cw = c_w_in[0]
    w_in = jnp.concatenate(
        [cw[:, :1024][:, _C_Q_PERM], cw[:, 1024:1792], cw[:, 1792:1856], cw[:, 1792:1856], cw[:, 1856:1860],
         jnp.zeros((d, LANES - IDX_HEADS), F32)], axis=1).astype(MXU_DTYPE)
    w_out = c_w_out[0][_C_Q_PERM].astype(MXU_DTYPE)
    q, k, kb, v, vb, qi, kk, kkb, wi = _c_in(x.reshape(bsz * t, d), w_in, cos, sin, t=t, tm=tm_in)
    q, k, kb, v, vb, qi, kk, kkb, wi = (r3(a) for a in (q, k, kb, v, vb, qi, kk, kkb, wi))
    ki = kk[:, :, :IDX_DIM]
    w_t = jnp.swapaxes(wi[:, :, :8], 1, 2) * (IDX_DIM ** -0.5 * IDX_HEADS ** -0.5)
    if sample:
        keys_k = jnp.concatenate([caches[2][0].reshape(bsz, -1, 256).astype(MXU_DTYPE), kb], axis=1)
        keys_v = jnp.concatenate([caches[3][0].reshape(bsz, -1, 256).astype(MXU_DTYPE), vb], axis=1)
        ci = caches[4][0].astype(MXU_DTYPE)
        keys_i = jnp.concatenate([jnp.concatenate([ci, ci], axis=-1), kkb], axis=1)
        n_keys = keys_k.shape[1]
        pad_k = _round_up(n_keys, SUB_TILE) - n_keys
        keys_k, keys_v, keys_i = (jnp.pad(a, ((0, 0), (0, pad_k), (0, 0))) for a in (keys_k, keys_v, keys_i))
        pad_q = QUERY_BLOCK - t
        padq = lambda a: jnp.pad(a, ((0, 0), (0, pad_q), (0, 0)))
        mixed = _dsa(padq(q), padq(qi), jnp.pad(w_t, ((0, 0), (0, 0), (0, pad_q))), keys_k,
                     _value_operand(keys_v), keys_i,
                     causal=False, n_keys=n_keys, ksel=min(TOPK_MAX, n_keys // 4))[:, :t]
    else:
        mixed = _dsa(q, qi, w_t, kb, _value_operand(vb), kkb,
                     causal=True, n_keys=t, ksel=min(TOPK_MAX, t // 4))
    x = _post(x.reshape(bsz * t, d), mixed.reshape(bsz * t, d), w_out, row(ln1_g[1]), row(ln1_b[1]),
              ff_w1[1].astype(MXU_DTYPE), ff_w2[1].astype(MXU_DTYPE),
              row(ln2_g[1]), row(ln2_b[1]), tm=tm).reshape(bsz, t, d)
    c_k = k.reshape(bsz, t, C_KV_HEADS, HEAD_DIM)
    c_v = v.reshape(bsz, t, C_KV_HEADS, HEAD_DIM)
    return x, va[None], b_k[None], b_v[None], c_k[None], c_v[None], ki[None]


def kernel(x_prompt, x_sample, cache_b_k, cache_b_v, cache_c_k, cache_c_v, cache_c_idx, ab_w_in, a_ln_g, a_ln_b, a_ws, a_bs, b_sinks, ab_w_out, c_w_in, c_w_out, ln1_g, ln1_b, ln2_g, ln2_b, ff_w1, ff_w2):
    wts = (ab_w_in, a_ln_g, a_ln_b, a_ws, a_bs, b_sinks, ab_w_out, c_w_in, c_w_out,
           ln1_g, ln1_b, ln2_g, ln2_b, ff_w1, ff_w2)
    past_len = cache_c_k.shape[2]
    pos_p = jnp.arange(x_prompt.shape[1], dtype=jnp.int32)
    pos_s = past_len + jnp.arange(x_sample.shape[1], dtype=jnp.int32)
    y_p, _, p_b_k, p_b_v, p_c_k, p_c_v, p_c_idx = _trunk(x_prompt, pos_p, None, wts)
    y_s, s_a_v, s_b_k, s_b_v, s_c_k, s_c_v, s_c_idx = _trunk(
        x_sample, pos_s, (cache_b_k, cache_b_v, cache_c_k, cache_c_v, cache_c_idx), wts)
    return (y_p, y_s, p_b_k[:, :, -B_WINDOW:], p_b_v[:, :, -B_WINDOW:], p_c_k, p_c_v, p_c_idx,
            s_a_v, s_b_k, s_b_v, s_c_k, s_c_v, s_c_idx)
```

```python
import functools
import math

import numpy as np
import jax
import jax.numpy as jnp
from jax import lax
from jax.experimental import pallas as pl
from jax.experimental.pallas import tpu as pltpu

F32 = jnp.float32
I32 = jnp.int32
MXU_DTYPE = jnp.bfloat16

HEAD_DIM = 64
CHUNK = 64
ROPE_THETA = 10000.0
LN_EPS = 1e-5
DEPTH = 2
ALPHA = (2.0 * DEPTH) ** 0.25
A_GROUPS = 8
A_CHUNK = 128
B_KV_HEADS = 2
B_GROUP = 4
B_WINDOW = 128
C_KV_HEADS = 4
C_GROUP = 4
IDX_HEADS = 4
IDX_DIM = 64
TOPK_MAX = 256

LANES = 128
KEY_TILE = 1024
SUB_TILE = 256
QUERY_BLOCK = 128
INT_MIN = -(2 ** 31)
NEG_BIG = -(2.0 ** 100)
LOG2E = math.log2(math.e)
V_ROWS = 80
VMEM_LIMIT = 56 * 1024 * 1024


def _gelu(x):
    c = math.sqrt(2.0 / math.pi)
    return 0.5 * x * (1.0 + jnp.tanh(c * (x + 0.044715 * (x * x * x))))


def _ln(z, g, b):
    mu = jnp.mean(z, axis=-1, keepdims=True)
    d = z - mu
    var = jnp.mean(d * d, axis=-1, keepdims=True)
    return d * lax.rsqrt(var + LN_EPS) * g + b


def _rope2(x, cos, sin):
    lane = lax.broadcasted_iota(I32, (1, LANES), 1)
    first = (lane % HEAD_DIM) < (HEAD_DIM // 2)
    swapped = jnp.where(first, pltpu.roll(x, LANES - HEAD_DIM // 2, 1), pltpu.roll(x, HEAD_DIM // 2, 1))
    return x * cos + swapped * sin


def _dot(a, b):
    return jnp.dot(a, b, preferred_element_type=F32)


def _dot_nt(a, b):
    return lax.dot_general(a, b, (((1,), (1,)), ((), ())), preferred_element_type=F32)


def _ab_in_kernel(x_ref, w_ref, cos_ref, sin_ref, g_ref, b_ref, u_ref, va_ref, q_ref, k_ref, v_ref):
    y = _dot(x_ref[...].astype(MXU_DTYPE), w_ref[...])
    u_ref[...] = _gelu(y[:, 0:512]).astype(u_ref.dtype)
    va_ref[...] = _ln(_gelu(y[:, 512:1024]), g_ref[...], b_ref[...]).astype(va_ref.dtype)
    cos = cos_ref[...]
    sin = sin_ref[...]
    for c in range(4):
        lo = 1024 + LANES * c
        q_ref[:, LANES * c:LANES * (c + 1)] = (
            _rope2(y[:, lo:lo + LANES], cos, sin) * (HEAD_DIM ** -0.5)).astype(q_ref.dtype)
    k_ref[...] = _rope2(y[:, 1536:1664], cos, sin)
    v_ref[...] = y[:, 1664:1792]


def _ab_in(x2d, w, cos, sin, g, b, *, t, tm, act_dtype):
    rows = x2d.shape[0]
    nt = t // tm
    row_spec = lambda w_: pl.BlockSpec((tm, w_), lambda i: (i, 0))
    full = lambda a: pl.BlockSpec(a.shape, lambda i: (0,) * a.ndim)
    tab = pl.BlockSpec((tm, LANES), lambda i: (i % nt, 0))
    return pl.pallas_call(
        _ab_in_kernel,
        grid=(rows // tm,),
        in_specs=[row_spec(1024), full(w), tab, tab, full(g), full(b)],
        out_specs=[row_spec(512), row_spec(512), row_spec(512), row_spec(128), row_spec(128)],
        out_shape=[
            jax.ShapeDtypeStruct((rows, 512), act_dtype),
            jax.ShapeDtypeStruct((rows, 512), act_dtype),
            jax.ShapeDtypeStruct((rows, 512), MXU_DTYPE),
            jax.ShapeDtypeStruct((rows, 128), F32),
            jax.ShapeDtypeStruct((rows, 128), F32),
        ],
        compiler_params=pltpu.CompilerParams(
            dimension_semantics=("arbitrary",), vmem_limit_bytes=VMEM_LIMIT),
        name="ab_in",
    )(x2d, w, cos, sin, g, b)


def _ab_mix_kernel(sink_ref, u_ref, va_ref, q_ref, k_ref, v_ref, hk_ref, hv_ref,
                   ws_ref, bias_ref, o_ref, *, rows, cs, mask_first):
    t = pl.program_id(1)
    lane = lax.broadcasted_iota(I32, (1, LANES), 1)
    lo_half = lane < HEAD_DIM

    r_i = lax.broadcasted_iota(I32, (cs, cs), 0)
    c_i = lax.broadcasted_iota(I32, (cs, cs), 1)
    tril = r_i >= c_i
    w_tril = [jnp.where(tril, ws_ref[g], 0.0).astype(MXU_DTYPE) for g in range(A_GROUPS)]
    for c in range(rows // cs):
        rs = slice(c * cs, (c + 1) * cs)
        for p in range(A_GROUPS // 2):
            ls = slice(LANES * p, LANES * (p + 1))
            vp = va_ref[0, rs, ls].astype(MXU_DTYPE)
            gate = jnp.where(lo_half, _dot(w_tril[2 * p], vp), _dot(w_tril[2 * p + 1], vp)) + bias_ref[:, ls]
            o_ref[0, rs, ls] = (u_ref[0, rs, ls] * gate).astype(o_ref.dtype)

    kcat = jnp.concatenate([hk_ref[0], k_ref[0]], axis=0).astype(MXU_DTYPE)
    vcat = jnp.concatenate([hv_ref[0], v_ref[0]], axis=0).astype(MXU_DTYPE)
    nwin = B_WINDOW + CHUNK
    head_of_row = lax.broadcasted_iota(I32, (B_GROUP * CHUNK, 1), 0) // CHUNK
    col = lax.broadcasted_iota(I32, (1, nwin), 1)
    for j in range(rows // CHUNK):
        rs = slice(CHUNK * j, CHUNK * (j + 1))
        kwin = kcat[CHUNK * j:CHUNK * j + nwin]
        vwin = vcat[CHUNK * j:CHUNK * j + nwin]
        outs = []
        for h in range(B_KV_HEADS):
            half = lo_half if h == 0 else jnp.logical_not(lo_half)
            q4 = jnp.concatenate(
                [jnp.where(half, q_ref[0, rs, LANES * g:LANES * (g + 1)], 0).astype(MXU_DTYPE)
                 for g in range(B_GROUP)], axis=0)
            s = _dot_nt(q4, kwin)
            if mask_first:
                s = jnp.where(t * rows + CHUNK * j - B_WINDOW + col >= 0, s, -jnp.inf)
            sink = jnp.zeros((B_GROUP * CHUNK, 1), F32)
            for g in range(B_GROUP):
                sink = jnp.where(head_of_row == g, sink_ref[B_GROUP * h + g], sink)
            m = jnp.maximum(jnp.max(s, axis=-1, keepdims=True), sink)
            e = jnp.exp(s - m)
            p = e / (jnp.sum(e, axis=-1, keepdims=True) + jnp.exp(sink - m))
            outs.append(_dot(p.astype(MXU_DTYPE), vwin))
        for g in range(B_GROUP):
            gs = slice(CHUNK * g, CHUNK * (g + 1))
            o_ref[0, rs, 512 + LANES * g:512 + LANES * (g + 1)] = jnp.where(
                lo_half, outs[0][gs], outs[1][gs]).astype(o_ref.dtype)


def _ab_mix(sinks, u, va, q, k, v, hist_k, hist_v, ws, bias, *, rows, cs, hist_from_self):
    bsz, t, _ = u.shape
    full = lambda a: pl.BlockSpec(a.shape, lambda bi, ti: (0,) * a.ndim)
    blk = lambda w_: pl.BlockSpec((1, rows, w_), lambda bi, ti: (bi, ti, 0))
    if hist_from_self:
        per = rows // B_WINDOW
        hist = pl.BlockSpec((1, B_WINDOW, 128), lambda bi, ti: (bi, jnp.maximum(ti * per - 1, 0), 0))
    else:
        hist = pl.BlockSpec((1, B_WINDOW, 128), lambda bi, ti: (bi, 0, 0))
    kern = functools.partial(_ab_mix_kernel, rows=rows, cs=cs, mask_first=hist_from_self)
    return pl.pallas_call(
        kern,
        grid=(bsz, t // rows),
        in_specs=[pl.BlockSpec(memory_space=pltpu.SMEM),
                  blk(512), blk(512), blk(512), blk(128), blk(128), hist, hist,
                  full(ws), full(bias)],
        out_specs=blk(1024),
        out_shape=jax.ShapeDtypeStruct((bsz, t, 1024), MXU_DTYPE),
        compiler_params=pltpu.CompilerParams(
            dimension_semantics=("arbitrary", "arbitrary"), vmem_limit_bytes=VMEM_LIMIT),
        name="ab_mix",
    )(sinks, u, va, q, k, v, hist_k, hist_v, ws, bias)


def _post_kernel(x_ref, a_ref, wo_ref, g1_ref, b1_ref, w1_ref, w2_ref, g2_ref, b2_ref, o_ref, *, ff_tile):
    x = _ln(ALPHA * x_ref[...] + _dot(a_ref[...], wo_ref[...]), g1_ref[...], b1_ref[...])
    xb = x.astype(MXU_DTYPE)
    acc = jnp.zeros(x.shape, F32)
    for c in range(w1_ref.shape[1] // ff_tile):
        h = _dot(xb, w1_ref[:, c * ff_tile:(c + 1) * ff_tile])
        h = jnp.square(jnp.maximum(h, 0.0)).astype(MXU_DTYPE)
        acc = acc + _dot(h, w2_ref[c * ff_tile:(c + 1) * ff_tile, :])
    o_ref[...] = _ln(ALPHA * x + acc, g2_ref[...], b2_ref[...])


def _post(x2d, a2d, wo, g1, b1, w1, w2, g2, b2, *, tm):
    rows, d = x2d.shape
    full = lambda a: pl.BlockSpec(a.shape, lambda i: (0,) * a.ndim, pipeline_mode=pl.Buffered(1))
    row = pl.BlockSpec((tm, d), lambda i: (i, 0))
    return pl.pallas_call(
        functools.partial(_post_kernel, ff_tile=1024),
        grid=(rows // tm,),
        in_specs=[row, row, full(wo), full(g1), full(b1), full(w1), full(w2), full(g2), full(b2)],
        out_specs=row,
        out_shape=jax.ShapeDtypeStruct((rows, d), F32),
        compiler_params=pltpu.CompilerParams(
            dimension_semantics=("arbitrary",), vmem_limit_bytes=VMEM_LIMIT),
        name="post",
    )(x2d, a2d, wo, g1, b1, w1, w2, g2, b2)


def _c_in_kernel(x_ref, w_ref, cos_ref, sin_ref, q_ref, k_ref, kb_ref, v_ref, vb_ref, qi_ref, kk_ref, kkb_ref,
                 wi_ref):
    y = _dot(x_ref[...].astype(MXU_DTYPE), w_ref[...])
    cos = cos_ref[...]
    sin = sin_ref[...]
    for c in range(8):
        q_ref[:, LANES * c:LANES * (c + 1)] = (
            _rope2(y[:, LANES * c:LANES * (c + 1)], cos, sin) * (LOG2E * HEAD_DIM ** -0.5)).astype(q_ref.dtype)
    for c in range(2):
        kr = _rope2(y[:, 1024 + LANES * c:1024 + LANES * (c + 1)], cos, sin)
        k_ref[:, LANES * c:LANES * (c + 1)] = kr
        kb_ref[:, LANES * c:LANES * (c + 1)] = kr.astype(kb_ref.dtype)
    v_ref[...] = y[:, 1280:1536]
    vb_ref[...] = y[:, 1280:1536].astype(vb_ref.dtype)
    for c in range(2):
        qi_ref[:, LANES * c:LANES * (c + 1)] = _rope2(
            y[:, 1536 + LANES * c:1536 + LANES * (c + 1)], cos, sin).astype(qi_ref.dtype)
    kk = _rope2(y[:, 1792:1920], cos, sin)
    kk_ref[...] = kk
    kkb_ref[...] = kk.astype(kkb_ref.dtype)
    wi_ref[...] = y[:, 1920:2048]


def _c_in(x2d, w, cos, sin, *, t, tm):
    rows = x2d.shape[0]
    nt = t // tm
    row_spec = lambda w_: pl.BlockSpec((tm, w_), lambda i: (i, 0))
    full = lambda a: pl.BlockSpec(a.shape, lambda i: (0,) * a.ndim)
    tab = pl.BlockSpec((tm, LANES), lambda i: (i % nt, 0))
    widths = [(1024, MXU_DTYPE), (256, F32), (256, MXU_DTYPE), (256, F32), (256, MXU_DTYPE), (256, MXU_DTYPE),
              (128, F32), (128, MXU_DTYPE), (128, F32)]
    return pl.pallas_call(
        _c_in_kernel,
        grid=(rows // tm,),
        in_specs=[row_spec(1024), full(w), tab, tab],
        out_specs=[row_spec(w_) for w_, _ in widths],
        out_shape=[jax.ShapeDtypeStruct((rows, w_), dt) for w_, dt in widths],
        compiler_params=pltpu.CompilerParams(
            dimension_semantics=("arbitrary",), vmem_limit_bytes=VMEM_LIMIT),
        name="c_in",
    )(x2d, w, cos, sin)


def _bit_transpose32(words):
    a = list(words)
    j, m = 16, 0x0000FFFF
    while j:
        k = 0
        while k < 32:
            t = (a[k] ^ lax.shift_right_logical(a[k + j], jnp.int32(j))) & m
            a[k] = a[k] ^ t
            a[k + j] = a[k + j] ^ lax.shift_left(t, jnp.int32(j))
            k = (k + j + 1) & ~j
        j >>= 1
        m = (m ^ (m << j)) & 0xFFFFFFFF
    return a


def _dsa_kernel(q_ref, qi_ref, w_ref, qi_next_ref, w_next_ref, k_ref, vt_ref, kk_ref, tril_ref, eye_ref, o_ref,
                key_ref, planes_ref, rhs_ref, m_ref, acc_ref, ot_ref,
                *, causal, n_keys, ksel):
    tk, st_ = KEY_TILE, SUB_TILE
    i = pl.program_id(1)
    lane = lax.broadcasted_iota(I32, (1, LANES), 1)
    lo_half = lane < HEAD_DIM
    hi_half = jnp.logical_not(lo_half)
    nk = key_ref.shape[0] // 2

    def admissible(blk):
        if causal:
            lim = CHUNK * (2 * blk + jnp.where(lo_half, jnp.int32(0), jnp.int32(1)) + 1)
            hi = QUERY_BLOCK * (blk + 1)
        else:
            lim = jnp.full((1, LANES), n_keys, I32)
            hi = n_keys
        return lim, (hi + st_ - 1) // st_

    i_next = jnp.minimum(i + 1, pl.num_programs(1) - 1)
    limit, n_sub_tiles = admissible(i)
    limit_next, n_sub_next = admissible(i_next)
    key_base = (i % 2) * nk if causal else 0
    key_base_next = nk - key_base
    sub = lax.broadcasted_iota(I32, (st_, LANES), 0)
    per_tile = tk // st_
    n_full = n_sub_tiles // per_tile
    n_rem = n_sub_tiles - n_full * per_tile

    def over_keys(body, carry):
        carry = lax.fori_loop(0, n_full, lambda t, c: body(t * per_tile, per_tile, c), carry)
        return lax.fori_loop(0, n_rem, lambda r, c: body(n_full * per_tile + r, 1, c), carry)

    def key_off(j):
        return pl.multiple_of(j * st_, st_)

    def plane_row(j):
        return pl.multiple_of(j * (st_ // 32), 8)

    def scorer(qi_blk_ref, w_blk_ref, lim, base):
        qi_blk = qi_blk_ref[0]
        qi4 = jnp.concatenate(
            [jnp.where(lo_half if h % 2 == 0 else hi_half, qi_blk[:, LANES * (h // 2):LANES * (h // 2 + 1)], 0)
             for h in range(IDX_HEADS)], axis=0)
        w = w_blk_ref[0]

        def score_tile(j0, n_sub):
            for s in range(n_sub):
                off = key_off(j0 + s)
                sc4 = _dot_nt(kk_ref[0, pl.ds(off, st_), :], qi4)
                sc = jnp.maximum(sc4[:, 0:LANES], 0.0) * w[0:1]
                for h in range(1, IDX_HEADS):
                    sc = sc + jnp.maximum(sc4[:, LANES * h:LANES * (h + 1)], 0.0) * w[h:h + 1]
                bits = pltpu.bitcast(sc, I32)
                key = jnp.where(bits < 0, bits ^ 0x7FFFFFFF, bits)
                key = jnp.where(sub + off < lim, key, INT_MIN)
                key_ref[pl.ds(pl.multiple_of(base + off, st_), st_), :] = key
                ukey = key ^ INT_MIN
                planes = _bit_transpose32([ukey[8 * r:8 * (r + 1)] for r in range(32)])
                row = plane_row(j0 + s)
                for b in range(32):
                    planes_ref[b, pl.ds(row, 8), :] = planes[b]

        return score_tile

    @pl.when(jnp.logical_and(pl.program_id(0) == 0, i == 0))
    def _():
        planes_ref[...] = jnp.zeros(planes_ref.shape, I32)

    def score_own_block():
        score_own = scorer(qi_ref, w_ref, limit, key_base)
        over_keys(lambda j0, n_sub, c: (score_own(j0, n_sub), c)[1], 0)

    if causal:
        pl.when(i == 0)(score_own_block)
    else:
        score_own_block()

    plane_rows = planes_ref.shape[1]
    group = lax.broadcasted_iota(I32, (plane_rows, LANES), 0) // (st_ // 32)
    alive0 = jnp.where(group < n_sub_tiles, jnp.int32(-1), jnp.int32(0))

    def bit_step(b, state):
        alive, above, ubits = state
        plane = planes_ref[b]
        hits = lax.population_count(alive & plane)
        cnt = jnp.sum(jnp.sum(hits.reshape(plane_rows // 8, 8, LANES), axis=0), axis=0, keepdims=True)
        take = above + cnt >= ksel
        ubits = ubits | jnp.where(take, lax.shift_left(jnp.int32(1), jnp.int32(31) - b), jnp.int32(0))
        above = jnp.where(take, above, above + cnt)
        alive = alive & (plane ^ jnp.where(take, jnp.int32(0), jnp.int32(-1)))
        return alive, above, ubits

    zero_row = jnp.zeros((1, LANES), I32)
    _, above, ubits = lax.fori_loop(0, 32, bit_step, (alive0, zero_row, zero_row))
    thr = ubits ^ INT_MIN

    need = ksel - above
    need = jnp.where(thr == INT_MIN, jnp.int32(0), need).astype(F32)
    tril = tril_ref[...]
    keep = jnp.zeros((st_, LANES), F32)
    drop = jnp.full((st_, LANES), NEG_BIG, F32)
    f_one = jnp.ones((st_, LANES), F32)

    def tile_bias(j0, n_sub, seen):
        blks = [key_ref[pl.ds(pl.multiple_of(key_base + key_off(j0 + s), st_), st_), :] for s in range(n_sub)]
        ranks = [_dot(tril, jnp.where(blk == thr, f_one, keep).astype(MXU_DTYPE)) for blk in blks]
        out = []
        for blk, rank in zip(blks, ranks):
            rank = rank + seen
            tie_bias = jnp.where(blk == thr, jnp.where(rank <= need, keep, drop), drop)
            out.append(jnp.where(blk > thr, keep, tie_bias).astype(MXU_DTYPE))
            seen = rank[st_ - 1:st_, :]
        return out, seen

    eye4 = eye_ref[...]
    for h in range(C_KV_HEADS):
        half = lo_half if h % 2 == 0 else hi_half
        q4t = jnp.concatenate(
            [jnp.transpose(jnp.where(half, q_ref[0, :, LANES * (4 * (h // 2) + g):LANES * (4 * (h // 2) + g + 1)],
                                     0).astype(F32)) for g in range(C_GROUP)], axis=1)
        rhs_ref[h] = jnp.concatenate([q4t.astype(MXU_DTYPE), eye4], axis=0)
    m_ref[...] = jnp.full(m_ref.shape, NEG_BIG, F32)
    acc_ref[...] = jnp.zeros(acc_ref.shape, F32)

    score_next = scorer(qi_next_ref, w_next_ref, limit_next, key_base_next) if causal else None

    def attn_tile(j0, n_sub, seen):
        chains = [(h, s) for s in range(n_sub) for h in range(C_KV_HEADS)]
        bias, seen = tile_bias(j0, n_sub, seen)
        if causal:
            score_next(j0, n_sub)
        scores = []
        for h, s in chains:
            kcol = slice(LANES * (h // 2), LANES * (h // 2 + 1))
            lhs = jnp.concatenate([k_ref[0, pl.ds(key_off(j0 + s), st_), kcol], bias[s]], axis=1)
            scores.append(_dot(lhs, rhs_ref[h]))
        for (h, s), sc in zip(chains, scores):
            sc = sc.astype(MXU_DTYPE)
            m_old = m_ref[h]
            m_new = jnp.maximum(m_old, jnp.max(sc, axis=0, keepdims=True).astype(F32))
            p = jnp.exp2(sc - m_new.astype(MXU_DTYPE))
            vt = vt_ref[0, V_ROWS * h:V_ROWS * (h + 1), pl.ds(key_off(j0 + s), st_)]
            acc_ref[h] = jnp.exp2(m_old - m_new) * acc_ref[h] + _dot(vt, p)
            m_ref[h] = m_new
        return seen

    over_keys(attn_tile, jnp.zeros((1, LANES), F32))
    if causal:
        lax.fori_loop(n_sub_tiles, n_sub_next, lambda j, c: (score_next(j, 1), c)[1], 0)

    for h in range(C_KV_HEADS):
        acc = acc_ref[h]
        o = acc[0:HEAD_DIM] / acc[HEAD_DIM:HEAD_DIM + 1]
        for g in range(C_GROUP):
            slot = 8 * (h // 2) + 2 * g + (h % 2)
            ot_ref[HEAD_DIM * slot:HEAD_DIM * (slot + 1), :] = o[:, LANES * g:LANES * (g + 1)]

    o_ref[0] = jnp.transpose(ot_ref[...]).astype(o_ref.dtype)


def _dsa(q, qi, w_t, k, v_t, kk, *, causal, n_keys, ksel):
    bsz, t, _ = q.shape
    nk = k.shape[1]
    qb = QUERY_BLOCK
    nqb = t // qb
    nxt = lambda qi_: jnp.minimum(qi_ + 1, nqb - 1)
    full = lambda a: pl.BlockSpec(a.shape, lambda bi, qi_: (0,) * a.ndim)
    tril = jnp.tril(jnp.ones((SUB_TILE, SUB_TILE), MXU_DTYPE))
    eye4 = jnp.tile(jnp.eye(LANES, dtype=MXU_DTYPE), (1, C_GROUP))
    blk = lambda w_: pl.BlockSpec((1, qb, w_), lambda bi, qi_: (bi, qi_, 0))
    per_b = lambda a: pl.BlockSpec((1,) + a.shape[1:], lambda bi, qi_: (bi, 0, 0))
    kern = functools.partial(_dsa_kernel, causal=causal, n_keys=n_keys, ksel=ksel)
    return pl.pallas_call(
        kern,
        grid=(bsz, nqb),
        in_specs=[blk(1024), blk(256),
                  pl.BlockSpec((1, 8, qb), lambda bi, qi_: (bi, 0, qi_)),
                  pl.BlockSpec((1, qb, 256), lambda bi, qi_: (bi, nxt(qi_), 0)),
                  pl.BlockSpec((1, 8, qb), lambda bi, qi_: (bi, 0, nxt(qi_))),
                  per_b(k), per_b(v_t), per_b(kk), full(tril), full(eye4)],
        out_specs=blk(1024),
        out_shape=jax.ShapeDtypeStruct((bsz, t, 1024), MXU_DTYPE),
        scratch_shapes=[pltpu.VMEM((2 * nk, LANES), I32),
                        pltpu.VMEM((32, nk // 32, LANES), I32),
                        pltpu.VMEM((C_KV_HEADS, 2 * LANES, C_GROUP * LANES), MXU_DTYPE),
                        pltpu.VMEM((C_KV_HEADS, 1, C_GROUP * LANES), F32),
                        pltpu.VMEM((C_KV_HEADS, V_ROWS, C_GROUP * LANES), F32),
                        pltpu.VMEM((1024, LANES), F32)],
        compiler_params=pltpu.CompilerParams(
            dimension_semantics=("arbitrary", "arbitrary"), vmem_limit_bytes=VMEM_LIMIT),
        name="dsa",
    )(q, qi, w_t, qi, w_t, k, v_t, kk, tril, eye4)


def _slot_perm(head_of_slot):
    return np.concatenate([np.arange(HEAD_DIM * j, HEAD_DIM * (j + 1)) for j in head_of_slot])


_AB_Q_PERM = _slot_perm([B_GROUP * (s % 2) + s // 2 for s in range(8)])
_C_Q_PERM = _slot_perm([4 * (2 * (s // 8) + (s % 8) % 2) + (s % 8) // 2 for s in range(16)])


def _rope_tables(pos):
    half = HEAD_DIM // 2
    inv_freq = jnp.exp(-math.log(ROPE_THETA) * jnp.arange(half, dtype=F32) / half)
    ang = pos.astype(F32)[:, None] * inv_freq[None, :]
    cos, sin = jnp.cos(ang), jnp.sin(ang)
    return jnp.concatenate([cos] * 4, axis=1), jnp.concatenate([-sin, sin, -sin, sin], axis=1)


def _value_operand(v):
    bsz, n, _ = v.shape
    vt = jnp.swapaxes(v, 1, 2).reshape(bsz, C_KV_HEADS, HEAD_DIM, n)
    ones = jnp.ones((bsz, C_KV_HEADS, 1, n), v.dtype)
    zeros = jnp.zeros((bsz, C_KV_HEADS, V_ROWS - HEAD_DIM - 1, n), v.dtype)
    return jnp.concatenate([vt, ones, zeros], axis=2).reshape(bsz, C_KV_HEADS * V_ROWS, n)


def _round_up(n, m):
    return (n + m - 1) // m * m


def _trunk(x, pos, caches, wts):
    (ab_w_in, a_ln_g, a_ln_b, a_ws, a_bs, b_sinks, ab_w_out, c_w_in, c_w_out,
     ln1_g, ln1_b, ln2_g, ln2_b, ff_w1, ff_w2) = wts
    bsz, t, d = x.shape
    sample = caches is not None
    cos, sin = _rope_tables(pos)
    t_tab = t
    if sample:
        cos, sin, t_tab = jnp.tile(cos, (bsz, 1)), jnp.tile(sin, (bsz, 1)), bsz * t
    tm = min(512, bsz * t)
    tm_in = min(1024, t_tab)
    row = lambda a: a.reshape(1, -1)

    w_in = jnp.concatenate([ab_w_in[0][:, :1024], ab_w_in[0][:, 1024:1536][:, _AB_Q_PERM],
                            ab_w_in[0][:, 1536:]], axis=1).astype(MXU_DTYPE)
    w_out = jnp.concatenate([ab_w_out[0][:512], ab_w_out[0][512:][_AB_Q_PERM]], axis=0).astype(MXU_DTYPE)
    u, va, q, k, v = _ab_in(x.reshape(bsz * t, d), w_in, cos, sin, row(a_ln_g[0]), row(a_ln_b[0]), t=t_tab, tm=tm_in,
                            act_dtype=F32 if sample else MXU_DTYPE)
    r3 = lambda a: a.reshape(bsz, t, a.shape[-1])
    u, va, q, k, v = r3(u), r3(va), r3(q), r3(k), r3(v)
    cs = min(A_CHUNK, t)
    ws = a_ws[0][:, :cs, :cs]
    bias = jnp.repeat(a_bs[0][:, :cs].T, HEAD_DIM, axis=1)
    if sample:
        hist_k = caches[0][0].reshape(bsz, B_WINDOW, 128)
        hist_v = caches[1][0].reshape(bsz, B_WINDOW, 128)
        rows = t
    else:
        hist_k, hist_v = k, v
        rows = tm_in
    mixed = _ab_mix(b_sinks[0], u, va, q, k, v, hist_k, hist_v, ws, bias,
                    rows=rows, cs=cs, hist_from_self=not sample)
    x = _post(x.reshape(bsz * t, d), mixed.reshape(bsz * t, d), w_out, row(ln1_g[0]), row(ln1_b[0]),
              ff_w1[0].astype(MXU_DTYPE), ff_w2[0].astype(MXU_DTYPE),
              row(ln2_g[0]), row(ln2_b[0]), tm=tm).reshape(bsz, t, d)
    b_k = k.reshape(bsz, t, B_KV_HEADS, HEAD_DIM)
    b_v = v.reshape(bsz, t, B_KV_HEADS, HEAD_DIM)

    cw = c_w_in[0]
    w_in = jnp.concatenate(
        [cw[:, :1024][:, _C_Q_PERM], cw[:, 1024:1792], cw[:, 1792:1856], cw[:, 1792:1856], cw[:, 1856:1860],
         jnp.zeros((d, LANES - IDX_HEADS), F32)], axis=1).astype(MXU_DTYPE)
    w_out = c_w_out[0][_C_Q_PERM].astype(MXU_DTYPE)
    q, k, kb, v, vb, qi, kk, kkb, wi = _c_in(x.reshape(bsz * t, d), w_in, cos, sin, t=t_tab, tm=tm_in)
    q, k, kb, v, vb, qi, kk, kkb, wi = (r3(a) for a in (q, k, kb, v, vb, qi, kk, kkb, wi))
    ki = kk[:, :, :IDX_DIM]
    w_t = jnp.swapaxes(wi[:, :, :8], 1, 2) * (IDX_DIM ** -0.5 * IDX_HEADS ** -0.5)
    if sample:
        keys_k = jnp.concatenate([caches[2][0].reshape(bsz, -1, 256).astype(MXU_DTYPE), kb], axis=1)
        keys_v = jnp.concatenate([caches[3][0].reshape(bsz, -1, 256).astype(MXU_DTYPE), vb], axis=1)
        ci = caches[4][0].astype(MXU_DTYPE)
        keys_i = jnp.concatenate([jnp.concatenate([ci, ci], axis=-1), kkb], axis=1)
        n_keys = keys_k.shape[1]
        pad_k = _round_up(n_keys, SUB_TILE) - n_keys
        keys_k, keys_v, keys_i = (jnp.pad(a, ((0, 0), (0, pad_k), (0, 0))) for a in (keys_k, keys_v, keys_i))
        pad_q = QUERY_BLOCK - t
        padq = lambda a: jnp.pad(a, ((0, 0), (0, pad_q), (0, 0)))
        mixed = _dsa(padq(q), padq(qi), jnp.pad(w_t, ((0, 0), (0, 0), (0, pad_q))), keys_k,
                     _value_operand(keys_v), keys_i,
                     causal=False, n_keys=n_keys, ksel=min(TOPK_MAX, n_keys // 4))[:, :t]
    else:
        mixed = _dsa(q, qi, w_t, kb, _value_operand(vb), kkb,
                     causal=True, n_keys=t, ksel=min(TOPK_MAX, t // 4))
    x = _post(x.reshape(bsz * t, d), mixed.reshape(bsz * t, d), w_out, row(ln1_g[1]), row(ln1_b[1]),
              ff_w1[1].astype(MXU_DTYPE), ff_w2[1].astype(MXU_DTYPE),
              row(ln2_g[1]), row(ln2_b[1]), tm=tm).reshape(bsz, t, d)
    c_k = k.reshape(bsz, t, C_KV_HEADS, HEAD_DIM)
    c_v = v.reshape(bsz, t, C_KV_HEADS, HEAD_DIM)
    return x, va[None], b_k[None], b_v[None], c_k[None], c_v[None], ki[None]


def kernel(x_prompt, x_sample, cache_b_k, cache_b_v, cache_c_k, cache_c_v, cache_c_idx, ab_w_in, a_ln_g, a_ln_b, a_ws, a_bs, b_sinks, ab_w_out, c_w_in, c_w_out, ln1_g, ln1_b, ln2_g, ln2_b, ff_w1, ff_w2):
    wts = (ab_w_in, a_ln_g, a_ln_b, a_ws, a_bs, b_sinks, ab_w_out, c_w_in, c_w_out,
           ln1_g, ln1_b, ln2_g, ln2_b, ff_w1, ff_w2)
    past_len = cache_c_k.shape[2]
    pos_p = jnp.arange(x_prompt.shape[1], dtype=jnp.int32)
    pos_s = past_len + jnp.arange(x_sample.shape[1], dtype=jnp.int32)
    y_p, _, p_b_k, p_b_v, p_c_k, p_c_v, p_c_idx = _trunk(x_prompt, pos_p, None, wts)
    y_s, s_a_v, s_b_k, s_b_v, s_c_k, s_c_v, s_c_idx = _trunk(
        x_sample, pos_s, (cache_b_k, cache_b_v, cache_c_k, cache_c_v, cache_c_idx), wts)
    return (y_p, y_s, p_b_k[:, :, -B_WINDOW:], p_b_v[:, :, -B_WINDOW:], p_c_k, p_c_v, p_c_idx,
            s_a_v, s_b_k, s_b_v, s_c_k, s_c_v, s_c_idx)
```

```python
import functools
import math

import numpy as np
import jax
import jax.numpy as jnp
from jax import lax
from jax.experimental import pallas as pl
from jax.experimental.pallas import tpu as pltpu

F32 = jnp.float32
I32 = jnp.int32
MXU_DTYPE = jnp.bfloat16

HEAD_DIM = 64
CHUNK = 64
ROPE_THETA = 10000.0
LN_EPS = 1e-5
DEPTH = 2
ALPHA = (2.0 * DEPTH) ** 0.25
A_GROUPS = 8
A_CHUNK = 128
B_KV_HEADS = 2
B_GROUP = 4
B_WINDOW = 128
C_KV_HEADS = 4
C_GROUP = 4
IDX_HEADS = 4
IDX_DIM = 64
TOPK_MAX = 256

LANES = 128
KEY_TILE = 1024
SUB_TILE = 256
QUERY_BLOCK = 128
INT_MIN = -(2 ** 31)
NEG_BIG = -(2.0 ** 100)
LOG2E = math.log2(math.e)
V_ROWS = 80
VMEM_LIMIT = 56 * 1024 * 1024


def _gelu(x):
    c = math.sqrt(2.0 / math.pi)
    return 0.5 * x * (1.0 + jnp.tanh(c * (x + 0.044715 * (x * x * x))))


def _ln(z, g, b):
    mu = jnp.mean(z, axis=-1, keepdims=True)
    d = z - mu
    var = jnp.mean(d * d, axis=-1, keepdims=True)
    return d * lax.rsqrt(var + LN_EPS) * g + b


def _rope2(x, cos, sin):
    lane = lax.broadcasted_iota(I32, (1, LANES), 1)
    first = (lane % HEAD_DIM) < (HEAD_DIM // 2)
    swapped = jnp.where(first, pltpu.roll(x, LANES - HEAD_DIM // 2, 1), pltpu.roll(x, HEAD_DIM // 2, 1))
    return x * cos + swapped * sin


def _dot(a, b):
    return jnp.dot(a, b, preferred_element_type=F32)


def _dot_nt(a, b):
    return lax.dot_general(a, b, (((1,), (1,)), ((), ())), preferred_element_type=F32)


def _ab_in_kernel(x_ref, w_ref, cos_ref, sin_ref, g_ref, b_ref, u_ref, va_ref, q_ref, k_ref, v_ref):
    y = _dot(x_ref[...].astype(MXU_DTYPE), w_ref[...])
    u_ref[...] = _gelu(y[:, 0:512]).astype(u_ref.dtype)
    va_ref[...] = _ln(_gelu(y[:, 512:1024]), g_ref[...], b_ref[...]).astype(va_ref.dtype)
    cos = cos_ref[...]
    sin = sin_ref[...]
    for c in range(4):
        lo = 1024 + LANES * c
        q_ref[:, LANES * c:LANES * (c + 1)] = (
            _rope2(y[:, lo:lo + LANES], cos, sin) * (HEAD_DIM ** -0.5)).astype(q_ref.dtype)
    k_ref[...] = _rope2(y[:, 1536:1664], cos, sin)
    v_ref[...] = y[:, 1664:1792]


def _ab_in(x2d, w, cos, sin, g, b, *, t, tm, act_dtype):
    rows = x2d.shape[0]
    nt = t // tm
    row_spec = lambda w_: pl.BlockSpec((tm, w_), lambda i: (i, 0))
    full = lambda a: pl.BlockSpec(a.shape, lambda i: (0,) * a.ndim)
    tab = pl.BlockSpec((tm, LANES), lambda i: (i % nt, 0))
    return pl.pallas_call(
        _ab_in_kernel,
        grid=(rows // tm,),
        in_specs=[row_spec(1024), full(w), tab, tab, full(g), full(b)],
        out_specs=[row_spec(512), row_spec(512), row_spec(512), row_spec(128), row_spec(128)],
        out_shape=[
            jax.ShapeDtypeStruct((rows, 512), act_dtype),
            jax.ShapeDtypeStruct((rows, 512), act_dtype),
            jax.ShapeDtypeStruct((rows, 512), MXU_DTYPE),
            jax.ShapeDtypeStruct((rows, 128), F32),
            jax.ShapeDtypeStruct((rows, 128), F32),
        ],
        compiler_params=pltpu.CompilerParams(
            dimension_semantics=("arbitrary",), vmem_limit_bytes=VMEM_LIMIT),
        name="ab_in",
    )(x2d, w, cos, sin, g, b)


def _ab_mix_kernel(sink_ref, u_ref, va_ref, q_ref, k_ref, v_ref, hk_ref, hv_ref,
                   ws_ref, bias_ref, o_ref, *, rows, cs, mask_first):
    t = pl.program_id(1)
    lane = lax.broadcasted_iota(I32, (1, LANES), 1)
    lo_half = lane < HEAD_DIM

    r_i = lax.broadcasted_iota(I32, (cs, cs), 0)
    c_i = lax.broadcasted_iota(I32, (cs, cs), 1)
    tril = r_i >= c_i
    w_tril = [jnp.where(tril, ws_ref[g], 0.0).astype(MXU_DTYPE) for g in range(A_GROUPS)]
    for c in range(rows // cs):
        rs = slice(c * cs, (c + 1) * cs)
        for p in range(A_GROUPS // 2):
            ls = slice(LANES * p, LANES * (p + 1))
            vp = va_ref[0, rs, ls].astype(MXU_DTYPE)
            gate = jnp.where(lo_half, _dot(w_tril[2 * p], vp), _dot(w_tril[2 * p + 1], vp)) + bias_ref[:, ls]
            o_ref[0, rs, ls] = (u_ref[0, rs, ls] * gate).astype(o_ref.dtype)

    kcat = jnp.concatenate([hk_ref[0], k_ref[0]], axis=0).astype(MXU_DTYPE)
    vcat = jnp.concatenate([hv_ref[0], v_ref[0]], axis=0).astype(MXU_DTYPE)
    nwin = B_WINDOW + CHUNK
    head_of_row = lax.broadcasted_iota(I32, (B_GROUP * CHUNK, 1), 0) // CHUNK
    col = lax.broadcasted_iota(I32, (1, nwin), 1)
    for j in range(rows // CHUNK):
        rs = slice(CHUNK * j, CHUNK * (j + 1))
        kwin = kcat[CHUNK * j:CHUNK * j + nwin]
        vwin = vcat[CHUNK * j:CHUNK * j + nwin]
        outs = []
        for h in range(B_KV_HEADS):
            half = lo_half if h == 0 else jnp.logical_not(lo_half)
            q4 = jnp.concatenate(
                [jnp.where(half, q_ref[0, rs, LANES * g:LANES * (g + 1)], 0).astype(MXU_DTYPE)
                 for g in range(B_GROUP)], axis=0)
            s = _dot_nt(q4, kwin)
            if mask_first:
                s = jnp.where(t * rows + CHUNK * j - B_WINDOW + col >= 0, s, -jnp.inf)
            sink = jnp.zeros((B_GROUP * CHUNK, 1), F32)
            for g in range(B_GROUP):
                sink = jnp.where(head_of_row == g, sink_ref[B_GROUP * h + g], sink)
            m = jnp.maximum(jnp.max(s, axis=-1, keepdims=True), sink)
            e = jnp.exp(s - m)
            p = e / (jnp.sum(e, axis=-1, keepdims=True) + jnp.exp(sink - m))
            outs.append(_dot(p.astype(MXU_DTYPE), vwin))
        for g in range(B_GROUP):
            gs = slice(CHUNK * g, CHUNK * (g + 1))
            o_ref[0, rs, 512 + LANES * g:512 + LANES * (g + 1)] = jnp.where(
                lo_half, outs[0][gs], outs[1][gs]).astype(o_ref.dtype)


def _ab_mix(sinks, u, va, q, k, v, hist_k, hist_v, ws, bias, *, rows, cs, hist_from_self):
    bsz, t, _ = u.shape
    full = lambda a: pl.BlockSpec(a.shape, lambda bi, ti: (0,) * a.ndim)
    blk = lambda w_: pl.BlockSpec((1, rows, w_), lambda bi, ti: (bi, ti, 0))
    if hist_from_self:
        per = rows // B_WINDOW
        hist = pl.BlockSpec((1, B_WINDOW, 128), lambda bi, ti: (bi, jnp.maximum(ti * per - 1, 0), 0))
    else:
        hist = pl.BlockSpec((1, B_WINDOW, 128), lambda bi, ti: (bi, 0, 0))
    kern = functools.partial(_ab_mix_kernel, rows=rows, cs=cs, mask_first=hist_from_self)
    return pl.pallas_call(
        kern,
        grid=(bsz, t // rows),
        in_specs=[pl.BlockSpec(memory_space=pltpu.SMEM),
                  blk(512), blk(512), blk(512), blk(128), blk(128), hist, hist,
                  full(ws), full(bias)],
        out_specs=blk(1024),
        out_shape=jax.ShapeDtypeStruct((bsz, t, 1024), MXU_DTYPE),
        compiler_params=pltpu.CompilerParams(
            dimension_semantics=("arbitrary", "arbitrary"), vmem_limit_bytes=VMEM_LIMIT),
        name="ab_mix",
    )(sinks, u, va, q, k, v, hist_k, hist_v, ws, bias)


def _post_kernel(x_ref, a_ref, wo_ref, g1_ref, b1_ref, w1_ref, w2_ref, g2_ref, b2_ref, o_ref, *, ff_tile):
    x = _ln(ALPHA * x_ref[...] + _dot(a_ref[...], wo_ref[...]), g1_ref[...], b1_ref[...])
    xb = x.astype(MXU_DTYPE)
    acc = jnp.zeros(x.shape, F32)
    for c in range(w1_ref.shape[1] // ff_tile):
        h = _dot(xb, w1_ref[:, c * ff_tile:(c + 1) * ff_tile])
        h = jnp.square(jnp.maximum(h, 0.0)).astype(MXU_DTYPE)
        acc = acc + _dot(h, w2_ref[c * ff_tile:(c + 1) * ff_tile, :])
    o_ref[...] = _ln(ALPHA * x + acc, g2_ref[...], b2_ref[...])


def _post(x2d, a2d, wo, g1, b1, w1, w2, g2, b2, *, tm):
    rows, d = x2d.shape
    full = lambda a: pl.BlockSpec(a.shape, lambda i: (0,) * a.ndim, pipeline_mode=pl.Buffered(1))
    row = pl.BlockSpec((tm, d), lambda i: (i, 0))
    return pl.pallas_call(
        functools.partial(_post_kernel, ff_tile=1024),
        grid=(rows // tm,),
        in_specs=[row, row, full(wo), full(g1), full(b1), full(w1), full(w2), full(g2), full(b2)],
        out_specs=row,
        out_shape=jax.ShapeDtypeStruct((rows, d), F32),
        compiler_params=pltpu.CompilerParams(
            dimension_semantics=("arbitrary",), vmem_limit_bytes=VMEM_LIMIT),
        name="post",
    )(x2d, a2d, wo, g1, b1, w1, w2, g2, b2)


def _c_in_kernel(x_ref, w_ref, cos_ref, sin_ref, q_ref, k_ref, kb_ref, v_ref, vb_ref, qi_ref, kk_ref, kkb_ref,
                 wi_ref):
    y = _dot(x_ref[...].astype(MXU_DTYPE), w_ref[...])
    cos = cos_ref[...]
    sin = sin_ref[...]
    for c in range(8):
        q_ref[:, LANES * c:LANES * (c + 1)] = (
            _rope2(y[:, LANES * c:LANES * (c + 1)], cos, sin) * (LOG2E * HEAD_DIM ** -0.5)).astype(q_ref.dtype)
    for c in range(2):
        kr = _rope2(y[:, 1024 + LANES * c:1024 + LANES * (c + 1)], cos, sin)
        k_ref[:, LANES * c:LANES * (c + 1)] = kr
        kb_ref[:, LANES * c:LANES * (c + 1)] = kr.astype(kb_ref.dtype)
    v_ref[...] = y[:, 1280:1536]
    vb_ref[...] = y[:, 1280:1536].astype(vb_ref.dtype)
    for c in range(2):
        qi_ref[:, LANES * c:LANES * (c + 1)] = _rope2(
            y[:, 1536 + LANES * c:1536 + LANES * (c + 1)], cos, sin).astype(qi_ref.dtype)
    kk = _rope2(y[:, 1792:1920], cos, sin)
    kk_ref[...] = kk
    kkb_ref[...] = kk.astype(kkb_ref.dtype)
    wi_ref[...] = y[:, 1920:2048]


def _c_in(x2d, w, cos, sin, *, t, tm):
    rows = x2d.shape[0]
    nt = t // tm
    row_spec = lambda w_: pl.BlockSpec((tm, w_), lambda i: (i, 0))
    full = lambda a: pl.BlockSpec(a.shape, lambda i: (0,) * a.ndim)
    tab = pl.BlockSpec((tm, LANES), lambda i: (i % nt, 0))
    widths = [(1024, MXU_DTYPE), (256, F32), (256, MXU_DTYPE), (256, F32), (256, MXU_DTYPE), (256, MXU_DTYPE),
              (128, F32), (128, MXU_DTYPE), (128, F32)]
    return pl.pallas_call(
        _c_in_kernel,
        grid=(rows // tm,),
        in_specs=[row_spec(1024), full(w), tab, tab],
        out_specs=[row_spec(w_) for w_, _ in widths],
        out_shape=[jax.ShapeDtypeStruct((rows, w_), dt) for w_, dt in widths],
        compiler_params=pltpu.CompilerParams(
            dimension_semantics=("arbitrary",), vmem_limit_bytes=VMEM_LIMIT),
        name="c_in",
    )(x2d, w, cos, sin)


def _bit_transpose32(words):
    a = list(words)
    j, m = 16, 0x0000FFFF
    while j:
        k = 0
        while k < 32:
            t = (a[k] ^ lax.shift_right_logical(a[k + j], jnp.int32(j))) & m
            a[k] = a[k] ^ t
            a[k + j] = a[k + j] ^ lax.shift_left(t, jnp.int32(j))
            k = (k + j + 1) & ~j
        j >>= 1
        m = (m ^ (m << j)) & 0xFFFFFFFF
    return a


def _dsa_kernel(q_ref, qi_ref, w_ref, qi_next_ref, w_next_ref, k_ref, vt_ref, kk_ref, tril_ref, eye_ref, o_ref,
                key_ref, planes_ref, rhs_ref, m_ref, acc_ref, ot_ref,
                *, causal, n_keys, ksel):
    tk, st_ = KEY_TILE, SUB_TILE
    i = pl.program_id(1)
    lane = lax.broadcasted_iota(I32, (1, LANES), 1)
    lo_half = lane < HEAD_DIM
    hi_half = jnp.logical_not(lo_half)
    nk = key_ref.shape[0] // 2

    def admissible(blk):
        if causal:
            lim = CHUNK * (2 * blk + jnp.where(lo_half, jnp.int32(0), jnp.int32(1)) + 1)
            hi = QUERY_BLOCK * (blk + 1)
        else:
            lim = jnp.full((1, LANES), n_keys, I32)
            hi = n_keys
        return lim, (hi + st_ - 1) // st_

    i_next = jnp.minimum(i + 1, pl.num_programs(1) - 1)
    limit, n_sub_tiles = admissible(i)
    limit_next, n_sub_next = admissible(i_next)
    key_base = (i % 2) * nk if causal else 0
    key_base_next = nk - key_base
    sub = lax.broadcasted_iota(I32, (st_, LANES), 0)
    per_tile = tk // st_
    n_full = n_sub_tiles // per_tile
    n_rem = n_sub_tiles - n_full * per_tile

    def over_keys(body, carry):
        carry = lax.fori_loop(0, n_full, lambda t, c: body(t * per_tile, per_tile, c), carry)
        return lax.fori_loop(0, n_rem, lambda r, c: body(n_full * per_tile + r, 1, c), carry)

    def key_off(j):
        return pl.multiple_of(j * st_, st_)

    def plane_row(j):
        return pl.multiple_of(j * (st_ // 32), 8)

    def scorer(qi_blk_ref, w_blk_ref, lim, base):
        qi_blk = qi_blk_ref[0]
        qi4 = jnp.concatenate(
            [jnp.where(lo_half if h % 2 == 0 else hi_half, qi_blk[:, LANES * (h // 2):LANES * (h // 2 + 1)], 0)
             for h in range(IDX_HEADS)], axis=0)
        w = w_blk_ref[0]

        def score_tile(j0, n_sub):
            for s in range(n_sub):
                off = key_off(j0 + s)
                sc4 = _dot_nt(kk_ref[0, pl.ds(off, st_), :], qi4)
                sc = jnp.maximum(sc4[:, 0:LANES], 0.0) * w[0:1]
                for h in range(1, IDX_HEADS):
                    sc = sc + jnp.maximum(sc4[:, LANES * h:LANES * (h + 1)], 0.0) * w[h:h + 1]
                bits = pltpu.bitcast(sc, I32)
                key = jnp.where(bits < 0, bits ^ 0x7FFFFFFF, bits)
                key = jnp.where(sub + off < lim, key, INT_MIN)
                key_ref[pl.ds(pl.multiple_of(base + off, st_), st_), :] = key
                ukey = key ^ INT_MIN
                planes = _bit_transpose32([ukey[8 * r:8 * (r + 1)] for r in range(32)])
                row = plane_row(j0 + s)
                for b in range(32):
                    planes_ref[b, pl.ds(row, 8), :] = planes[b]

        return score_tile

    @pl.when(jnp.logical_and(pl.program_id(0) == 0, i == 0))
    def _():
        planes_ref[...] = jnp.zeros(planes_ref.shape, I32)

    def score_own_block():
        score_own = scorer(qi_ref, w_ref, limit, key_base)
        over_keys(lambda j0, n_sub, c: (score_own(j0, n_sub), c)[1], 0)

    if causal:
        pl.when(i == 0)(score_own_block)
    else:
        score_own_block()

    def select_kth(rows):
        def run():
            group = lax.broadcasted_iota(I32, (rows, LANES), 0) // (st_ // 32)
            alive0 = jnp.where(group < n_sub_tiles, jnp.int32(-1), jnp.int32(0))

            def bit_step(b, state):
                alive, above, ubits = state
                plane = planes_ref[b, 0:rows, :]
                hits = lax.population_count(alive & plane)
                cnt = jnp.sum(jnp.sum(hits.reshape(rows // 8, 8, LANES), axis=0), axis=0, keepdims=True)
                take = above + cnt >= ksel
                ubits = ubits | jnp.where(take, lax.shift_left(jnp.int32(1), jnp.int32(31) - b), jnp.int32(0))
                above = jnp.where(take, above, above + cnt)
                alive = alive & (plane ^ jnp.where(take, jnp.int32(0), jnp.int32(-1)))
                return alive, above, ubits

            zero_row = jnp.zeros((1, LANES), I32)
            _, above, ubits = lax.fori_loop(0, 32, bit_step, (alive0, zero_row, zero_row))
            return above, ubits
        return run

    plane_rows = planes_ref.shape[1]
    if causal and plane_rows % 4 == 0:
        quarter = plane_rows // 4
        groups_per_quarter = quarter // (st_ // 32)
        above, ubits = lax.switch((n_sub_tiles - 1) // groups_per_quarter,
                                  [select_kth(quarter * c) for c in range(1, 5)])
    else:
        above, ubits = select_kth(plane_rows)()
    thr = ubits ^ INT_MIN

    need = ksel - above
    need = jnp.where(thr == INT_MIN, jnp.int32(0), need).astype(F32)
    tril = tril_ref[...]
    keep = jnp.zeros((st_, LANES), F32)
    drop = jnp.full((st_, LANES), NEG_BIG, F32)
    f_one = jnp.ones((st_, LANES), F32)

    def tile_bias(j0, n_sub, seen):
        blks = [key_ref[pl.ds(pl.multiple_of(key_base + key_off(j0 + s), st_), st_), :] for s in range(n_sub)]
        ranks = [_dot(tril, jnp.where(blk == thr, f_one, keep).astype(MXU_DTYPE)) for blk in blks]
        out = []
        for blk, rank in zip(blks, ranks):
            rank = rank + seen
            tie_bias = jnp.where(blk == thr, jnp.where(rank <= need, keep, drop), drop)
            out.append(jnp.where(blk > thr, keep, tie_bias).astype(MXU_DTYPE))
            seen = rank[st_ - 1:st_, :]
        return out, seen

    eye4 = eye_ref[...]
    for h in range(C_KV_HEADS):
        half = lo_half if h % 2 == 0 else hi_half
        q4t = jnp.concatenate(
            [jnp.transpose(jnp.where(half, q_ref[0, :, LANES * (4 * (h // 2) + g):LANES * (4 * (h // 2) + g + 1)],
                                     0).astype(F32)) for g in range(C_GROUP)], axis=1)
        rhs_ref[h] = jnp.concatenate([q4t.astype(MXU_DTYPE), eye4], axis=0)
    m_ref[...] = jnp.full(m_ref.shape, NEG_BIG, F32)
    acc_ref[...] = jnp.zeros(acc_ref.shape, F32)

    score_next = scorer(qi_next_ref, w_next_ref, limit_next, key_base_next) if causal else None

    def attn_tile(j0, n_sub, seen):
        chains = [(h, s) for s in range(n_sub) for h in range(C_KV_HEADS)]
        bias, seen = tile_bias(j0, n_sub, seen)
        if causal:
            score_next(j0, n_sub)
        scores = []
        for h, s in chains:
            kcol = slice(LANES * (h // 2), LANES * (h // 2 + 1))
            lhs = jnp.concatenate([k_ref[0, pl.ds(key_off(j0 + s), st_), kcol], bias[s]], axis=1)
            scores.append(_dot(lhs, rhs_ref[h]))
        for (h, s), sc in zip(chains, scores):
            sc = sc.astype(MXU_DTYPE)
            m_old = m_ref[h]
            m_new = jnp.maximum(m_old, jnp.max(sc, axis=0, keepdims=True).astype(F32))
            p = jnp.exp2(sc - m_new.astype(MXU_DTYPE))
            vt = vt_ref[0, V_ROWS * h:V_ROWS * (h + 1), pl.ds(key_off(j0 + s), st_)]
            acc_ref[h] = jnp.exp2(m_old - m_new) * acc_ref[h] + _dot(vt, p)
            m_ref[h] = m_new
        return seen

    over_keys(attn_tile, jnp.zeros((1, LANES), F32))
    if causal:
        lax.fori_loop(n_sub_tiles, n_sub_next, lambda j, c: (score_next(j, 1), c)[1], 0)

    for h in range(C_KV_HEADS):
        acc = acc_ref[h]
        o = acc[0:HEAD_DIM] / acc[HEAD_DIM:HEAD_DIM + 1]
        for g in range(C_GROUP):
            slot = 8 * (h // 2) + 2 * g + (h % 2)
            ot_ref[HEAD_DIM * slot:HEAD_DIM * (slot + 1), :] = o[:, LANES * g:LANES * (g + 1)]

    o_ref[0] = jnp.transpose(ot_ref[...]).astype(o_ref.dtype)


def _dsa(q, qi, w_t, k, v_t, kk, *, causal, n_keys, ksel):
    bsz, t, _ = q.shape
    nk = k.shape[1]
    qb = QUERY_BLOCK
    nqb = t // qb
    nxt = lambda qi_: jnp.minimum(qi_ + 1, nqb - 1)
    full = lambda a: pl.BlockSpec(a.shape, lambda bi, qi_: (0,) * a.ndim)
    tril = jnp.tril(jnp.ones((SUB_TILE, SUB_TILE), MXU_DTYPE))
    eye4 = jnp.tile(jnp.eye(LANES, dtype=MXU_DTYPE), (1, C_GROUP))
    blk = lambda w_: pl.BlockSpec((1, qb, w_), lambda bi, qi_: (bi, qi_, 0))
    per_b = lambda a: pl.BlockSpec((1,) + a.shape[1:], lambda bi, qi_: (bi, 0, 0))
    kern = functools.partial(_dsa_kernel, causal=causal, n_keys=n_keys, ksel=ksel)
    return pl.pallas_call(
        kern,
        grid=(bsz, nqb),
        in_specs=[blk(1024), blk(256),
                  pl.BlockSpec((1, 8, qb), lambda bi, qi_: (bi, 0, qi_)),
                  pl.BlockSpec((1, qb, 256), lambda bi, qi_: (bi, nxt(qi_), 0)),
                  pl.BlockSpec((1, 8, qb), lambda bi, qi_: (bi, 0, nxt(qi_))),
                  per_b(k), per_b(v_t), per_b(kk), full(tril), full(eye4)],
        out_specs=blk(1024),
        out_shape=jax.ShapeDtypeStruct((bsz, t, 1024), MXU_DTYPE),
        scratch_shapes=[pltpu.VMEM((2 * nk, LANES), I32),
                        pltpu.VMEM((32, nk // 32, LANES), I32),
                        pltpu.VMEM((C_KV_HEADS, 2 * LANES, C_GROUP * LANES), MXU_DTYPE),
                        pltpu.VMEM((C_KV_HEADS, 1, C_GROUP * LANES), F32),
                        pltpu.VMEM((C_KV_HEADS, V_ROWS, C_GROUP * LANES), F32),
                        pltpu.VMEM((1024, LANES), F32)],
        compiler_params=pltpu.CompilerParams(
            dimension_semantics=("arbitrary", "arbitrary"), vmem_limit_bytes=VMEM_LIMIT),
        name="dsa",
    )(q, qi, w_t, qi, w_t, k, v_t, kk, tril, eye4)


def _slot_perm(head_of_slot):
    return np.concatenate([np.arange(HEAD_DIM * j, HEAD_DIM * (j + 1)) for j in head_of_slot])


_AB_Q_PERM = _slot_perm([B_GROUP * (s % 2) + s // 2 for s in range(8)])
_C_Q_PERM = _slot_perm([4 * (2 * (s // 8) + (s % 8) % 2) + (s % 8) // 2 for s in range(16)])


def _rope_tables(pos):
    half = HEAD_DIM // 2
    inv_freq = jnp.exp(-math.log(ROPE_THETA) * jnp.arange(half, dtype=F32) / half)
    ang = pos.astype(F32)[:, None] * inv_freq[None, :]
    cos, sin = jnp.cos(ang), jnp.sin(ang)
    return jnp.concatenate([cos] * 4, axis=1), jnp.concatenate([-sin, sin, -sin, sin], axis=1)


def _value_operand(v):
    bsz, n, _ = v.shape
    vt = jnp.swapaxes(v, 1, 2).reshape(bsz, C_KV_HEADS, HEAD_DIM, n)
    ones = jnp.ones((bsz, C_KV_HEADS, 1, n), v.dtype)
    zeros = jnp.zeros((bsz, C_KV_HEADS, V_ROWS - HEAD_DIM - 1, n), v.dtype)
    return jnp.concatenate([vt, ones, zeros], axis=2).reshape(bsz, C_KV_HEADS * V_ROWS, n)


def _round_up(n, m):
    return (n + m - 1) // m * m


def _trunk(x, pos, caches, wts):
    (ab_w_in, a_ln_g, a_ln_b, a_ws, a_bs, b_sinks, ab_w_out, c_w_in, c_w_out,
     ln1_g, ln1_b, ln2_g, ln2_b, ff_w1, ff_w2) = wts
    bsz, t, d = x.shape
    sample = caches is not None
    cos, sin = _rope_tables(pos)
    t_tab = t
    if sample:
        cos, sin, t_tab = jnp.tile(cos, (bsz, 1)), jnp.tile(sin, (bsz, 1)), bsz * t
    tm = min(512, bsz * t)
    tm_in = min(1024, t_tab)
    row = lambda a: a.reshape(1, -1)

    w_in = jnp.concatenate([ab_w_in[0][:, :1024], ab_w_in[0][:, 1024:1536][:, _AB_Q_PERM],
                            ab_w_in[0][:, 1536:]], axis=1).astype(MXU_DTYPE)
    w_out = jnp.concatenate([ab_w_out[0][:512], ab_w_out[0][512:][_AB_Q_PERM]], axis=0).astype(MXU_DTYPE)
    u, va, q, k, v = _ab_in(x.reshape(bsz * t, d), w_in, cos, sin, row(a_ln_g[0]), row(a_ln_b[0]), t=t_tab, tm=tm_in,
                            act_dtype=F32 if sample else MXU_DTYPE)
    r3 = lambda a: a.reshape(bsz, t, a.shape[-1])
    u, va, q, k, v = r3(u), r3(va), r3(q), r3(k), r3(v)
    cs = min(A_CHUNK, t)
    ws = a_ws[0][:, :cs, :cs]
    bias = jnp.repeat(a_bs[0][:, :cs].T, HEAD_DIM, axis=1)
    if sample:
        hist_k = caches[0][0].reshape(bsz, B_WINDOW, 128)
        hist_v = caches[1][0].reshape(bsz, B_WINDOW, 128)
        rows = t
    else:
        hist_k, hist_v = k, v
        rows = tm_in
    mixed = _ab_mix(b_sinks[0], u, va, q, k, v, hist_k, hist_v, ws, bias,
                    rows=rows, cs=cs, hist_from_self=not sample)
    x = _post(x.reshape(bsz * t, d), mixed.reshape(bsz * t, d), w_out, row(ln1_g[0]), row(ln1_b[0]),
              ff_w1[0].astype(MXU_DTYPE), ff_w2[0].astype(MXU_DTYPE),
              row(ln2_g[0]), row(ln2_b[0]), tm=tm).reshape(bsz, t, d)
    b_k = k.reshape(bsz, t, B_KV_HEADS, HEAD_DIM)
    b_v = v.reshape(bsz, t, B_KV_HEADS, HEAD_DIM)

    cw = c_w_in[0]
    w_in = jnp.concatenate(
        [cw[:, :1024][:, _C_Q_PERM], cw[:, 1024:1792], cw[:, 1792:1856], cw[:, 1792:1856], cw[:, 1856:1860],
         jnp.zeros((d, LANES - IDX_HEADS), F32)], axis=1).astype(MXU_DTYPE)
    w_out = c_w_out[0][_C_Q_PERM].astype(MXU_DTYPE)
    q, k, kb, v, vb, qi, kk, kkb, wi = _c_in(x.reshape(bsz * t, d), w_in, cos, sin, t=t_tab, tm=tm_in)
    q, k, kb, v, vb, qi, kk, kkb, wi = (r3(a) for a in (q, k, kb, v, vb, qi, kk, kkb, wi))
    ki = kk[:, :, :IDX_DIM]
    w_t = jnp.swapaxes(wi[:, :, :8], 1, 2) * (IDX_DIM ** -0.5 * IDX_HEADS ** -0.5)
    if sample:
        keys_k = jnp.concatenate([caches[2][0].reshape(bsz, -1, 256).astype(MXU_DTYPE), kb], axis=1)
        keys_v = jnp.concatenate([caches[3][0].reshape(bsz, -1, 256).astype(MXU_DTYPE), vb], axis=1)
        ci = caches[4][0].astype(MXU_DTYPE)
        keys_i = jnp.concatenate([jnp.concatenate([ci, ci], axis=-1), kkb], axis=1)
        n_keys = keys_k.shape[1]
        pad_k = _round_up(n_keys, SUB_TILE) - n_keys
        keys_k, keys_v, keys_i = (jnp.pad(a, ((0, 0), (0, pad_k), (0, 0))) for a in (keys_k, keys_v, keys_i))
        pad_q = QUERY_BLOCK - t
        padq = lambda a: jnp.pad(a, ((0, 0), (0, pad_q), (0, 0)))
        mixed = _dsa(padq(q), padq(qi), jnp.pad(w_t, ((0, 0), (0, 0), (0, pad_q))), keys_k,
                     _value_operand(keys_v), keys_i,
                     causal=False, n_keys=n_keys, ksel=min(TOPK_MAX, n_keys // 4))[:, :t]
    else:
        mixed = _dsa(q, qi, w_t, kb, _value_operand(vb), kkb,
                     causal=True, n_keys=t, ksel=min(TOPK_MAX, t // 4))
    x = _post(x.reshape(bsz * t, d), mixed.reshape(bsz * t, d), w_out, row(ln1_g[1]), row(ln1_b[1]),
              ff_w1[1].astype(MXU_DTYPE), ff_w2[1].astype(MXU_DTYPE),
              row(ln2_g[1]), row(ln2_b[1]), tm=tm).reshape(bsz, t, d)
    c_k = k.reshape(bsz, t, C_KV_HEADS, HEAD_DIM)
    c_v = v.reshape(bsz, t, C_KV_HEADS, HEAD_DIM)
    return x, va[None], b_k[None], b_v[None], c_k[None], c_v[None], ki[None]


def kernel(x_prompt, x_sample, cache_b_k, cache_b_v, cache_c_k, cache_c_v, cache_c_idx, ab_w_in, a_ln_g, a_ln_b, a_ws, a_bs, b_sinks, ab_w_out, c_w_in, c_w_out, ln1_g, ln1_b, ln2_g, ln2_b, ff_w1, ff_w2):
    wts = (ab_w_in, a_ln_g, a_ln_b, a_ws, a_bs, b_sinks, ab_w_out, c_w_in, c_w_out,
           ln1_g, ln1_b, ln2_g, ln2_b, ff_w1, ff_w2)
    past_len = cache_c_k.shape[2]
    pos_p = jnp.arange(x_prompt.shape[1], dtype=jnp.int32)
    pos_s = past_len + jnp.arange(x_sample.shape[1], dtype=jnp.int32)
    y_p, _, p_b_k, p_b_v, p_c_k, p_c_v, p_c_idx = _trunk(x_prompt, pos_p, None, wts)
    y_s, s_a_v, s_b_k, s_b_v, s_c_k, s_c_v, s_c_idx = _trunk(
        x_sample, pos_s, (cache_b_k, cache_b_v, cache_c_k, cache_c_v, cache_c_idx), wts)
    return (y_p, y_s, p_b_k[:, :, -B_WINDOW:], p_b_v[:, :, -B_WINDOW:], p_c_k, p_c_v, p_c_idx,
            s_a_v, s_b_k, s_b_v, s_c_k, s_c_v, s_c_idx)
```

```python
import functools
import math

import numpy as np
import jax
import jax.numpy as jnp
from jax import lax
from jax.experimental import pallas as pl
from jax.experimental.pallas import tpu as pltpu

F32 = jnp.float32
I32 = jnp.int32
MXU_DTYPE = jnp.bfloat16

HEAD_DIM = 64
CHUNK = 64
ROPE_THETA = 10000.0
LN_EPS = 1e-5
DEPTH = 2
ALPHA = (2.0 * DEPTH) ** 0.25
A_GROUPS = 8
A_CHUNK = 128
B_KV_HEADS = 2
B_GROUP = 4
B_WINDOW = 128
C_KV_HEADS = 4
C_GROUP = 4
IDX_HEADS = 4
IDX_DIM = 64
TOPK_MAX = 256

LANES = 128
KEY_TILE = 1024
SUB_TILE = 256
QUERY_BLOCK = 128
INT_MIN = -(2 ** 31)
NEG_BIG = -(2.0 ** 100)
LOG2E = math.log2(math.e)
V_ROWS = 80
VMEM_LIMIT = 56 * 1024 * 1024


def _gelu(x):
    c = math.sqrt(2.0 / math.pi)
    return 0.5 * x * (1.0 + jnp.tanh(c * (x + 0.044715 * (x * x * x))))


def _ln(z, g, b):
    mu = jnp.mean(z, axis=-1, keepdims=True)
    d = z - mu
    var = jnp.mean(d * d, axis=-1, keepdims=True)
    return d * lax.rsqrt(var + LN_EPS) * g + b


def _rope2(x, cos, sin):
    lane = lax.broadcasted_iota(I32, (1, LANES), 1)
    first = (lane % HEAD_DIM) < (HEAD_DIM // 2)
    swapped = jnp.where(first, pltpu.roll(x, LANES - HEAD_DIM // 2, 1), pltpu.roll(x, HEAD_DIM // 2, 1))
    return x * cos + swapped * sin


def _dot(a, b):
    return jnp.dot(a, b, preferred_element_type=F32)


def _dot_nt(a, b):
    return lax.dot_general(a, b, (((1,), (1,)), ((), ())), preferred_element_type=F32)


def _ab_in_kernel(x_ref, w_ref, cos_ref, sin_ref, g_ref, b_ref, u_ref, va_ref, q_ref, k_ref, v_ref):
    y = _dot(x_ref[...].astype(MXU_DTYPE), w_ref[...])
    u_ref[...] = _gelu(y[:, 0:512]).astype(u_ref.dtype)
    va_ref[...] = _ln(_gelu(y[:, 512:1024]), g_ref[...], b_ref[...]).astype(va_ref.dtype)
    cos = cos_ref[...]
    sin = sin_ref[...]
    for c in range(4):
        lo = 1024 + LANES * c
        q_ref[:, LANES * c:LANES * (c + 1)] = (
            _rope2(y[:, lo:lo + LANES], cos, sin) * (HEAD_DIM ** -0.5)).astype(q_ref.dtype)
    k_ref[...] = _rope2(y[:, 1536:1664], cos, sin)
    v_ref[...] = y[:, 1664:1792]


def _ab_in(x2d, w, cos, sin, g, b, *, t, tm, act_dtype):
    rows = x2d.shape[0]
    nt = t // tm
    row_spec = lambda w_: pl.BlockSpec((tm, w_), lambda i: (i, 0))
    full = lambda a: pl.BlockSpec(a.shape, lambda i: (0,) * a.ndim)
    tab = pl.BlockSpec((tm, LANES), lambda i: (i % nt, 0))
    return pl.pallas_call(
        _ab_in_kernel,
        grid=(rows // tm,),
        in_specs=[row_spec(1024), full(w), tab, tab, full(g), full(b)],
        out_specs=[row_spec(512), row_spec(512), row_spec(512), row_spec(128), row_spec(128)],
        out_shape=[
            jax.ShapeDtypeStruct((rows, 512), act_dtype),
            jax.ShapeDtypeStruct((rows, 512), act_dtype),
            jax.ShapeDtypeStruct((rows, 512), MXU_DTYPE),
            jax.ShapeDtypeStruct((rows, 128), F32),
            jax.ShapeDtypeStruct((rows, 128), F32),
        ],
        compiler_params=pltpu.CompilerParams(
            dimension_semantics=("arbitrary",), vmem_limit_bytes=VMEM_LIMIT),
        name="ab_in",
    )(x2d, w, cos, sin, g, b)


def _ab_mix_kernel(sink_ref, u_ref, va_ref, q_ref, k_ref, v_ref, hk_ref, hv_ref,
                   ws_ref, bias_ref, o_ref, *, rows, cs, mask_first):
    t = pl.program_id(1)
    lane = lax.broadcasted_iota(I32, (1, LANES), 1)
    lo_half = lane < HEAD_DIM

    r_i = lax.broadcasted_iota(I32, (cs, cs), 0)
    c_i = lax.broadcasted_iota(I32, (cs, cs), 1)
    tril = r_i >= c_i
    w_tril = [jnp.where(tril, ws_ref[g], 0.0).astype(MXU_DTYPE) for g in range(A_GROUPS)]
    for c in range(rows // cs):
        rs = slice(c * cs, (c + 1) * cs)
        for p in range(A_GROUPS // 2):
            ls = slice(LANES * p, LANES * (p + 1))
            vp = va_ref[0, rs, ls].astype(MXU_DTYPE)
            gate = jnp.where(lo_half, _dot(w_tril[2 * p], vp), _dot(w_tril[2 * p + 1], vp)) + bias_ref[:, ls]
            o_ref[0, rs, ls] = (u_ref[0, rs, ls] * gate).astype(o_ref.dtype)

    kcat = jnp.concatenate([hk_ref[0], k_ref[0]], axis=0).astype(MXU_DTYPE)
    vcat = jnp.concatenate([hv_ref[0], v_ref[0]], axis=0).astype(MXU_DTYPE)
    nwin = B_WINDOW + CHUNK
    head_of_row = lax.broadcasted_iota(I32, (B_GROUP * CHUNK, 1), 0) // CHUNK
    col = lax.broadcasted_iota(I32, (1, nwin), 1)
    for j in range(rows // CHUNK):
        rs = slice(CHUNK * j, CHUNK * (j + 1))
        kwin = kcat[CHUNK * j:CHUNK * j + nwin]
        vwin = vcat[CHUNK * j:CHUNK * j + nwin]
        outs = []
        for h in range(B_KV_HEADS):
            half = lo_half if h == 0 else jnp.logical_not(lo_half)
            q4 = jnp.concatenate(
                [jnp.where(half, q_ref[0, rs, LANES * g:LANES * (g + 1)], 0).astype(MXU_DTYPE)
                 for g in range(B_GROUP)], axis=0)
            s = _dot_nt(q4, kwin)
            if mask_first:
                s = jnp.where(t * rows + CHUNK * j - B_WINDOW + col >= 0, s, -jnp.inf)
            sink = jnp.zeros((B_GROUP * CHUNK, 1), F32)
            for g in range(B_GROUP):
                sink = jnp.where(head_of_row == g, sink_ref[B_GROUP * h + g], sink)
            m = jnp.maximum(jnp.max(s, axis=-1, keepdims=True), sink)
            e = jnp.exp(s - m)
            p = e / (jnp.sum(e, axis=-1, keepdims=True) + jnp.exp(sink - m))
            outs.append(_dot(p.astype(MXU_DTYPE), vwin))
        for g in range(B_GROUP):
            gs = slice(CHUNK * g, CHUNK * (g + 1))
            o_ref[0, rs, 512 + LANES * g:512 + LANES * (g + 1)] = jnp.where(
                lo_half, outs[0][gs], outs[1][gs]).astype(o_ref.dtype)


def _ab_mix(sinks, u, va, q, k, v, hist_k, hist_v, ws, bias, *, rows, cs, hist_from_self):
    bsz, t, _ = u.shape
    full = lambda a: pl.BlockSpec(a.shape, lambda bi, ti: (0,) * a.ndim)
    blk = lambda w_: pl.BlockSpec((1, rows, w_), lambda bi, ti: (bi, ti, 0))
    if hist_from_self:
        per = rows // B_WINDOW
        hist = pl.BlockSpec((1, B_WINDOW, 128), lambda bi, ti: (bi, jnp.maximum(ti * per - 1, 0), 0))
    else:
        hist = pl.BlockSpec((1, B_WINDOW, 128), lambda bi, ti: (bi, 0, 0))
    kern = functools.partial(_ab_mix_kernel, rows=rows, cs=cs, mask_first=hist_from_self)
    return pl.pallas_call(
        kern,
        grid=(bsz, t // rows),
        in_specs=[pl.BlockSpec(memory_space=pltpu.SMEM),
                  blk(512), blk(512), blk(512), blk(128), blk(128), hist, hist,
                  full(ws), full(bias)],
        out_specs=blk(1024),
        out_shape=jax.ShapeDtypeStruct((bsz, t, 1024), MXU_DTYPE),
        compiler_params=pltpu.CompilerParams(
            dimension_semantics=("arbitrary", "arbitrary"), vmem_limit_bytes=VMEM_LIMIT),
        name="ab_mix",
    )(sinks, u, va, q, k, v, hist_k, hist_v, ws, bias)


def _post_kernel(x_ref, a_ref, wo_ref, g1_ref, b1_ref, w1_ref, w2_ref, g2_ref, b2_ref, o_ref, *, ff_tile):
    x = _ln(ALPHA * x_ref[...] + _dot(a_ref[...], wo_ref[...]), g1_ref[...], b1_ref[...])
    xb = x.astype(MXU_DTYPE)
    acc = jnp.zeros(x.shape, F32)
    for c in range(w1_ref.shape[1] // ff_tile):
        h = _dot(xb, w1_ref[:, c * ff_tile:(c + 1) * ff_tile])
        h = jnp.square(jnp.maximum(h, 0.0)).astype(MXU_DTYPE)
        acc = acc + _dot(h, w2_ref[c * ff_tile:(c + 1) * ff_tile, :])
    o_ref[...] = _ln(ALPHA * x + acc, g2_ref[...], b2_ref[...])


def _post(x2d, a2d, wo, g1, b1, w1, w2, g2, b2, *, tm):
    rows, d = x2d.shape
    full = lambda a: pl.BlockSpec(a.shape, lambda i: (0,) * a.ndim, pipeline_mode=pl.Buffered(1))
    row = pl.BlockSpec((tm, d), lambda i: (i, 0))
    return pl.pallas_call(
        functools.partial(_post_kernel, ff_tile=1024),
        grid=(rows // tm,),
        in_specs=[row, row, full(wo), full(g1), full(b1), full(w1), full(w2), full(g2), full(b2)],
        out_specs=row,
        out_shape=jax.ShapeDtypeStruct((rows, d), F32),
        compiler_params=pltpu.CompilerParams(
            dimension_semantics=("arbitrary",), vmem_limit_bytes=VMEM_LIMIT),
        name="post",
    )(x2d, a2d, wo, g1, b1, w1, w2, g2, b2)


def _c_in_kernel(x_ref, w_ref, cos_ref, sin_ref, q_ref, k_ref, kb_ref, v_ref, vb_ref, qi_ref, kk_ref, kkb_ref,
                 wi_ref):
    y = _dot(x_ref[...].astype(MXU_DTYPE), w_ref[...])
    cos = cos_ref[...]
    sin = sin_ref[...]
    for c in range(8):
        q_ref[:, LANES * c:LANES * (c + 1)] = (
            _rope2(y[:, LANES * c:LANES * (c + 1)], cos, sin) * (LOG2E * HEAD_DIM ** -0.5)).astype(q_ref.dtype)
    for c in range(2):
        kr = _rope2(y[:, 1024 + LANES * c:1024 + LANES * (c + 1)], cos, sin)
        k_ref[:, LANES * c:LANES * (c + 1)] = kr
        kb_ref[:, LANES * c:LANES * (c + 1)] = kr.astype(kb_ref.dtype)
    v_ref[...] = y[:, 1280:1536]
    vb_ref[...] = y[:, 1280:1536].astype(vb_ref.dtype)
    for c in range(2):
        qi_ref[:, LANES * c:LANES * (c + 1)] = _rope2(
            y[:, 1536 + LANES * c:1536 + LANES * (c + 1)], cos, sin).astype(qi_ref.dtype)
    kk = _rope2(y[:, 1792:1920], cos, sin)
    kk_ref[...] = kk
    kkb_ref[...] = kk.astype(kkb_ref.dtype)
    wi_ref[...] = y[:, 1920:2048]


def _c_in(x2d, w, cos, sin, *, t, tm):
    rows = x2d.shape[0]
    nt = t // tm
    row_spec = lambda w_: pl.BlockSpec((tm, w_), lambda i: (i, 0))
    full = lambda a: pl.BlockSpec(a.shape, lambda i: (0,) * a.ndim)
    tab = pl.BlockSpec((tm, LANES), lambda i: (i % nt, 0))
    widths = [(1024, MXU_DTYPE), (256, F32), (256, MXU_DTYPE), (256, F32), (256, MXU_DTYPE), (256, MXU_DTYPE),
              (128, F32), (128, MXU_DTYPE), (128, F32)]
    return pl.pallas_call(
        _c_in_kernel,
        grid=(rows // tm,),
        in_specs=[row_spec(1024), full(w), tab, tab],
        out_specs=[row_spec(w_) for w_, _ in widths],
        out_shape=[jax.ShapeDtypeStruct((rows, w_), dt) for w_, dt in widths],
        compiler_params=pltpu.CompilerParams(
            dimension_semantics=("arbitrary",), vmem_limit_bytes=VMEM_LIMIT),
        name="c_in",
    )(x2d, w, cos, sin)


def _bit_transpose32(words):
    a = list(words)
    j, m = 16, 0x0000FFFF
    while j:
        k = 0
        while k < 32:
            t = (a[k] ^ lax.shift_right_logical(a[k + j], jnp.int32(j))) & m
            a[k] = a[k] ^ t
            a[k + j] = a[k + j] ^ lax.shift_left(t, jnp.int32(j))
            k = (k + j + 1) & ~j
        j >>= 1
        m = (m ^ (m << j)) & 0xFFFFFFFF
    return a


def _dsa_kernel(q_ref, qi_ref, w_ref, qi_next_ref, w_next_ref, k_ref, vt_ref, kk_ref, tril_ref, eye_ref, o_ref,
                key_ref, planes_ref, rhs_ref, m_ref, acc_ref, ot_ref,
                *, causal, n_keys, ksel):
    tk, st_ = KEY_TILE, SUB_TILE
    i = pl.program_id(1)
    lane = lax.broadcasted_iota(I32, (1, LANES), 1)
    lo_half = lane < HEAD_DIM
    hi_half = jnp.logical_not(lo_half)
    nk = key_ref.shape[0] // 2

    def admissible(blk):
        if causal:
            lim = CHUNK * (2 * blk + jnp.where(lo_half, jnp.int32(0), jnp.int32(1)) + 1)
            hi = QUERY_BLOCK * (blk + 1)
        else:
            lim = jnp.full((1, LANES), n_keys, I32)
            hi = n_keys
        return lim, (hi + st_ - 1) // st_

    i_next = jnp.minimum(i + 1, pl.num_programs(1) - 1)
    limit, n_sub_tiles = admissible(i)
    limit_next, n_sub_next = admissible(i_next)
    key_base = (i % 2) * nk if causal else 0
    key_base_next = nk - key_base
    sub = lax.broadcasted_iota(I32, (st_, LANES), 0)
    per_tile = tk // st_
    n_full = n_sub_tiles // per_tile
    n_rem = n_sub_tiles - n_full * per_tile

    def over_keys(body, carry):
        carry = lax.fori_loop(0, n_full, lambda t, c: body(t * per_tile, per_tile, c), carry)
        return lax.fori_loop(0, n_rem, lambda r, c: body(n_full * per_tile + r, 1, c), carry)

    def key_off(j):
        return pl.multiple_of(j * st_, st_)

    def plane_row(j):
        return pl.multiple_of(j * (st_ // 32), 8)

    def scorer(qi_blk_ref, w_blk_ref, lim, base):
        qi_blk = qi_blk_ref[0]
        qi4 = jnp.concatenate(
            [jnp.where(lo_half if h % 2 == 0 else hi_half, qi_blk[:, LANES * (h // 2):LANES * (h // 2 + 1)], 0)
             for h in range(IDX_HEADS)], axis=0)
        w = w_blk_ref[0]

        def score_tile(j0, n_sub):
            for s in range(n_sub):
                off = key_off(j0 + s)
                sc4 = _dot_nt(kk_ref[0, pl.ds(off, st_), :], qi4)
                sc = jnp.maximum(sc4[:, 0:LANES], 0.0) * w[0:1]
                for h in range(1, IDX_HEADS):
                    sc = sc + jnp.maximum(sc4[:, LANES * h:LANES * (h + 1)], 0.0) * w[h:h + 1]
                bits = pltpu.bitcast(sc, I32)
                key = jnp.where(bits < 0, bits ^ 0x7FFFFFFF, bits)
                key = jnp.where(sub + off < lim, key, INT_MIN)
                key_ref[pl.ds(pl.multiple_of(base + off, st_), st_), :] = key
                ukey = key ^ INT_MIN
                planes = _bit_transpose32([ukey[8 * r:8 * (r + 1)] for r in range(32)])
                row = plane_row(j0 + s)
                for b in range(32):
                    planes_ref[b, pl.ds(row, 8), :] = planes[b]

        return score_tile

    @pl.when(jnp.logical_and(pl.program_id(0) == 0, i == 0))
    def _():
        planes_ref[...] = jnp.zeros(planes_ref.shape, I32)

    def score_own_block():
        score_own = scorer(qi_ref, w_ref, limit, key_base)
        over_keys(lambda j0, n_sub, c: (score_own(j0, n_sub), c)[1], 0)

    if causal:
        pl.when(i == 0)(score_own_block)
    else:
        score_own_block()

    def select_kth(rows):
        def run():
            group = lax.broadcasted_iota(I32, (rows, LANES), 0) // (st_ // 32)
            alive0 = jnp.where(group < n_sub_tiles, jnp.int32(-1), jnp.int32(0))

            def bit_step(b, state):
                alive, above, ubits = state
                plane = planes_ref[b, 0:rows, :]
                hits = lax.population_count(alive & plane)
                cnt = jnp.sum(jnp.sum(hits.reshape(rows // 8, 8, LANES), axis=0), axis=0, keepdims=True)
                take = above + cnt >= ksel
                ubits = ubits | jnp.where(take, lax.shift_left(jnp.int32(1), jnp.int32(31) - b), jnp.int32(0))
                above = jnp.where(take, above, above + cnt)
                alive = alive & (plane ^ jnp.where(take, jnp.int32(0), jnp.int32(-1)))
                return alive, above, ubits

            zero_row = jnp.zeros((1, LANES), I32)
            _, above, ubits = lax.fori_loop(0, 32, bit_step, (alive0, zero_row, zero_row))
            return above, ubits
        return run

    plane_rows = planes_ref.shape[1]
    if causal and plane_rows % 4 == 0:
        quarter = plane_rows // 4
        groups_per_quarter = quarter // (st_ // 32)
        above, ubits = lax.switch((n_sub_tiles - 1) // groups_per_quarter,
                                  [select_kth(quarter * c) for c in range(1, 5)])
    else:
        above, ubits = select_kth(plane_rows)()
    thr = ubits ^ INT_MIN

    need = ksel - above
    need = jnp.where(thr == INT_MIN, jnp.int32(0), need).astype(F32)
    tril = tril_ref[...]
    keep = jnp.zeros((st_, LANES), F32)
    drop = jnp.full((st_, LANES), NEG_BIG, F32)
    f_one = jnp.ones((st_, LANES), F32)

    def tile_bias(j0, n_sub, seen):
        blks = [key_ref[pl.ds(pl.multiple_of(key_base + key_off(j0 + s), st_), st_), :] for s in range(n_sub)]
        ranks = [_dot(tril, jnp.where(blk == thr, f_one, keep).astype(MXU_DTYPE)) for blk in blks]
        out = []
        for blk, rank in zip(blks, ranks):
            rank = rank + seen
            tie_bias = jnp.where(blk == thr, jnp.where(rank <= need, keep, drop), drop)
            out.append(jnp.where(blk > thr, keep, tie_bias).astype(MXU_DTYPE))
            seen = rank[st_ - 1:st_, :]
        return out, seen

    eye4 = eye_ref[...]
    for h in range(C_KV_HEADS):
        half = lo_half if h % 2 == 0 else hi_half
        q4t = jnp.concatenate(
            [jnp.transpose(jnp.where(half, q_ref[0, :, LANES * (4 * (h // 2) + g):LANES * (4 * (h // 2) + g + 1)],
                                     0).astype(F32)) for g in range(C_GROUP)], axis=1)
        rhs_ref[h] = jnp.concatenate([q4t.astype(MXU_DTYPE), eye4], axis=0)
    m_ref[...] = jnp.full(m_ref.shape, NEG_BIG, F32)
    acc_ref[...] = jnp.zeros(acc_ref.shape, F32)

    score_next = scorer(qi_next_ref, w_next_ref, limit_next, key_base_next) if causal else None

    def attn_tile(j0, n_sub, seen):
        bias, seen = tile_bias(j0, n_sub, seen)
        if causal:
            score_next(j0, n_sub)
        halves = [range(0, n_sub // 2), range(n_sub // 2, n_sub)] if n_sub > 1 else [range(n_sub)]
        for part in halves:
            seen = attn_part(j0, part, bias, seen)
        return seen

    def attn_part(j0, subs, bias, seen):
        chains = [(h, s) for s in subs for h in range(C_KV_HEADS)]
        scores = []
        for h, s in chains:
            kcol = slice(LANES * (h // 2), LANES * (h // 2 + 1))
            lhs = jnp.concatenate([k_ref[0, pl.ds(key_off(j0 + s), st_), kcol], bias[s]], axis=1)
            scores.append(_dot(lhs, rhs_ref[h]))
        for (h, s), sc in zip(chains, scores):
            sc = sc.astype(MXU_DTYPE)
            m_old = m_ref[h]
            m_new = jnp.maximum(m_old, jnp.max(sc, axis=0, keepdims=True).astype(F32))
            p = jnp.exp2(sc - m_new.astype(MXU_DTYPE))
            vt = vt_ref[0, V_ROWS * h:V_ROWS * (h + 1), pl.ds(key_off(j0 + s), st_)]
            acc_ref[h] = jnp.exp2(m_old - m_new) * acc_ref[h] + _dot(vt, p)
            m_ref[h] = m_new
        return seen

    over_keys(attn_tile, jnp.zeros((1, LANES), F32))
    if causal:
        lax.fori_loop(n_sub_tiles, n_sub_next, lambda j, c: (score_next(j, 1), c)[1], 0)

    for h in range(C_KV_HEADS):
        acc = acc_ref[h]
        o = acc[0:HEAD_DIM] / acc[HEAD_DIM:HEAD_DIM + 1]
        for g in range(C_GROUP):
            slot = 8 * (h // 2) + 2 * g + (h % 2)
            ot_ref[HEAD_DIM * slot:HEAD_DIM * (slot + 1), :] = o[:, LANES * g:LANES * (g + 1)]

    o_ref[0] = jnp.transpose(ot_ref[...]).astype(o_ref.dtype)


def _dsa(q, qi, w_t, k, v_t, kk, *, causal, n_keys, ksel):
    bsz, t, _ = q.shape
    nk = k.shape[1]
    qb = QUERY_BLOCK
    nqb = t // qb
    nxt = lambda qi_: jnp.minimum(qi_ + 1, nqb - 1)
    full = lambda a: pl.BlockSpec(a.shape, lambda bi, qi_: (0,) * a.ndim)
    tril = jnp.tril(jnp.ones((SUB_TILE, SUB_TILE), MXU_DTYPE))
    eye4 = jnp.tile(jnp.eye(LANES, dtype=MXU_DTYPE), (1, C_GROUP))
    blk = lambda w_: pl.BlockSpec((1, qb, w_), lambda bi, qi_: (bi, qi_, 0))
    per_b = lambda a: pl.BlockSpec((1,) + a.shape[1:], lambda bi, qi_: (bi, 0, 0))
    kern = functools.partial(_dsa_kernel, causal=causal, n_keys=n_keys, ksel=ksel)
    return pl.pallas_call(
        kern,
        grid=(bsz, nqb),
        in_specs=[blk(1024), blk(256),
                  pl.BlockSpec((1, 8, qb), lambda bi, qi_: (bi, 0, qi_)),
                  pl.BlockSpec((1, qb, 256), lambda bi, qi_: (bi, nxt(qi_), 0)),
                  pl.BlockSpec((1, 8, qb), lambda bi, qi_: (bi, 0, nxt(qi_))),
                  per_b(k), per_b(v_t), per_b(kk), full(tril), full(eye4)],
        out_specs=blk(1024),
        out_shape=jax.ShapeDtypeStruct((bsz, t, 1024), MXU_DTYPE),
        scratch_shapes=[pltpu.VMEM((2 * nk, LANES), I32),
                        pltpu.VMEM((32, nk // 32, LANES), I32),
                        pltpu.VMEM((C_KV_HEADS, 2 * LANES, C_GROUP * LANES), MXU_DTYPE),
                        pltpu.VMEM((C_KV_HEADS, 1, C_GROUP * LANES), F32),
                        pltpu.VMEM((C_KV_HEADS, V_ROWS, C_GROUP * LANES), F32),
                        pltpu.VMEM((1024, LANES), F32)],
        compiler_params=pltpu.CompilerParams(
            dimension_semantics=("arbitrary", "arbitrary"), vmem_limit_bytes=VMEM_LIMIT),
        name="dsa",
    )(q, qi, w_t, qi, w_t, k, v_t, kk, tril, eye4)


def _slot_perm(head_of_slot):
    return np.concatenate([np.arange(HEAD_DIM * j, HEAD_DIM * (j + 1)) for j in head_of_slot])


_AB_Q_PERM = _slot_perm([B_GROUP * (s % 2) + s // 2 for s in range(8)])
_C_Q_PERM = _slot_perm([4 * (2 * (s // 8) + (s % 8) % 2) + (s % 8) // 2 for s in range(16)])


def _rope_tables(pos):
    half = HEAD_DIM // 2
    inv_freq = jnp.exp(-math.log(ROPE_THETA) * jnp.arange(half, dtype=F32) / half)
    ang = pos.astype(F32)[:, None] * inv_freq[None, :]
    cos, sin = jnp.cos(ang), jnp.sin(ang)
    return jnp.concatenate([cos] * 4, axis=1), jnp.concatenate([-sin, sin, -sin, sin], axis=1)


def _value_operand(v):
    bsz, n, _ = v.shape
    vt = jnp.swapaxes(v, 1, 2).reshape(bsz, C_KV_HEADS, HEAD_DIM, n)
    ones = jnp.ones((bsz, C_KV_HEADS, 1, n), v.dtype)
    zeros = jnp.zeros((bsz, C_KV_HEADS, V_ROWS - HEAD_DIM - 1, n), v.dtype)
    return jnp.concatenate([vt, ones, zeros], axis=2).reshape(bsz, C_KV_HEADS * V_ROWS, n)


def _round_up(n, m):
    return (n + m - 1) // m * m


def _trunk(x, pos, caches, wts):
    (ab_w_in, a_ln_g, a_ln_b, a_ws, a_bs, b_sinks, ab_w_out, c_w_in, c_w_out,
     ln1_g, ln1_b, ln2_g, ln2_b, ff_w1, ff_w2) = wts
    bsz, t, d = x.shape
    sample = caches is not None
    cos, sin = _rope_tables(pos)
    t_tab = t
    if sample:
        cos, sin, t_tab = jnp.tile(cos, (bsz, 1)), jnp.tile(sin, (bsz, 1)), bsz * t
    tm = min(512, bsz * t)
    tm_in = min(1024, t_tab)
    row = lambda a: a.reshape(1, -1)

    w_in = jnp.concatenate([ab_w_in[0][:, :1024], ab_w_in[0][:, 1024:1536][:, _AB_Q_PERM],
                            ab_w_in[0][:, 1536:]], axis=1).astype(MXU_DTYPE)
    w_out = jnp.concatenate([ab_w_out[0][:512], ab_w_out[0][512:][_AB_Q_PERM]], axis=0).astype(MXU_DTYPE)
    u, va, q, k, v = _ab_in(x.reshape(bsz * t, d), w_in, cos, sin, row(a_ln_g[0]), row(a_ln_b[0]), t=t_tab, tm=tm_in,
                            act_dtype=F32 if sample else MXU_DTYPE)
    r3 = lambda a: a.reshape(bsz, t, a.shape[-1])
    u, va, q, k, v = r3(u), r3(va), r3(q), r3(k), r3(v)
    cs = min(A_CHUNK, t)
    ws = a_ws[0][:, :cs, :cs]
    bias = jnp.repeat(a_bs[0][:, :cs].T, HEAD_DIM, axis=1)
    if sample:
        hist_k = caches[0][0].reshape(bsz, B_WINDOW, 128)
        hist_v = caches[1][0].reshape(bsz, B_WINDOW, 128)
        rows = t
    else:
        hist_k, hist_v = k, v
        rows = tm_in
    mixed = _ab_mix(b_sinks[0], u, va, q, k, v, hist_k, hist_v, ws, bias,
                    rows=rows, cs=cs, hist_from_self=not sample)
    x = _post(x.reshape(bsz * t, d), mixed.reshape(bsz * t, d), w_out, row(ln1_g[0]), row(ln1_b[0]),
              ff_w1[0].astype(MXU_DTYPE), ff_w2[0].astype(MXU_DTYPE),
              row(ln2_g[0]), row(ln2_b[0]), tm=tm).reshape(bsz, t, d)
    b_k = k.reshape(bsz, t, B_KV_HEADS, HEAD_DIM)
    b_v = v.reshape(bsz, t, B_KV_HEADS, HEAD_DIM)

    cw = c_w_in[0]
    w_in = jnp.concatenate(
        [cw[:, :1024][:, _C_Q_PERM], cw[:, 1024:1792], cw[:, 1792:1856], cw[:, 1792:1856], cw[:, 1856:1860],
         jnp.zeros((d, LANES - IDX_HEADS), F32)], axis=1).astype(MXU_DTYPE)
    w_out = c_w_out[0][_C_Q_PERM].astype(MXU_DTYPE)
    q, k, kb, v, vb, qi, kk, kkb, wi = _c_in(x.reshape(bsz * t, d), w_in, cos, sin, t=t_tab, tm=tm_in)
    q, k, kb, v, vb, qi, kk, kkb, wi = (r3(a) for a in (q, k, kb, v, vb, qi, kk, kkb, wi))
    ki = kk[:, :, :IDX_DIM]
    w_t = jnp.swapaxes(wi[:, :, :8], 1, 2) * (IDX_DIM ** -0.5 * IDX_HEADS ** -0.5)
    if sample:
        keys_k = jnp.concatenate([caches[2][0].reshape(bsz, -1, 256).astype(MXU_DTYPE), kb], axis=1)
        keys_v = jnp.concatenate([caches[3][0].reshape(bsz, -1, 256).astype(MXU_DTYPE), vb], axis=1)
        ci = caches[4][0].astype(MXU_DTYPE)
        keys_i = jnp.concatenate([jnp.concatenate([ci, ci], axis=-1), kkb], axis=1)
        n_keys = keys_k.shape[1]
        pad_k = _round_up(n_keys, SUB_TILE) - n_keys
        keys_k, keys_v, keys_i = (jnp.pad(a, ((0, 0), (0, pad_k), (0, 0))) for a in (keys_k, keys_v, keys_i))
        pad_q = QUERY_BLOCK - t
        padq = lambda a: jnp.pad(a, ((0, 0), (0, pad_q), (0, 0)))
        mixed = _dsa(padq(q), padq(qi), jnp.pad(w_t, ((0, 0), (0, 0), (0, pad_q))), keys_k,
                     _value_operand(keys_v), keys_i,
                     causal=False, n_keys=n_keys, ksel=min(TOPK_MAX, n_keys // 4))[:, :t]
    else:
        mixed = _dsa(q, qi, w_t, kb, _value_operand(vb), kkb,
                     causal=True, n_keys=t, ksel=min(TOPK_MAX, t // 4))
    x = _post(x.reshape(bsz * t, d), mixed.reshape(bsz * t, d), w_out, row(ln1_g[1]), row(ln1_b[1]),
              ff_w1[1].astype(MXU_DTYPE), ff_w2[1].astype(MXU_DTYPE),
              row(ln2_g[1]), row(ln2_b[1]), tm=tm).reshape(bsz, t, d)
    c_k = k.reshape(bsz, t, C_KV_HEADS, HEAD_DIM)
    c_v = v.reshape(bsz, t, C_KV_HEADS, HEAD_DIM)
    return x, va[None], b_k[None], b_v[None], c_k[None], c_v[None], ki[None]


def kernel(x_prompt, x_sample, cache_b_k, cache_b_v, cache_c_k, cache_c_v, cache_c_idx, ab_w_in, a_ln_g, a_ln_b, a_ws, a_bs, b_sinks, ab_w_out, c_w_in, c_w_out, ln1_g, ln1_b, ln2_g, ln2_b, ff_w1, ff_w2):
    wts = (ab_w_in, a_ln_g, a_ln_b, a_ws, a_bs, b_sinks, ab_w_out, c_w_in, c_w_out,
           ln1_g, ln1_b, ln2_g, ln2_b, ff_w1, ff_w2)
    past_len = cache_c_k.shape[2]
    pos_p = jnp.arange(x_prompt.shape[1], dtype=jnp.int32)
    pos_s = past_len + jnp.arange(x_sample.shape[1], dtype=jnp.int32)
    y_p, _, p_b_k, p_b_v, p_c_k, p_c_v, p_c_idx = _trunk(x_prompt, pos_p, None, wts)
    y_s, s_a_v, s_b_k, s_b_v, s_c_k, s_c_v, s_c_idx = _trunk(
        x_sample, pos_s, (cache_b_k, cache_b_v, cache_c_k, cache_c_v, cache_c_idx), wts)
    return (y_p, y_s, p_b_k[:, :, -B_WINDOW:], p_b_v[:, :, -B_WINDOW:], p_c_k, p_c_v, p_c_idx,
            s_a_v, s_b_k, s_b_v, s_c_k, s_c_v, s_c_idx)
```
